```python
import math
import jax, jax.numpy as jnp
from jax import lax
import numpy as np

D_MODEL = 1024
BATCH = 32
SEQ = 2048
DEPTH = 4

CTX_LEN = 256
GRID_W = 64
N_MIXERS = 4
QBLOCK = 128
ROPE_THETA = 10000.0
NORM_EPS = 1e-6
LN_EPS = 1e-5
N_MOD = 6
A_HEADS = 16
A_KV_HEADS = 4
A_HEAD_DIM = 64
B_HEADS = 8
B_HEAD_DIM = 64
C_HEADS = 4
C_KEY_DIM = D_MODEL // 2
C_VAL_DIM = D_MODEL
C_GATE_RANK = 16
C_GATE_NORMALIZER = 16.0
C_CHUNK = 64
D_HEADS = 16
D_HEAD_DIM = 64
WIN_R = 8
WIN_C = 16
FF_DIM = 2816
CONV_W = 3
DEEPNORM_ALPHA = (2 * DEPTH) ** 0.25
DEEPNORM_BETA = (8 * DEPTH) ** -0.25

kernel_name = "hybrid_diffusion_interleaved_block"


def rms_norm(x, g, eps=NORM_EPS):
    xf = x.astype(jnp.float32)
    y = xf * lax.rsqrt(jnp.mean(xf * xf, axis=-1, keepdims=True) + eps)
    return (y * g.astype(jnp.float32)).astype(x.dtype)


def layer_norm(x, g, b):
    xf = x.astype(jnp.float32)
    mu = jnp.mean(xf, axis=-1, keepdims=True)
    var = jnp.mean(jnp.square(xf - mu), axis=-1, keepdims=True)
    y = (xf - mu) * lax.rsqrt(var + LN_EPS)
    return (y * g.astype(jnp.float32) + b.astype(jnp.float32)).astype(x.dtype)


def _split_heads(x, n_heads):
    b, l, _ = x.shape
    return x.reshape(b, l, n_heads, -1).transpose(0, 2, 1, 3)


def _merge_heads(x):
    b, h, l, d = x.shape
    return x.transpose(0, 2, 1, 3).reshape(b, l, h * d)


def _softmax32(s):
    return jax.nn.softmax(s.astype(jnp.float32), axis=-1)


def axial_rope(n_tokens, head_dim):
    t = jnp.arange(n_tokens)
    row = (t // GRID_W).astype(jnp.float32)
    col = (t % GRID_W).astype(jnp.float32)
    n_freq = head_dim // 4
    inv = 1.0 / (ROPE_THETA ** (jnp.arange(n_freq, dtype=jnp.float32) / n_freq))
    ang = jnp.concatenate([row[:, None] * inv, col[:, None] * inv], axis=-1)
    return jnp.cos(ang), jnp.sin(ang)


def apply_rope(x, cos, sin):
    half = x.shape[-1] // 2
    x1 = x[..., :half].astype(jnp.float32)
    x2 = x[..., half:].astype(jnp.float32)
    return jnp.concatenate([x1 * cos - x2 * sin, x1 * sin + x2 * cos], axis=-1).astype(x.dtype)


def sweep_query_blocks(fn, qs):
    n = qs[0].shape[-2]
    nb = n // QBLOCK

    def split(a):
        a = a.reshape(a.shape[:-2] + (nb, QBLOCK, a.shape[-1]))
        return jnp.moveaxis(a, -3, 0)

    out = lax.map(lambda blk: fn(*blk), tuple(split(a) for a in qs))
    out = jnp.moveaxis(out, 0, -3)
    return out.reshape(out.shape[:-3] + (n, out.shape[-1]))


def gqa_mixer(h_lat, h_ctx, w_qkv, q_norm, k_norm, w_o, cos, sin, with_ctx_out):
    hq, hkv, d = A_HEADS, A_KV_HEADS, A_HEAD_DIM
    grp = hq // hkv
    scale = d ** -0.5
    cuts = [hq * d, hq * d + hkv * d]

    def project(h):
        q, k, v = jnp.split(h @ w_qkv, cuts, axis=-1)
        return _split_heads(q, hq), rms_norm(_split_heads(k, hkv), k_norm), _split_heads(v, hkv)

    q_l, k_l, v_l = project(h_lat)
    q_c, k_c, v_c = project(h_ctx)
    q_l = apply_rope(rms_norm(q_l, q_norm), cos, sin)
    k_l = apply_rope(k_l, cos, sin)
    k_all = jnp.concatenate([k_c, k_l], axis=2)
    v_all = jnp.concatenate([v_c, v_l], axis=2)

    def attend(q, k, v):
        s = jnp.einsum("bkgqd,bkld->bkgql", q, k).astype(jnp.float32) * scale
        p = _softmax32(s).astype(v.dtype)
        return jnp.einsum("bkgql,bkld->bkgqd", p, v)

    bsz, _, n_lat, _ = q_l.shape
    o_l = sweep_query_blocks(lambda qb: attend(qb, k_all, v_all),
                             (q_l.reshape(bsz, hkv, grp, n_lat, d),))
    y_lat = _merge_heads(o_l.reshape(bsz, hq, n_lat, d)) @ w_o
    if not with_ctx_out:
        return y_lat, None
    n_ctx = q_c.shape[2]
    q_c = rms_norm(q_c, q_norm).reshape(bsz, hkv, grp, n_ctx, d)
    o_c = attend(q_c, k_c, v_c)
    y_ctx = _merge_heads(o_c.reshape(bsz, hq, n_ctx, d)) @ w_o
    return y_lat, y_ctx


def diff_mixer(h_lat, h_ctx, w_qkv, lam_q1, lam_k1, lam_q2, lam_k2, subln, w_o,
               cos, sin, layer_idx, with_ctx_out):
    nh, d = B_HEADS, B_HEAD_DIM
    scale = d ** -0.5
    lam_init = 0.8 - 0.6 * math.exp(-0.3 * layer_idx)
    lam = (jnp.exp(jnp.sum(lam_q1.astype(jnp.float32) * lam_k1.astype(jnp.float32)))
           - jnp.exp(jnp.sum(lam_q2.astype(jnp.float32) * lam_k2.astype(jnp.float32)))
           + lam_init)
    cuts = [2 * nh * d, 4 * nh * d]

    def project(h):
        bsz, n, _ = h.shape
        q, k, v = jnp.split(h @ w_qkv, cuts, axis=-1)
        q = q.reshape(bsz, n, nh, 2, d).transpose(3, 0, 2, 1, 4)
        k = k.reshape(bsz, n, nh, 2, d).transpose(3, 0, 2, 1, 4)
        return q, k, _split_heads(v, nh)

    q_l, k_l, v_l = project(h_lat)
    q_c, k_c, v_c = project(h_ctx)
    q_l = apply_rope(q_l, cos, sin)
    k_l = apply_rope(k_l, cos, sin)
    k_all = jnp.concatenate([k_c, k_l], axis=3)
    v_all = jnp.concatenate([v_c, v_l], axis=2)

    def attend(q1, q2, k, v):
        s1 = jnp.einsum("bhqd,bhld->bhql", q1, k[0]).astype(jnp.float32) * scale
        s2 = jnp.einsum("bhqd,bhld->bhql", q2, k[1]).astype(jnp.float32) * scale
        p = _softmax32(s1) - lam * _softmax32(s2)
        return jnp.einsum("bhql,bhle->bhqe", p.astype(v.dtype), v)

    def finish(o):
        return _merge_heads(rms_norm(o, subln) * (1.0 - lam_init)) @ w_o

    o_l = sweep_query_blocks(lambda a, b: attend(a, b, k_all, v_all), (q_l[0], q_l[1]))
    y_lat = finish(o_l)
    if not with_ctx_out:
        return y_lat, None
    y_ctx = finish(attend(q_c[0], q_c[1], k_c, v_c))
    return y_lat, y_ctx


def gla_chunk_scan(q, k, v, g, s0):
    bsz, nh, n_tok, _ = q.shape
    nc = n_tok // C_CHUNK

    def chunks(a):
        return jnp.moveaxis(a.reshape(bsz, nh, nc, C_CHUNK, a.shape[-1]), 2, 0)

    tri = jnp.tril(jnp.ones((C_CHUNK, C_CHUNK), dtype=bool))

    def step(s, blk):
        qc, kc, vc, gc = blk
        b = jnp.cumsum(gc, axis=-2)
        b_last = b[:, :, -1:, :]
        o_inter = jnp.einsum("bhld,bhdv->bhlv", qc * jnp.exp(b), s)
        rel = b[:, :, :, None, :] - b[:, :, None, :, :]
        decay = jnp.exp(jnp.where(tri[:, :, None], rel, -jnp.inf))
        a = jnp.einsum("bhid,bhjd,bhijd->bhij", qc, kc, decay)
        o = o_inter + jnp.einsum("bhij,bhjv->bhiv", a, vc)
        s = (s * jnp.exp(b_last)[:, :, 0, :, None]
             + jnp.einsum("bhld,bhlv->bhdv", kc * jnp.exp(b_last - b), vc))
        return s, o

    s, o = lax.scan(step, s0, (chunks(q), chunks(k), chunks(v), chunks(g)))
    o = jnp.moveaxis(o, 0, 2).reshape(bsz, nh, n_tok, v.shape[-1])
    return o, s


def gla_final_state(k, v, g):
    b = jnp.cumsum(g, axis=-2)
    return jnp.einsum("bhtd,bhtv->bhdv", k * jnp.exp(b[:, :, -1:, :] - b), v)


def gla_mixer(h_lat, h_ctx, w_in, w_gate_fwd, b_gate_fwd, w_gate_bwd, b_gate_bwd,
              norm_g, w_o, with_ctx_out):
    nh = C_HEADS
    dk, dv = C_KEY_DIM // nh, C_VAL_DIM // nh
    cuts = np.cumsum([C_KEY_DIM, C_KEY_DIM, C_VAL_DIM, C_VAL_DIM, C_GATE_RANK]).tolist()

    def heads32(a):
        return _split_heads(a, nh).astype(jnp.float32)

    def log_gate(z, w, b):
        return heads32(jax.nn.log_sigmoid((z @ w + b).astype(jnp.float32)) / C_GATE_NORMALIZER)

    def project(h):
        q, k, v, og, zf, zb = jnp.split(h @ w_in, cuts, axis=-1)
        return (heads32(q) * dk ** -0.5, heads32(k), heads32(v),
                log_gate(zf, w_gate_fwd, b_gate_fwd), log_gate(zb, w_gate_bwd, b_gate_bwd), og)

    def flip(a):
        return jnp.flip(a, axis=2)

    def finish(o, og):
        o = _merge_heads(rms_norm(o, norm_g)).astype(og.dtype)
        return (o * jax.nn.silu(og)) @ w_o

    q_l, k_l, v_l, gf_l, gb_l, og_l = project(h_lat)
    q_c, k_c, v_c, gf_c, gb_c, og_c = project(h_ctx)
    zero = jnp.zeros((q_l.shape[0], nh, dk, dv), jnp.float32)
    if with_ctx_out:
        o_cf, s_f = gla_chunk_scan(q_c, k_c, v_c, gf_c, zero)
        o_cb, s_b = gla_chunk_scan(flip(q_c), flip(k_c), flip(v_c), flip(gb_c), zero)
        y_ctx = finish(o_cf + flip(o_cb), og_c)
    else:
        s_f = gla_final_state(k_c, v_c, gf_c)
        s_b = gla_final_state(flip(k_c), flip(v_c), flip(gb_c))
        y_ctx = None
    o_lf, _ = gla_chunk_scan(q_l, k_l, v_l, gf_l, s_f)
    o_lb, _ = gla_chunk_scan(flip(q_l), flip(k_l), flip(v_l), flip(gb_l), s_b)
    y_lat = finish(o_lf + flip(o_lb), og_l)
    return y_lat, y_ctx


def neighbourhood_mixer(h_lat, h_ctx, w_qkv, rpb, w_o, with_ctx_out):
    nh, d = D_HEADS, D_HEAD_DIM
    scale = d ** -0.5
    cuts = [nh * d, 2 * nh * d]

    def project(h):
        q, k, v = jnp.split(h @ w_qkv, cuts, axis=-1)
        return _split_heads(q, nh), _split_heads(k, nh), _split_heads(v, nh)

    q_l, k_l, v_l = project(h_lat)
    q_c, k_c, v_c = project(h_ctx)
    bsz, _, n_lat, _ = q_l.shape
    rows = n_lat // GRID_W
    wr = min(WIN_R, rows)

    def grid(a):
        return a.reshape(bsz, nh, rows, GRID_W, d)

    k_g, v_g = grid(k_l), grid(v_l)
    cq = np.arange(GRID_W)
    c0 = np.clip(cq - WIN_C // 2, 0, GRID_W - WIN_C)
    col_in = (cq[None, :] >= c0[:, None]) & (cq[None, :] < c0[:, None] + WIN_C)
    dc_idx = np.clip(cq[None, :] - cq[:, None], -(WIN_C - 1), WIN_C - 1) + WIN_C - 1
    n_band = wr * GRID_W

    def row_block(r, q_r):
        r0 = jnp.clip(r - wr // 2, 0, rows - wr)
        k_band = lax.dynamic_slice_in_dim(k_g, r0, wr, axis=2)
        v_band = lax.dynamic_slice_in_dim(v_g, r0, wr, axis=2)
        dr_idx = r0 + jnp.arange(wr) - r + (WIN_R - 1)
        bias = jnp.take(rpb, dr_idx, axis=1)[:, :, dc_idx]
        bias = bias.transpose(0, 2, 1, 3).astype(jnp.float32)
        s_nb = jnp.einsum("bhqd,bhrkd->bhqrk", q_r, k_band).astype(jnp.float32) * scale + bias
        s_nb = jnp.where(col_in[:, None, :], s_nb, -jnp.inf).reshape(bsz, nh, GRID_W, n_band)
        s_c = jnp.einsum("bhqd,bhcd->bhqc", q_r, k_c).astype(jnp.float32) * scale
        p = _softmax32(jnp.concatenate([s_nb, s_c], axis=-1)).astype(v_l.dtype)
        p_nb = p[..., :n_band].reshape(bsz, nh, GRID_W, wr, GRID_W)
        return (jnp.einsum("bhqrk,bhrkd->bhqd", p_nb, v_band)
                + jnp.einsum("bhqc,bhcd->bhqd", p[..., n_band:], v_c))

    q_rows = jnp.moveaxis(grid(q_l), 2, 0)
    o = lax.map(lambda a: row_block(*a), (jnp.arange(rows), q_rows))
    o = jnp.moveaxis(o, 0, 2).reshape(bsz, nh, n_lat, d)
    y_lat = _merge_heads(o) @ w_o
    if not with_ctx_out:
        return y_lat, None
    s = jnp.einsum("bhqd,bhcd->bhqc", q_c, k_c).astype(jnp.float32) * scale
    o_c = jnp.einsum("bhqc,bhcd->bhqd", _softmax32(s).astype(v_c.dtype), v_c)
    return y_lat, _merge_heads(o_c) @ w_o


def dwconv_centred(u, w, b):
    pad = CONV_W // 2
    n = u.shape[1]
    up = jnp.pad(u, ((0, 0), (pad, pad), (0, 0)))
    out = b
    for j in range(CONV_W):
        out = out + up[:, j:j + n] * w[j]
    return out


def conv_ffn(h, w_up, conv_w, conv_b, w_down):
    u = dwconv_centred(h @ w_up, conv_w, conv_b)
    gate, val = jnp.split(u, 2, axis=-1)
    return (jax.nn.silu(gate) * val) @ w_down


def _n_of(kind):
    return len(range(kind, DEPTH, N_MIXERS))


def setup_inputs(seed: int = 0) -> dict:
    key = jax.random.key(seed)
    ks = iter(jax.random.split(key, 48))

    def nrm(shape, scale):
        return jax.random.normal(next(ks), shape, jnp.float32) * scale

    D = D_MODEL
    na, nb, nc, nd = _n_of(0), _n_of(1), _n_of(2), _n_of(3)
    beta = DEEPNORM_BETA
    return {
        "x": nrm((BATCH, SEQ, D), 1.0),
        "c": nrm((BATCH, D), 1.0),
        "ctx": nrm((BATCH, CTX_LEN, D), 1.0),
        "c_ctx": nrm((D,), 1.0),
        "mod_w": nrm((DEPTH, D, N_MOD * D), 0.5 * D ** -0.5),
        "mod_b": nrm((DEPTH, N_MOD * D), 0.02),
        "ln1_g": 1.0 + nrm((DEPTH, D), 0.02),
        "ln1_b": nrm((DEPTH, D), 0.02),
        "ffn_w_up": nrm((DEPTH, D, 2 * FF_DIM), D ** -0.5),
        "ffn_conv_w": nrm((DEPTH, CONV_W, 2 * FF_DIM), CONV_W ** -0.5),
        "ffn_conv_b": nrm((DEPTH, 2 * FF_DIM), 0.02),
        "ffn_w_down": nrm((DEPTH, FF_DIM, D), beta * FF_DIM ** -0.5),
        "ln2_g": 1.0 + nrm((DEPTH, D), 0.02),
        "ln2_b": nrm((DEPTH, D), 0.02),
        "a_w_qkv": nrm((na, D, (A_HEADS + 2 * A_KV_HEADS) * A_HEAD_DIM), D ** -0.5),
        "a_q_norm": 1.0 + nrm((na, A_HEAD_DIM), 0.02),
        "a_k_norm": 1.0 + nrm((na, A_HEAD_DIM), 0.02),
        "a_w_o": nrm((na, A_HEADS * A_HEAD_DIM, D), beta * (A_HEADS * A_HEAD_DIM) ** -0.5),
        "b_w_qkv": nrm((nb, D, 6 * B_HEADS * B_HEAD_DIM), D ** -0.5),
        "b_lambda_q1": nrm((nb, B_HEAD_DIM), 0.1),
        "b_lambda_k1": nrm((nb, B_HEAD_DIM), 0.1),
        "b_lambda_q2": nrm((nb, B_HEAD_DIM), 0.1),
        "b_lambda_k2": nrm((nb, B_HEAD_DIM), 0.1),
        "b_subln": 1.0 + nrm((nb, 2 * B_HEAD_DIM), 0.02),
        "b_w_o": nrm((nb, 2 * B_HEADS * B_HEAD_DIM, D), beta * (2 * B_HEADS * B_HEAD_DIM) ** -0.5),
        "c_w_in": nrm((nc, D, 2 * C_KEY_DIM + 2 * C_VAL_DIM + 2 * C_GATE_RANK), D ** -0.5),
        "c_w_gate_fwd": nrm((nc, C_GATE_RANK, C_KEY_DIM), C_GATE_RANK ** -0.5),
        "c_b_gate_fwd": nrm((nc, C_KEY_DIM), 0.02),
        "c_w_gate_bwd": nrm((nc, C_GATE_RANK, C_KEY_DIM), C_GATE_RANK ** -0.5),
        "c_b_gate_bwd": nrm((nc, C_KEY_DIM), 0.02),
        "c_norm": 1.0 + nrm((nc, C_VAL_DIM // C_HEADS), 0.02),
        "c_w_o": nrm((nc, C_VAL_DIM, D), beta * C_VAL_DIM ** -0.5),
        "d_w_qkv": nrm((nd, D, 3 * D_HEADS * D_HEAD_DIM), D ** -0.5),
        "d_rpb": nrm((nd, D_HEADS, 2 * WIN_R - 1, 2 * WIN_C - 1), 0.02),
        "d_w_o": nrm((nd, D_HEADS * D_HEAD_DIM, D), beta * (D_HEADS * D_HEAD_DIM) ** -0.5),
    }


def reference(x, c, ctx, c_ctx, mod_w, mod_b, ln1_g, ln1_b, ffn_w_up, ffn_conv_w,
              ffn_conv_b, ffn_w_down, ln2_g, ln2_b,
              a_w_qkv, a_q_norm, a_k_norm, a_w_o,
              b_w_qkv, b_lambda_q1, b_lambda_k1, b_lambda_q2, b_lambda_k2, b_subln, b_w_o,
              c_w_in, c_w_gate_fwd, c_b_gate_fwd, c_w_gate_bwd, c_b_gate_bwd, c_norm, c_w_o,
              d_w_qkv, d_rpb, d_w_o):
    bsz, n_lat, dm = x.shape
    cos, sin = axial_rope(n_lat, A_HEAD_DIM)
    cond_lat = jax.nn.silu(c)
    cond_ctx = jax.nn.silu(c_ctx)
    alpha = DEEPNORM_ALPHA
    x_lat, x_ctx = x, ctx
    for i in range(DEPTH):
        kind, j = i % N_MIXERS, i // N_MIXERS
        last = i == DEPTH - 1
        m_lat = (cond_lat @ mod_w[i] + mod_b[i]).reshape(bsz, N_MOD, 1, dm)
        n_ctx_mod = 3 if last else N_MOD
        m_ctx = (cond_ctx @ mod_w[i][:, :n_ctx_mod * dm] + mod_b[i][:n_ctx_mod * dm]).reshape(n_ctx_mod, dm)
        h_lat = x_lat * (1.0 + m_lat[:, 1]) + m_lat[:, 0]
        h_ctx = x_ctx * (1.0 + m_ctx[1]) + m_ctx[0]
        if kind == 0:
            y_lat, y_ctx = gqa_mixer(h_lat, h_ctx, a_w_qkv[j], a_q_norm[j], a_k_norm[j], a_w_o[j],
                                     cos, sin, not last)
        elif kind == 1:
            y_lat, y_ctx = diff_mixer(h_lat, h_ctx, b_w_qkv[j], b_lambda_q1[j], b_lambda_k1[j],
                                      b_lambda_q2[j], b_lambda_k2[j], b_subln[j], b_w_o[j],
                                      cos, sin, i, not last)
        elif kind == 2:
            y_lat, y_ctx = gla_mixer(h_lat, h_ctx, c_w_in[j], c_w_gate_fwd[j], c_b_gate_fwd[j],
                                     c_w_gate_bwd[j], c_b_gate_bwd[j], c_norm[j], c_w_o[j], not last)
        else:
            y_lat, y_ctx = neighbourhood_mixer(h_lat, h_ctx, d_w_qkv[j], d_rpb[j], d_w_o[j], not last)
        x_lat = layer_norm(alpha * x_lat + m_lat[:, 2] * y_lat, ln1_g[i], ln1_b[i])
        h_lat = x_lat * (1.0 + m_lat[:, 4]) + m_lat[:, 3]
        f_lat = conv_ffn(h_lat, ffn_w_up[i], ffn_conv_w[i], ffn_conv_b[i], ffn_w_down[i])
        x_lat = layer_norm(alpha * x_lat + m_lat[:, 5] * f_lat, ln2_g[i], ln2_b[i])
        if not last:
            x_ctx = layer_norm(alpha * x_ctx + m_ctx[2] * y_ctx, ln1_g[i], ln1_b[i])
            h_ctx = x_ctx * (1.0 + m_ctx[4]) + m_ctx[3]
            f_ctx = conv_ffn(h_ctx, ffn_w_up[i], ffn_conv_w[i], ffn_conv_b[i], ffn_w_down[i])
            x_ctx = layer_norm(alpha * x_ctx + m_ctx[5] * f_ctx, ln2_g[i], ln2_b[i])
    return x_lat
```

```python
import functools
import math

import jax
import jax.numpy as jnp
import numpy as np
from jax import lax
from jax.experimental import pallas as pl
from jax.experimental.pallas import tpu as pltpu

F32 = jnp.float32
BF = jnp.bfloat16

DEPTH = 4
N_MIXERS = 4
N_MOD = 6
GRID_W = 64
ROPE_THETA = 10000.0
NORM_EPS = 1e-6
LN_EPS = 1e-5
HEAD_DIM = 64
A_HEADS, A_KV_HEADS = 16, 4
B_HEADS = 8
C_HEADS = 4
C_GATE_RANK = 16
C_GATE_NORMALIZER = 16.0
C_CHUNK = 64
C_SUB = 16
D_HEADS = 16
WIN_R, WIN_C = 8, 16
FF_DIM = 2816
DEEPNORM_ALPHA = (2 * DEPTH) ** 0.25
MASK_VALUE = -1e30

LANES = 128
MXU_WIDTH = 256
VMEM_LIMIT_BYTES = 56 * 1024 * 1024


def _cparams(n_axes):
    return pltpu.CompilerParams(
        dimension_semantics=("arbitrary",) * n_axes,
        vmem_limit_bytes=VMEM_LIMIT_BYTES)


def _dot(a, b):
    return jnp.dot(a, b, preferred_element_type=F32)


def _dot_nt(a, b):
    return lax.dot_general(a, b, (((1,), (1,)), ((), ())), preferred_element_type=F32)


def _dot_tn(a, b):
    return lax.dot_general(a, b, (((0,), (0,)), ((), ())), preferred_element_type=F32)


def _split_dot(x, m_bf16):
    hi = x.astype(BF)
    lo = (x - hi.astype(F32)).astype(BF)
    return _dot(hi, m_bf16) + _dot(lo, m_bf16)


def _split_dot_left(m_bf16, x):
    hi = x.astype(BF)
    lo = (x - hi.astype(F32)).astype(BF)
    return _dot(m_bf16, hi) + _dot(m_bf16, lo)


def _sigmoid(x):
    return 1.0 / (1.0 + jnp.exp(-x))


def _modulate(x_ref, mod_ref, shift_row, scale_row):
    x = x_ref[...]
    return x * (1.0 + mod_ref[scale_row:scale_row + 1, :]) + mod_ref[shift_row:shift_row + 1, :]


def _layer_norm(z, g, b):
    mu = jnp.mean(z, axis=-1, keepdims=True)
    zc = z - mu
    var = jnp.mean(zc * zc, axis=-1, keepdims=True)
    return zc * lax.rsqrt(var + LN_EPS) * g + b


def _rope(y, cos_t, sin_t, lo_half):
    swap = jnp.where(lo_half, pltpu.roll(y, LANES - 32, axis=1), pltpu.roll(y, 32, axis=1))
    return y * cos_t + swap * sin_t


def _idiv(x, n):
    return lax.shift_right_logical(x, int(n).bit_length() - 1)


def _imod(x, n):
    return x & (n - 1)


def _lane_iota(shape=(1, LANES)):
    return lax.broadcasted_iota(jnp.int32, shape, len(shape) - 1)


def _mod_kernel(cond_ref, w_ref, b_ref, o_ref):
    cnd = cond_ref[...]
    act = cnd * _sigmoid(cnd)
    o_ref[...] = _dot(act.astype(BF), w_ref[...].astype(BF)) + b_ref[...]


def _modulation(cond, mod_w, mod_b):
    depth, dm, n = mod_w.shape
    rows = cond.shape[0]
    tn = 1536
    return pl.pallas_call(
        _mod_kernel,
        out_shape=jax.ShapeDtypeStruct((depth, rows, n), F32),
        grid=(depth, n // tn),
        in_specs=[
            pl.BlockSpec((rows, dm), lambda i, j: (0, 0)),
            pl.BlockSpec((None, dm, tn), lambda i, j: (i, 0, j)),
            pl.BlockSpec((None, 1, tn), lambda i, j: (i, 0, j)),
        ],
        out_specs=pl.BlockSpec((None, rows, tn), lambda i, j: (i, 0, j)),
        compiler_params=_cparams(2),
        name="adaln_modulation",
    )(cond, mod_w, mod_b.reshape(depth, 1, n))


def _row_tile(n_tok):
    return 512 if n_tok % 512 == 0 else n_tok


def _proj_call(kernel, x, mod, w, extras, extra_specs, out_shapes, out_specs, name):
    bsz, n_tok, dm = x.shape
    tm = _row_tile(n_tok)
    in_specs = [
        pl.BlockSpec((None, tm, dm), lambda b, i: (b, i, 0)),
        pl.BlockSpec((None, N_MOD, dm), lambda b, i: (b, 0, 0)),
        pl.BlockSpec(w.shape, lambda b, i: (0, 0)),
    ] + list(extra_specs(tm))
    return pl.pallas_call(
        kernel,
        out_shape=out_shapes,
        grid=(bsz, n_tok // tm),
        in_specs=in_specs,
        out_specs=out_specs(tm),
        compiler_params=_cparams(2),
        name=name,
    )(x, mod, w, *extras)


def _full_spec(arr):
    nd = arr.ndim
    return pl.BlockSpec(arr.shape, lambda b, i: (0,) * nd)


def _seg_matrix():
    idx = np.arange(LANES) // HEAD_DIM
    return jnp.asarray((idx[:, None] == idx[None, :]).astype(np.float32) / HEAD_DIM, dtype=BF)


def _rope_tables(n_tok):
    t = jnp.arange(n_tok)
    row = (t // GRID_W).astype(F32)
    col = (t % GRID_W).astype(F32)
    n_freq = HEAD_DIM // 4
    inv = 1.0 / (ROPE_THETA ** (jnp.arange(n_freq, dtype=F32) / n_freq))
    ang = jnp.concatenate([row[:, None] * inv, col[:, None] * inv], axis=-1)
    cos, sin = jnp.cos(ang), jnp.sin(ang)
    cos_t = jnp.concatenate([cos, cos, cos, cos], axis=-1)
    sin_t = jnp.concatenate([-sin, sin, -sin, sin], axis=-1)
    return cos_t, sin_t


def _proj_qkv_kernel(x_ref, mod_ref, w_ref, *rest, n_q, n_k, n_v, norm, rope, q_scale):
    rest = list(rest)
    seg_ref = gq_ref = gk_ref = cos_ref = sin_ref = None
    if norm:
        seg_ref, gq_ref, gk_ref = rest[:3]
        rest = rest[3:]
    if rope:
        cos_ref, sin_ref = rest[:2]
        rest = rest[2:]
    (o_ref,) = rest
    hb = _modulate(x_ref, mod_ref, 0, 1).astype(BF)
    lo_half = _imod(_lane_iota(), HEAD_DIM) < (HEAD_DIM // 2)
    n_blocks = n_q + n_k + n_v
    per_dot = MXU_WIDTH // LANES
    for c0 in range(0, n_blocks, per_dot):
        nb = min(per_dot, n_blocks - c0)
        y2 = _dot(hb, w_ref[:, c0 * LANES:(c0 + nb) * LANES])
        for j in range(nb):
            c = c0 + j
            y = y2[:, j * LANES:(j + 1) * LANES]
            if c < n_q + n_k:
                if norm:
                    ms = _split_dot(y * y, seg_ref[...])
                    gain = gq_ref[...] if c < n_q else gk_ref[...]
                    y = y * lax.rsqrt(ms + NORM_EPS) * gain
                if rope:
                    y = _rope(y, cos_ref[...], sin_ref[...], lo_half)
                if c < n_q and q_scale != 1.0:
                    y = y * q_scale
            o_ref[:, c * LANES:(c + 1) * LANES] = y.astype(BF)


def _proj_qkv(x, mod, w, *, n_q, n_k, n_v, q_scale, norm_gains=None, rope=False):
    bsz, n_tok, _ = x.shape
    n_out = w.shape[1]
    extras, spec_fns = [], []
    if norm_gains is not None:
        gq, gk = norm_gains
        extras += [_seg_matrix(), jnp.tile(gq, 2).reshape(1, LANES), jnp.tile(gk, 2).reshape(1, LANES)]
        spec_fns += [lambda tm, a=a: _full_spec(a) for a in extras]
    if rope:
        cos_t, sin_t = _rope_tables(n_tok)
        extras += [cos_t, sin_t]
        spec_fns += [lambda tm: pl.BlockSpec((tm, LANES), lambda b, i: (i, 0))] * 2
    kern = functools.partial(_proj_qkv_kernel, n_q=n_q, n_k=n_k, n_v=n_v,
                             norm=norm_gains is not None, rope=rope, q_scale=q_scale)
    return _proj_call(
        kern, x, mod, w, extras, lambda tm: [f(tm) for f in spec_fns],
        jax.ShapeDtypeStruct((bsz, n_tok, n_out), BF),
        lambda tm: pl.BlockSpec((None, tm, n_out), lambda b, i: (b, i, 0)),
        "mixer_qkv_projection")


def _log_sigmoid(x):
    return jnp.minimum(x, 0.0) - jnp.log(1.0 + jnp.exp(-jnp.abs(x)))


def _proj_gla_kernel(x_ref, mod_ref, w_ref, wz_ref, wg_ref, bg_ref, o_ref, g_ref, *, q_scale, n_q):
    hb = _modulate(x_ref, mod_ref, 0, 1).astype(BF)
    n_blocks = o_ref.shape[-1] // LANES
    per_dot = MXU_WIDTH // LANES
    for c0 in range(0, n_blocks, per_dot):
        y2 = _dot(hb, w_ref[:, c0 * LANES:(c0 + per_dot) * LANES])
        if c0 < n_q:
            y2 = y2 * q_scale
        o_ref[:, c0 * LANES:(c0 + per_dot) * LANES] = y2.astype(BF)
    z = _dot(hb, wz_ref[...]).astype(BF)
    n_gate = g_ref.shape[-1]
    for c0 in range(0, n_gate, MXU_WIDTH):
        pre = _dot(z, wg_ref[:, c0:c0 + MXU_WIDTH]) + bg_ref[:, c0:c0 + MXU_WIDTH]
        g_ref[:, c0:c0 + MXU_WIDTH] = _log_sigmoid(pre) * (1.0 / C_GATE_NORMALIZER)


def _proj_gla(x, mod, w_main, w_z, w_gate, b_gate, *, q_scale, n_q):
    bsz, n_tok, _ = x.shape
    n_out, n_gate = w_main.shape[1], w_gate.shape[1]
    extras = [w_z, w_gate, b_gate]
    kern = functools.partial(_proj_gla_kernel, q_scale=q_scale, n_q=n_q)
    return _proj_call(
        kern, x, mod, w_main, extras, lambda tm: [_full_spec(a) for a in extras],
        (jax.ShapeDtypeStruct((bsz, n_tok, n_out), BF), jax.ShapeDtypeStruct((bsz, n_tok, n_gate), F32)),
        lambda tm: (pl.BlockSpec((None, tm, n_out), lambda b, i: (b, i, 0)),
                    pl.BlockSpec((None, tm, n_gate), lambda b, i: (b, i, 0))),
        "gla_projection")


def _out_kernel(a_ref, w_ref, x_ref, mod_ref, g_ref, b_ref, o_ref, *, gate_row):
    y = _dot(a_ref[...], w_ref[...])
    z = DEEPNORM_ALPHA * x_ref[...] + mod_ref[gate_row:gate_row + 1, :] * y
    o_ref[...] = _layer_norm(z, g_ref[...], b_ref[...])


def _out_proj_ln(a, w_o, x, mod, ln_g, ln_b, gate_row):
    bsz, n_tok, dm = x.shape
    k_in = a.shape[-1]
    tm = _row_tile(n_tok)
    return pl.pallas_call(
        functools.partial(_out_kernel, gate_row=gate_row),
        out_shape=jax.ShapeDtypeStruct((bsz, n_tok, dm), F32),
        grid=(bsz, n_tok // tm),
        in_specs=[
            pl.BlockSpec((None, tm, k_in), lambda b, i: (b, i, 0)),
            pl.BlockSpec((k_in, dm), lambda b, i: (0, 0)),
            pl.BlockSpec((None, tm, dm), lambda b, i: (b, i, 0)),
            pl.BlockSpec((None, N_MOD, dm), lambda b, i: (b, 0, 0)),
            pl.BlockSpec((1, dm), lambda b, i: (0, 0)),
            pl.BlockSpec((1, dm), lambda b, i: (0, 0)),
        ],
        out_specs=pl.BlockSpec((None, tm, dm), lambda b, i: (b, i, 0)),
        compiler_params=_cparams(2),
        name="out_projection_layernorm",
    )(a, w_o, x, mod, ln_g.reshape(1, dm), ln_b.reshape(1, dm))


FFN_CHUNK = 256
HALO = 8


def _ffn_kernel(x_ref, xp_ref, xn_ref, mod_ref, wu_ref, cw_ref, cb_ref, wd_ref, g_ref, b_ref,
                o_ref, acc_ref):
    tm = x_ref.shape[0]
    i = pl.program_id(1)
    n_i = pl.num_programs(1)
    scale = 1.0 + mod_ref[4:5, :]
    shift = mod_ref[3:4, :]
    x = x_ref[...]
    h_ext = jnp.concatenate(
        [x * scale + shift, xp_ref[...] * scale + shift, xn_ref[...] * scale + shift], axis=0)
    hb = h_ext.astype(BF)
    rows = lax.broadcasted_iota(jnp.int32, (tm, 1), 0)
    first_row = rows == 0
    last_row = rows == tm - 1
    has_prev = (i > 0).astype(F32)
    has_next = (i < n_i - 1).astype(F32)

    def conv(u_ext, c0):
        u = u_ext[:tm]
        prev = u_ext[tm + HALO - 1:tm + HALO] * has_prev
        nxt = u_ext[tm + HALO:tm + HALO + 1] * has_next
        u_dn = jnp.where(first_row, prev, pltpu.roll(u, 1, axis=0))
        u_up = jnp.where(last_row, nxt, pltpu.roll(u, tm - 1, axis=0))
        w0 = cw_ref[0:1, c0:c0 + FFN_CHUNK]
        w1 = cw_ref[1:2, c0:c0 + FFN_CHUNK]
        w2 = cw_ref[2:3, c0:c0 + FFN_CHUNK]
        return cb_ref[:, c0:c0 + FFN_CHUNK] + u_dn * w0 + u * w1 + u_up * w2

    for c in range(FF_DIM // FFN_CHUNK):
        cg = c * FFN_CHUNK
        cv = FF_DIM + cg
        gate = conv(_dot(hb, wu_ref[:, cg:cg + FFN_CHUNK]), cg)
        val = conv(_dot(hb, wu_ref[:, cv:cv + FFN_CHUNK]), cv)
        act = (gate * _sigmoid(gate) * val).astype(BF)
        part = _dot(act, wd_ref[cg:cg + FFN_CHUNK, :])
        if c == 0:
            acc_ref[...] = part
        else:
            acc_ref[...] += part
    z = DEEPNORM_ALPHA * x + mod_ref[5:6, :] * acc_ref[...]
    o_ref[...] = _layer_norm(z, g_ref[...], b_ref[...])


def _conv_ffn_ln(x, mod, w_up, conv_w, conv_b, w_down, ln_g, ln_b):
    bsz, n_tok, dm = x.shape
    tm = _row_tile(n_tok)
    n_halo = n_tok // HALO
    per = tm // HALO
    const2 = lambda b, i: (0, 0)
    return pl.pallas_call(
        _ffn_kernel,
        out_shape=jax.ShapeDtypeStruct((bsz, n_tok, dm), F32),
        grid=(bsz, n_tok // tm),
        in_specs=[
            pl.BlockSpec((None, tm, dm), lambda b, i: (b, i, 0)),
            pl.BlockSpec((None, HALO, dm), lambda b, i: (b, jnp.maximum(i * per - 1, 0), 0)),
            pl.BlockSpec((None, HALO, dm), lambda b, i: (b, jnp.minimum((i + 1) * per, n_halo - 1), 0)),
            pl.BlockSpec((None, N_MOD, dm), lambda b, i: (b, 0, 0)),
            pl.BlockSpec(w_up.shape, const2),
            pl.BlockSpec(conv_w.shape, const2),
            pl.BlockSpec((1, 2 * FF_DIM), const2),
            pl.BlockSpec(w_down.shape, const2),
            pl.BlockSpec((1, dm), const2),
            pl.BlockSpec((1, dm), const2),
        ],
        out_specs=pl.BlockSpec((None, tm, dm), lambda b, i: (b, i, 0)),
        scratch_shapes=[pltpu.VMEM((tm, dm), F32)],
        compiler_params=_cparams(2),
        name="conv_ffn_layernorm",
    )(x, x, x, mod, w_up, conv_w, conv_b.reshape(1, -1), w_down, ln_g.reshape(1, dm), ln_b.reshape(1, dm))


def _softmax_pv(q, keys, values, biases):
    scores = []
    for k, bias in zip(keys, biases):
        s = _dot_nt(q, k)
        scores.append(s if bias is None else s + bias)
    m = functools.reduce(jnp.maximum, [jnp.max(s, axis=-1, keepdims=True) for s in scores])
    probs = [jnp.exp(s - m) for s in scores]
    denom = functools.reduce(lambda a, b: a + b, [jnp.sum(p, axis=-1, keepdims=True) for p in probs])
    out = functools.reduce(lambda a, b: a + b, [_dot(p.astype(BF), v) for p, v in zip(probs, values)])
    return out / denom


def _attn_gqa_kernel(q_ref, *rest, n_seg):
    k_refs, v_refs, o_ref = rest[:n_seg], rest[n_seg:2 * n_seg], rest[2 * n_seg]
    grp = A_HEADS // A_KV_HEADS
    kv_pos = (pl.program_id(1) * 2 // grp) % 2
    half = _idiv(_lane_iota(), HEAD_DIM)
    q = q_ref[...].astype(F32)
    q_sw = pltpu.roll(q, HEAD_DIM, axis=1)
    keys = [r[...] for r in k_refs]
    values = [r[...] for r in v_refs]
    outs = []
    for e in range(2):
        aligned = kv_pos == e
        q_e = jnp.where(aligned, q, q_sw)
        q_e = jnp.where(half == kv_pos, q_e, 0.0).astype(BF)
        o_e = _softmax_pv(q_e, keys, values, [None] * n_seg)
        outs.append(jnp.where(aligned, o_e, pltpu.roll(o_e, HEAD_DIM, axis=1)))
    o_ref[...] = jnp.where(half == 0, outs[0], outs[1]).astype(BF)


def _attn_gqa(q_src, kv_srcs):
    bsz, n_q_tok, _ = q_src.shape
    tq = _row_tile(n_q_tok)
    n_qb = A_HEADS * HEAD_DIM // LANES
    n_kb = A_KV_HEADS * HEAD_DIM // LANES
    per_kb = n_qb // n_kb
    n_seg = len(kv_srcs)
    k_specs = [pl.BlockSpec((None, s.shape[1], LANES), lambda b, p, i: (b, 0, n_qb + p // per_kb))
               for s in kv_srcs]
    v_specs = [pl.BlockSpec((None, s.shape[1], LANES), lambda b, p, i: (b, 0, n_qb + n_kb + p // per_kb))
               for s in kv_srcs]
    return pl.pallas_call(
        functools.partial(_attn_gqa_kernel, n_seg=n_seg),
        out_shape=jax.ShapeDtypeStruct((bsz, n_q_tok, n_qb * LANES), BF),
        grid=(bsz, n_qb, n_q_tok // tq),
        in_specs=[pl.BlockSpec((None, tq, LANES), lambda b, p, i: (b, i, p))] + k_specs + v_specs,
        out_specs=pl.BlockSpec((None, tq, LANES), lambda b, p, i: (b, i, p)),
        compiler_params=_cparams(3),
        name="gqa_attention",
    )(q_src, *kv_srcs, *kv_srcs)


def _attn_diff_kernel(q_ref, lam_ref, subln_ref, *rest, n_seg, lam_init):
    k_refs, v_refs, o_ref = rest[:n_seg], rest[n_seg:2 * n_seg], rest[2 * n_seg]
    lam_v = lam_ref[...]
    lam = (jnp.exp(jnp.sum(lam_v[0:1] * lam_v[1:2], axis=-1, keepdims=True))
           - jnp.exp(jnp.sum(lam_v[2:3] * lam_v[3:4], axis=-1, keepdims=True)) + lam_init)
    half = _idiv(_lane_iota(), HEAD_DIM)
    q = q_ref[...]
    keys = [r[...] for r in k_refs]
    values = [r[...] for r in v_refs]
    zero = jnp.zeros_like(q)
    o1 = _softmax_pv(jnp.where(half == 0, q, zero), keys, values, [None] * n_seg)
    o2 = _softmax_pv(jnp.where(half == 1, q, zero), keys, values, [None] * n_seg)
    o = o1 - lam * o2
    ms = jnp.mean(o * o, axis=-1, keepdims=True)
    o = o * lax.rsqrt(ms + NORM_EPS) * subln_ref[...] * (1.0 - lam_init)
    o_ref[...] = o.astype(BF)


def _attn_diff(q_src, kv_srcs, lam_vecs, subln, lam_init):
    bsz, n_q_tok, _ = q_src.shape
    tq = _row_tile(n_q_tok)
    nh = B_HEADS
    n_seg = len(kv_srcs)
    k_specs = [pl.BlockSpec((None, s.shape[1], LANES), lambda b, h, i: (b, 0, nh + h)) for s in kv_srcs]
    v_specs = [pl.BlockSpec((None, s.shape[1], LANES), lambda b, h, i: (b, 0, 2 * nh + h)) for s in kv_srcs]
    return pl.pallas_call(
        functools.partial(_attn_diff_kernel, n_seg=n_seg, lam_init=lam_init),
        out_shape=jax.ShapeDtypeStruct((bsz, n_q_tok, nh * LANES), BF),
        grid=(bsz, nh, n_q_tok // tq),
        in_specs=[pl.BlockSpec((None, tq, LANES), lambda b, h, i: (b, i, h)),
                  pl.BlockSpec(lam_vecs.shape, lambda b, h, i: (0, 0)),
                  pl.BlockSpec((1, LANES), lambda b, h, i: (0, 0))] + k_specs + v_specs,
        out_specs=pl.BlockSpec((None, tq, LANES), lambda b, h, i: (b, i, h)),
        compiler_params=_cparams(3),
        name="differential_attention",
    )(q_src, lam_vecs, subln.reshape(1, LANES), *kv_srcs, *kv_srcs)


NB_QROWS = 4
NB_BAND = 12


def _nb_band_start(g, rows):
    return int(np.clip(NB_QROWS * g - NB_QROWS, 0, rows - NB_BAND))


def _nb_config(g, rows):
    n_g = rows // NB_QROWS
    return 0 if g == 0 else (2 if g == n_g - 1 else 1)


def _nb_bias_table(rpb, rows):
    nh = rpb.shape[0]
    wr = min(WIN_R, rows)
    cq = np.arange(GRID_W)
    c0 = np.clip(cq - WIN_C // 2, 0, GRID_W - WIN_C)
    col_in = (cq[None, :] >= c0[:, None]) & (cq[None, :] < c0[:, None] + WIN_C)
    dc_idx = np.clip(cq[None, :] - cq[:, None], -(WIN_C - 1), WIN_C - 1) + WIN_C - 1
    n_dr = 2 * WIN_R - 1
    tiles = rpb[:, :, dc_idx]
    tiles = jnp.where(col_in[None, None], tiles, MASK_VALUE)
    tiles = jnp.concatenate([tiles, jnp.full((nh, 1, GRID_W, GRID_W), MASK_VALUE, F32)], axis=1)
    n_g = rows // NB_QROWS
    sel = np.full((3, NB_QROWS, NB_BAND), n_dr, dtype=np.int32)
    for cfg, g in ((0, 0), (1, 1), (2, n_g - 1)):
        start = _nb_band_start(g, rows)
        for a in range(NB_QROWS):
            r = NB_QROWS * g + a
            r0 = int(np.clip(r - wr // 2, 0, rows - wr))
            for jb in range(NB_BAND):
                kr = start + jb
                if r0 <= kr < r0 + wr:
                    sel[cfg, a, jb] = kr - r + (WIN_R - 1)
    big = tiles[:, sel]
    big = big.transpose(0, 1, 2, 4, 3, 5)
    return big.reshape(nh, 3, NB_QROWS * GRID_W, NB_BAND * GRID_W)


def _attn_nb_kernel(q_ref, kl_ref, vl_ref, kc_ref, vc_ref, bias_ref, o_ref, *, rows):
    half = _idiv(_lane_iota(), HEAD_DIM)
    kc = kc_ref[...]
    vc = vc_ref[...]
    tq = NB_QROWS * GRID_W
    tk = NB_BAND * GRID_W
    for g in range(rows // NB_QROWS):
        ks = _nb_band_start(g, rows) * GRID_W
        cfg = _nb_config(g, rows)
        q = q_ref[g * tq:(g + 1) * tq, :]
        kb = kl_ref[ks:ks + tk, :]
        vb = vl_ref[ks:ks + tk, :]
        zero = jnp.zeros_like(q)
        outs = []
        for e in range(2):
            q_e = jnp.where(half == e, q, zero)
            outs.append(_softmax_pv(q_e, [kb, kc], [vb, vc], [bias_ref[e, cfg], None]))
        o_ref[g * tq:(g + 1) * tq, :] = jnp.where(half == 0, outs[0], outs[1]).astype(BF)


def _attn_nb(qkv_lat, qkv_ctx, bias):
    bsz, n_tok, _ = qkv_lat.shape
    n_ctx = qkv_ctx.shape[1]
    rows = n_tok // GRID_W
    npair = D_HEADS * HEAD_DIM // LANES
    return pl.pallas_call(
        functools.partial(_attn_nb_kernel, rows=rows),
        out_shape=jax.ShapeDtypeStruct((bsz, n_tok, npair * LANES), BF),
        grid=(npair, bsz),
        in_specs=[
            pl.BlockSpec((None, n_tok, LANES), lambda p, b: (b, 0, p)),
            pl.BlockSpec((None, n_tok, LANES), lambda p, b: (b, 0, npair + p)),
            pl.BlockSpec((None, n_tok, LANES), lambda p, b: (b, 0, 2 * npair + p)),
            pl.BlockSpec((None, n_ctx, LANES), lambda p, b: (b, 0, npair + p)),
            pl.BlockSpec((None, n_ctx, LANES), lambda p, b: (b, 0, 2 * npair + p)),
            pl.BlockSpec((2,) + bias.shape[1:], lambda p, b: (p, 0, 0, 0)),
        ],
        out_specs=pl.BlockSpec((None, n_tok, LANES), lambda p, b: (b, 0, p)),
        compiler_params=_cparams(2),
        name="neighbourhood_attention",
    )(qkv_lat, qkv_lat, qkv_lat, qkv_ctx, qkv_ctx, bias)


def _attn_ctx_kernel(q_ref, k_ref, v_ref, o_ref):
    half = _idiv(_lane_iota(), HEAD_DIM)
    q = q_ref[...]
    zero = jnp.zeros_like(q)
    outs = [_softmax_pv(jnp.where(half == e, q, zero), [k_ref[...]], [v_ref[...]], [None]) for e in range(2)]
    o_ref[...] = jnp.where(half == 0, outs[0], outs[1]).astype(BF)


def _attn_ctx(qkv_ctx, npair):
    bsz, n_ctx, _ = qkv_ctx.shape
    return pl.pallas_call(
        _attn_ctx_kernel,
        out_shape=jax.ShapeDtypeStruct((bsz, n_ctx, npair * LANES), BF),
        grid=(bsz, npair),
        in_specs=[
            pl.BlockSpec((None, n_ctx, LANES), lambda b, p: (b, 0, p)),
            pl.BlockSpec((None, n_ctx, LANES), lambda b, p: (b, 0, npair + p)),
            pl.BlockSpec((None, n_ctx, LANES), lambda b, p: (b, 0, 2 * npair + p)),
        ],
        out_specs=pl.BlockSpec((None, n_ctx, LANES), lambda b, p: (b, 0, p)),
        compiler_params=_cparams(2),
        name="context_attention",
    )(qkv_ctx, qkv_ctx, qkv_ctx)


def _gla_chunk(q, k, v, g, st, rev, tri, row, col):
    nsub = C_CHUNK // C_SUB
    b = _split_dot_left(tri, g)
    b_tot = b[0:1] if rev else b[C_CHUNK - 1:C_CHUNK]
    o = _dot_nt((q * jnp.exp(b)).astype(BF), st.astype(BF))
    sub = _idiv(row, C_SUB)
    ref_rows = jnp.zeros_like(b)
    for a in range(nsub):
        if rev:
            if a == nsub - 1:
                continue
            bnd = b[(a + 1) * C_SUB:(a + 1) * C_SUB + 1]
        else:
            if a == 0:
                continue
            bnd = b[a * C_SUB - 1:a * C_SUB]
        ref_rows = jnp.where(sub == a, bnd, ref_rows)
    q_t = (q * jnp.exp(jnp.minimum(b - ref_rows, 0.0))).astype(BF)
    a_mat = jnp.zeros((C_CHUNK, C_CHUNK), F32)
    col_sub = _idiv(col, C_SUB)
    for a in range(nsub):
        if rev:
            if a == nsub - 1:
                continue
            bnd = b[(a + 1) * C_SUB:(a + 1) * C_SUB + 1]
            use = (sub == a) & (col_sub > a)
        else:
            if a == 0:
                continue
            bnd = b[a * C_SUB - 1:a * C_SUB]
            use = (sub == a) & (col_sub < a)
        k_t = (k * jnp.exp(jnp.minimum(bnd - b, 0.0))).astype(BF)
        a_mat = jnp.where(use, _dot_nt(q_t, k_t), a_mat)
    rin = _imod(row, C_SUB)
    for d in range(C_SUB):
        shift = (C_CHUNK - d) % C_CHUNK if rev else d
        k_d = k if d == 0 else pltpu.roll(k, shift, axis=0)
        b_d = b if d == 0 else pltpu.roll(b, shift, axis=0)
        term = q * k_d * jnp.exp(jnp.minimum(b - b_d, 0.0))
        diag = jnp.sum(term, axis=-1, keepdims=True)
        if rev:
            use = (col == row + d) & (rin + d < C_SUB)
        else:
            use = (col == row - d) & (rin >= d)
        a_mat = jnp.where(use, diag, a_mat)
    o = o + _dot(a_mat.astype(BF), v)
    k_end = (k * jnp.exp(b_tot - b)).astype(BF)
    st_new = st * jnp.exp(b_tot) + _dot_tn(v, k_end)
    return o, st_new


def _gla_kernel(ql_ref, kl_ref, vl_ref, ogl_ref, gfl_ref, gbl_ref,
                qc_ref, kc_ref, vc_ref, ogc_ref, gfc_ref, gbc_ref, ng_ref,
                ol_ref, oc_ref, fl_ref, fc_ref, *, with_ctx_out):
    dk = ql_ref.shape[-1]
    dv = vl_ref.shape[-1]
    n_lat = ql_ref.shape[0] // C_CHUNK
    n_ctx = qc_ref.shape[0] // C_CHUNK
    row = lax.broadcasted_iota(jnp.int32, (C_CHUNK, 1), 0)
    col = lax.broadcasted_iota(jnp.int32, (1, C_CHUNK), 1)
    tri_f = (col <= row).astype(BF)
    tri_b = (col >= row).astype(BF)

    def load(q_ref, k_ref, v_ref, g_ref, c):
        sl = pl.ds(pl.multiple_of(c * C_CHUNK, C_CHUNK), C_CHUNK)
        return q_ref[sl, :].astype(F32), k_ref[sl, :].astype(F32), v_ref[sl, :], g_ref[sl, :]

    def fwd_pass(refs, f_ref, n_chunks, st):
        def body(c, st):
            q, k, v, g = load(*refs, c)
            o, st = _gla_chunk(q, k, v, g, st, False, tri_f, row, col)
            f_ref[pl.ds(pl.multiple_of(c * C_CHUNK, C_CHUNK), C_CHUNK), :] = o
            return st
        return lax.fori_loop(0, n_chunks, body, st)

    def bwd_pass(refs, f_ref, og_ref, out_ref, n_chunks, st, emit):
        def body(j, st):
            c = n_chunks - 1 - j
            q, k, v, g = load(*refs, c)
            o, st = _gla_chunk(q, k, v, g, st, True, tri_b, row, col)
            if emit:
                sl = pl.ds(pl.multiple_of(c * C_CHUNK, C_CHUNK), C_CHUNK)
                tot = o + f_ref[sl, :]
                ms = jnp.mean(tot * tot, axis=-1, keepdims=True)
                y = tot * lax.rsqrt(ms + NORM_EPS) * ng_ref[...]
                og = og_ref[sl, :].astype(F32)
                out_ref[sl, :] = (y * (og * _sigmoid(og))).astype(BF)
            return st
        return lax.fori_loop(0, n_chunks, body, st)

    zero = jnp.zeros((dv, dk), F32)
    lat_f = (ql_ref, kl_ref, vl_ref, gfl_ref)
    lat_b = (ql_ref, kl_ref, vl_ref, gbl_ref)
    ctx_f = (qc_ref, kc_ref, vc_ref, gfc_ref)
    ctx_b = (qc_ref, kc_ref, vc_ref, gbc_ref)
    st = fwd_pass(ctx_f, fc_ref, n_ctx, zero)
    fwd_pass(lat_f, fl_ref, n_lat, st)
    st = bwd_pass(ctx_b, fc_ref, ogc_ref, oc_ref, n_ctx, zero, with_ctx_out)
    bwd_pass(lat_b, fl_ref, ogl_ref, ol_ref, n_lat, st, True)
    if not with_ctx_out:
        oc_ref[...] = jnp.zeros_like(oc_ref)


def _gla(proj_lat, gates_lat, proj_ctx, gates_ctx, norm_g, with_ctx_out):
    bsz, n_lat, _ = proj_lat.shape
    n_ctx = proj_ctx.shape[1]
    nh = C_HEADS
    dk = 512 // nh
    dv = 1024 // nh
    kb, vb = 512 // dk, 1024 // dv

    def specs(n_tok):
        return [
            pl.BlockSpec((None, n_tok, dk), lambda b, h: (b, 0, h)),
            pl.BlockSpec((None, n_tok, dk), lambda b, h: (b, 0, kb + h)),
            pl.BlockSpec((None, n_tok, dv), lambda b, h: (b, 0, (2 * kb * dk) // dv + h)),
            pl.BlockSpec((None, n_tok, dv), lambda b, h: (b, 0, (2 * kb * dk) // dv + vb + h)),
            pl.BlockSpec((None, n_tok, dk), lambda b, h: (b, 0, h)),
            pl.BlockSpec((None, n_tok, dk), lambda b, h: (b, 0, kb + h)),
        ]

    out_l, out_c = pl.pallas_call(
        functools.partial(_gla_kernel, with_ctx_out=with_ctx_out),
        out_shape=(jax.ShapeDtypeStruct((bsz, n_lat, nh * dv), BF),
                   jax.ShapeDtypeStruct((bsz, n_ctx, nh * dv), BF)),
        grid=(bsz, nh),
        in_specs=specs(n_lat) + specs(n_ctx) + [pl.BlockSpec((1, dv), lambda b, h: (0, 0))],
        out_specs=(pl.BlockSpec((None, n_lat, dv), lambda b, h: (b, 0, h)),
                   pl.BlockSpec((None, n_ctx, dv), lambda b, h: (b, 0, h))),
        scratch_shapes=[pltpu.VMEM((n_lat, dv), F32), pltpu.VMEM((n_ctx, dv), F32)],
        compiler_params=_cparams(2),
        name="gated_linear_attention",
    )(proj_lat, proj_lat, proj_lat, proj_lat, gates_lat, gates_lat,
      proj_ctx, proj_ctx, proj_ctx, proj_ctx, gates_ctx, gates_ctx, norm_g.reshape(1, dv))
    return out_l, out_c


def kernel(x, c, ctx, c_ctx, mod_w, mod_b, ln1_g, ln1_b, ffn_w_up, ffn_conv_w, ffn_conv_b, ffn_w_down, ln2_g, ln2_b, a_w_qkv, a_q_norm, a_k_norm, a_w_o, b_w_qkv, b_lambda_q1, b_lambda_k1, b_lambda_q2, b_lambda_k2, b_subln, b_w_o, c_w_in, c_w_gate_fwd, c_b_gate_fwd, c_w_gate_bwd, c_b_gate_bwd, c_norm, c_w_o, d_w_qkv, d_rpb, d_w_o):
    bsz, n_lat, dm = x.shape
    n_ctx = ctx.shape[1]
    scale = HEAD_DIM ** -0.5

    pad_rows = (-(bsz + 1)) % 8
    cond = jnp.concatenate([c, c_ctx[None, :], jnp.zeros((pad_rows, dm), F32)], axis=0)
    mod_all = _modulation(cond, mod_w, mod_b)

    x_lat, x_ctx = x, ctx
    for i in range(DEPTH):
        kind, j = i % N_MIXERS, i // N_MIXERS
        last = i == DEPTH - 1
        m_lat = mod_all[i, :bsz].reshape(bsz, N_MOD, dm)
        m_ctx = jnp.broadcast_to(mod_all[i, bsz].reshape(1, N_MOD, dm), (bsz, N_MOD, dm))

        if kind == 0:
            w = a_w_qkv[j].astype(BF)
            gains = (a_q_norm[j], a_k_norm[j])
            nq, nk = A_HEADS * HEAD_DIM // LANES, A_KV_HEADS * HEAD_DIM // LANES
            p_lat = _proj_qkv(x_lat, m_lat, w, n_q=nq, n_k=nk, n_v=nk, q_scale=scale, norm_gains=gains, rope=True)
            p_ctx = _proj_qkv(x_ctx, m_ctx, w, n_q=nq, n_k=nk, n_v=nk, q_scale=scale, norm_gains=gains)
            a_lat = _attn_gqa(p_lat, [p_ctx, p_lat])
            a_ctx = None if last else _attn_gqa(p_ctx, [p_ctx])
            w_o = a_w_o[j]
        elif kind == 1:
            w = b_w_qkv[j].astype(BF)
            nb = 2 * B_HEADS * HEAD_DIM // LANES
            p_lat = _proj_qkv(x_lat, m_lat, w, n_q=nb, n_k=nb, n_v=nb, q_scale=scale, rope=True)
            p_ctx = _proj_qkv(x_ctx, m_ctx, w, n_q=nb, n_k=nb, n_v=nb, q_scale=scale)
            lam_vecs = jnp.stack([b_lambda_q1[j], b_lambda_k1[j], b_lambda_q2[j], b_lambda_k2[j]])
            lam_init = 0.8 - 0.6 * math.exp(-0.3 * i)
            a_lat = _attn_diff(p_lat, [p_ctx, p_lat], lam_vecs, b_subln[j], lam_init)
            a_ctx = None if last else _attn_diff(p_ctx, [p_ctx], lam_vecs, b_subln[j], lam_init)
            w_o = b_w_o[j]
        elif kind == 2:
            w_in = c_w_in[j]
            n_main = w_in.shape[1] - 2 * C_GATE_RANK
            w_main = w_in[:, :n_main].astype(BF)
            w_z = jnp.pad(w_in[:, n_main:], ((0, 0), (0, LANES - 2 * C_GATE_RANK))).astype(BF)
            kd = c_w_gate_fwd.shape[-1]
            w_gate = jnp.zeros((LANES, 2 * kd), F32)
            w_gate = w_gate.at[:C_GATE_RANK, :kd].set(c_w_gate_fwd[j])
            w_gate = w_gate.at[C_GATE_RANK:2 * C_GATE_RANK, kd:].set(c_w_gate_bwd[j]).astype(BF)
            b_gate = jnp.concatenate([c_b_gate_fwd[j], c_b_gate_bwd[j]]).reshape(1, 2 * kd)
            q_scale = (kd // C_HEADS) ** -0.5
            nq = kd // LANES
            p_lat, g_lat = _proj_gla(x_lat, m_lat, w_main, w_z, w_gate, b_gate, q_scale=q_scale, n_q=nq)
            p_ctx, g_ctx = _proj_gla(x_ctx, m_ctx, w_main, w_z, w_gate, b_gate, q_scale=q_scale, n_q=nq)
            a_lat, a_ctx = _gla(p_lat, g_lat, p_ctx, g_ctx, c_norm[j], not last)
            w_o = c_w_o[j]
        else:
            w = d_w_qkv[j].astype(BF)
            nb = D_HEADS * HEAD_DIM // LANES
            p_lat = _proj_qkv(x_lat, m_lat, w, n_q=nb, n_k=nb, n_v=nb, q_scale=scale)
            p_ctx = _proj_qkv(x_ctx, m_ctx, w, n_q=nb, n_k=nb, n_v=nb, q_scale=scale)
            bias = _nb_bias_table(d_rpb[j], n_lat // GRID_W)
            a_lat = _attn_nb(p_lat, p_ctx, bias)
            a_ctx = None if last else _attn_ctx(p_ctx, nb)
            w_o = d_w_o[j]

        w_o = w_o.astype(BF)
        w_up = ffn_w_up[i].astype(BF)
        w_dn = ffn_w_down[i].astype(BF)
        x_lat = _out_proj_ln(a_lat, w_o, x_lat, m_lat, ln1_g[i], ln1_b[i], 2)
        x_lat = _conv_ffn_ln(x_lat, m_lat, w_up, ffn_conv_w[i], ffn_conv_b[i], w_dn, ln2_g[i], ln2_b[i])
        if not last:
            x_ctx = _out_proj_ln(a_ctx, w_o, x_ctx, m_ctx, ln1_g[i], ln1_b[i], 2)
            x_ctx = _conv_ffn_ln(x_ctx, m_ctx, w_up, ffn_conv_w[i], ffn_conv_b[i], w_dn, ln2_g[i], ln2_b[i])
    return x_lat
```

```python
import functools
import math

import jax
import jax.numpy as jnp
import numpy as np
from jax import lax
from jax.experimental import pallas as pl
from jax.experimental.pallas import tpu as pltpu

F32 = jnp.float32
BF = jnp.bfloat16

DEPTH = 4
N_MIXERS = 4
N_MOD = 6
GRID_W = 64
ROPE_THETA = 10000.0
NORM_EPS = 1e-6
LN_EPS = 1e-5
HEAD_DIM = 64
A_HEADS, A_KV_HEADS = 16, 4
B_HEADS = 8
C_HEADS = 4
C_GATE_RANK = 16
C_GATE_NORMALIZER = 16.0
C_CHUNK = 64
C_SUB = 16
GLA_FINISH_ROWS = 256
D_HEADS = 16
WIN_R, WIN_C = 8, 16
FF_DIM = 2816
DEEPNORM_ALPHA = (2 * DEPTH) ** 0.25
MASK_VALUE = -1e30

LANES = 128
MXU_WIDTH = 256
VMEM_LIMIT_BYTES = 56 * 1024 * 1024


def _cparams(n_axes):
    return pltpu.CompilerParams(
        dimension_semantics=("arbitrary",) * n_axes,
        vmem_limit_bytes=VMEM_LIMIT_BYTES)


def _dot(a, b):
    return jnp.dot(a, b, preferred_element_type=F32)


def _dot_nt(a, b):
    return lax.dot_general(a, b, (((1,), (1,)), ((), ())), preferred_element_type=F32)


def _dot_tn(a, b):
    return lax.dot_general(a, b, (((0,), (0,)), ((), ())), preferred_element_type=F32)


def _split_dot(x, m_bf16):
    hi = x.astype(BF)
    lo = (x - hi.astype(F32)).astype(BF)
    return _dot(hi, m_bf16) + _dot(lo, m_bf16)


def _split_dot_left(m_bf16, x):
    hi = x.astype(BF)
    lo = (x - hi.astype(F32)).astype(BF)
    return _dot(m_bf16, hi) + _dot(m_bf16, lo)


def _sigmoid(x):
    return 1.0 / (1.0 + jnp.exp(-x))


def _modulate(x_ref, mod_ref, shift_row, scale_row):
    x = x_ref[...]
    return x * (1.0 + mod_ref[scale_row:scale_row + 1, :]) + mod_ref[shift_row:shift_row + 1, :]


def _layer_norm(z, g, b):
    mu = jnp.mean(z, axis=-1, keepdims=True)
    zc = z - mu
    var = jnp.mean(zc * zc, axis=-1, keepdims=True)
    return zc * lax.rsqrt(var + LN_EPS) * g + b


def _rope(y, cos_t, sin_t, lo_half):
    swap = jnp.where(lo_half, pltpu.roll(y, LANES - 32, axis=1), pltpu.roll(y, 32, axis=1))
    return y * cos_t + swap * sin_t


def _idiv(x, n):
    return lax.shift_right_logical(x, int(n).bit_length() - 1)


def _imod(x, n):
    return x & (n - 1)


def _lane_iota(shape=(1, LANES)):
    return lax.broadcasted_iota(jnp.int32, shape, len(shape) - 1)


def _mod_kernel(cond_ref, w_ref, b_ref, o_ref):
    cnd = cond_ref[...]
    act = cnd * _sigmoid(cnd)
    o_ref[...] = _dot(act.astype(BF), w_ref[...].astype(BF)) + b_ref[...]


def _modulation(cond, mod_w, mod_b):
    depth, dm, n = mod_w.shape
    rows = cond.shape[0]
    tn = 1536
    return pl.pallas_call(
        _mod_kernel,
        out_shape=jax.ShapeDtypeStruct((depth, rows, n), F32),
        grid=(depth, n // tn),
        in_specs=[
            pl.BlockSpec((rows, dm), lambda i, j: (0, 0)),
            pl.BlockSpec((None, dm, tn), lambda i, j: (i, 0, j)),
            pl.BlockSpec((None, 1, tn), lambda i, j: (i, 0, j)),
        ],
        out_specs=pl.BlockSpec((None, rows, tn), lambda i, j: (i, 0, j)),
        compiler_params=_cparams(2),
        name="adaln_modulation",
    )(cond, mod_w, mod_b.reshape(depth, 1, n))


def _row_tile(n_tok):
    return 512 if n_tok % 512 == 0 else n_tok


def _proj_call(kernel, x, mod, w, extras, extra_specs, out_shapes, out_specs, name):
    bsz, n_tok, dm = x.shape
    tm = _row_tile(n_tok)
    in_specs = [
        pl.BlockSpec((None, tm, dm), lambda b, i: (b, i, 0)),
        pl.BlockSpec((None, N_MOD, dm), lambda b, i: (b, 0, 0)),
        pl.BlockSpec(w.shape, lambda b, i: (0, 0)),
    ] + list(extra_specs(tm))
    return pl.pallas_call(
        kernel,
        out_shape=out_shapes,
        grid=(bsz, n_tok // tm),
        in_specs=in_specs,
        out_specs=out_specs(tm),
        compiler_params=_cparams(2),
        name=name,
    )(x, mod, w, *extras)


def _full_spec(arr):
    nd = arr.ndim
    return pl.BlockSpec(arr.shape, lambda b, i: (0,) * nd)


def _seg_matrix():
    idx = np.arange(LANES) // HEAD_DIM
    return jnp.asarray((idx[:, None] == idx[None, :]).astype(np.float32) / HEAD_DIM, dtype=BF)


def _rope_tables(n_tok):
    t = jnp.arange(n_tok)
    row = (t // GRID_W).astype(F32)
    col = (t % GRID_W).astype(F32)
    n_freq = HEAD_DIM // 4
    inv = 1.0 / (ROPE_THETA ** (jnp.arange(n_freq, dtype=F32) / n_freq))
    ang = jnp.concatenate([row[:, None] * inv, col[:, None] * inv], axis=-1)
    cos, sin = jnp.cos(ang), jnp.sin(ang)
    cos_t = jnp.concatenate([cos, cos, cos, cos], axis=-1)
    sin_t = jnp.concatenate([-sin, sin, -sin, sin], axis=-1)
    return cos_t, sin_t


def _proj_qkv_kernel(x_ref, mod_ref, w_ref, *rest, n_q, n_k, n_v, norm, rope, q_scale):
    rest = list(rest)
    seg_ref = gq_ref = gk_ref = cos_ref = sin_ref = None
    if norm:
        seg_ref, gq_ref, gk_ref = rest[:3]
        rest = rest[3:]
    if rope:
        cos_ref, sin_ref = rest[:2]
        rest = rest[2:]
    (o_ref,) = rest
    hb = _modulate(x_ref, mod_ref, 0, 1).astype(BF)
    lo_half = _imod(_lane_iota(), HEAD_DIM) < (HEAD_DIM // 2)
    n_blocks = n_q + n_k + n_v
    per_dot = MXU_WIDTH // LANES
    for c0 in range(0, n_blocks, per_dot):
        nb = min(per_dot, n_blocks - c0)
        y2 = _dot(hb, w_ref[:, c0 * LANES:(c0 + nb) * LANES])
        for j in range(nb):
            c = c0 + j
            y = y2[:, j * LANES:(j + 1) * LANES]
            if c < n_q + n_k:
                if norm:
                    ms = _split_dot(y * y, seg_ref[...])
                    gain = gq_ref[...] if c < n_q else gk_ref[...]
                    y = y * lax.rsqrt(ms + NORM_EPS) * gain
                if rope:
                    y = _rope(y, cos_ref[...], sin_ref[...], lo_half)
                if c < n_q and q_scale != 1.0:
                    y = y * q_scale
            o_ref[:, c * LANES:(c + 1) * LANES] = y.astype(BF)


def _proj_qkv(x, mod, w, *, n_q, n_k, n_v, q_scale, norm_gains=None, rope=False):
    bsz, n_tok, _ = x.shape
    n_out = w.shape[1]
    extras, spec_fns = [], []
    if norm_gains is not None:
        gq, gk = norm_gains
        extras += [_seg_matrix(), jnp.tile(gq, 2).reshape(1, LANES), jnp.tile(gk, 2).reshape(1, LANES)]
        spec_fns += [lambda tm, a=a: _full_spec(a) for a in extras]
    if rope:
        cos_t, sin_t = _rope_tables(n_tok)
        extras += [cos_t, sin_t]
        spec_fns += [lambda tm: pl.BlockSpec((tm, LANES), lambda b, i: (i, 0))] * 2
    kern = functools.partial(_proj_qkv_kernel, n_q=n_q, n_k=n_k, n_v=n_v,
                             norm=norm_gains is not None, rope=rope, q_scale=q_scale)
    return _proj_call(
        kern, x, mod, w, extras, lambda tm: [f(tm) for f in spec_fns],
        jax.ShapeDtypeStruct((bsz, n_tok, n_out), BF),
        lambda tm: pl.BlockSpec((None, tm, n_out), lambda b, i: (b, i, 0)),
        "mixer_qkv_projection")


def _log_sigmoid(x):
    return jnp.minimum(x, 0.0) - jnp.log(1.0 + jnp.exp(-jnp.abs(x)))


def _proj_gla_kernel(x_ref, mod_ref, w_ref, wz_ref, wg_ref, bg_ref, o_ref, g_ref, *, q_scale, n_q):
    hb = _modulate(x_ref, mod_ref, 0, 1).astype(BF)
    n_blocks = o_ref.shape[-1] // LANES
    per_dot = MXU_WIDTH // LANES
    for c0 in range(0, n_blocks, per_dot):
        y2 = _dot(hb, w_ref[:, c0 * LANES:(c0 + per_dot) * LANES])
        if c0 < n_q:
            y2 = y2 * q_scale
        o_ref[:, c0 * LANES:(c0 + per_dot) * LANES] = y2.astype(BF)
    z = _dot(hb, wz_ref[...]).astype(BF)
    n_gate = g_ref.shape[-1]
    for c0 in range(0, n_gate, MXU_WIDTH):
        pre = _dot(z, wg_ref[:, c0:c0 + MXU_WIDTH]) + bg_ref[:, c0:c0 + MXU_WIDTH]
        g_ref[:, c0:c0 + MXU_WIDTH] = _log_sigmoid(pre) * (1.0 / C_GATE_NORMALIZER)


def _proj_gla(x, mod, w_main, w_z, w_gate, b_gate, *, q_scale, n_q):
    bsz, n_tok, _ = x.shape
    n_out, n_gate = w_main.shape[1], w_gate.shape[1]
    extras = [w_z, w_gate, b_gate]
    kern = functools.partial(_proj_gla_kernel, q_scale=q_scale, n_q=n_q)
    return _proj_call(
        kern, x, mod, w_main, extras, lambda tm: [_full_spec(a) for a in extras],
        (jax.ShapeDtypeStruct((bsz, n_tok, n_out), BF), jax.ShapeDtypeStruct((bsz, n_tok, n_gate), F32)),
        lambda tm: (pl.BlockSpec((None, tm, n_out), lambda b, i: (b, i, 0)),
                    pl.BlockSpec((None, tm, n_gate), lambda b, i: (b, i, 0))),
        "gla_projection")


def _out_kernel(a_ref, w_ref, x_ref, mod_ref, g_ref, b_ref, o_ref, *, gate_row):
    y = _dot(a_ref[...], w_ref[...])
    z = DEEPNORM_ALPHA * x_ref[...] + mod_ref[gate_row:gate_row + 1, :] * y
    o_ref[...] = _layer_norm(z, g_ref[...], b_ref[...])


def _out_proj_ln(a, w_o, x, mod, ln_g, ln_b, gate_row):
    bsz, n_tok, dm = x.shape
    k_in = a.shape[-1]
    tm = _row_tile(n_tok)
    return pl.pallas_call(
        functools.partial(_out_kernel, gate_row=gate_row),
        out_shape=jax.ShapeDtypeStruct((bsz, n_tok, dm), F32),
        grid=(bsz, n_tok // tm),
        in_specs=[
            pl.BlockSpec((None, tm, k_in), lambda b, i: (b, i, 0)),
            pl.BlockSpec((k_in, dm), lambda b, i: (0, 0)),
            pl.BlockSpec((None, tm, dm), lambda b, i: (b, i, 0)),
            pl.BlockSpec((None, N_MOD, dm), lambda b, i: (b, 0, 0)),
            pl.BlockSpec((1, dm), lambda b, i: (0, 0)),
            pl.BlockSpec((1, dm), lambda b, i: (0, 0)),
        ],
        out_specs=pl.BlockSpec((None, tm, dm), lambda b, i: (b, i, 0)),
        compiler_params=_cparams(2),
        name="out_projection_layernorm",
    )(a, w_o, x, mod, ln_g.reshape(1, dm), ln_b.reshape(1, dm))


FFN_CHUNK = 256
HALO = 8


def _ffn_kernel(x_ref, xp_ref, xn_ref, mod_ref, wu_ref, cw_ref, cb_ref, wd_ref, g_ref, b_ref,
                o_ref, act_ref):
    tm = x_ref.shape[0]
    i = pl.program_id(1)
    n_i = pl.num_programs(1)
    scale = 1.0 + mod_ref[4:5, :]
    shift = mod_ref[3:4, :]
    x = x_ref[...]
    h_ext = jnp.concatenate(
        [x * scale + shift, xp_ref[...] * scale + shift, xn_ref[...] * scale + shift], axis=0)
    hb = h_ext.astype(BF)
    rows = lax.broadcasted_iota(jnp.int32, (tm, 1), 0)
    first_row = rows == 0
    last_row = rows == tm - 1
    has_prev = (i > 0).astype(F32)
    has_next = (i < n_i - 1).astype(F32)

    def conv(u_ext, c0):
        u = u_ext[:tm]
        prev = u_ext[tm + HALO - 1:tm + HALO] * has_prev
        nxt = u_ext[tm + HALO:tm + HALO + 1] * has_next
        u_dn = jnp.where(first_row, prev, pltpu.roll(u, 1, axis=0))
        u_up = jnp.where(last_row, nxt, pltpu.roll(u, tm - 1, axis=0))
        w0 = cw_ref[0:1, c0:c0 + FFN_CHUNK]
        w1 = cw_ref[1:2, c0:c0 + FFN_CHUNK]
        w2 = cw_ref[2:3, c0:c0 + FFN_CHUNK]
        return cb_ref[:, c0:c0 + FFN_CHUNK] + u_dn * w0 + u * w1 + u_up * w2

    for c in range(FF_DIM // FFN_CHUNK):
        cg = c * FFN_CHUNK
        cv = FF_DIM + cg
        gate = conv(_dot(hb, wu_ref[:, cg:cg + FFN_CHUNK]), cg)
        val = conv(_dot(hb, wu_ref[:, cv:cv + FFN_CHUNK]), cv)
        act_ref[:, cg:cg + FFN_CHUNK] = (gate * _sigmoid(gate) * val).astype(BF)
    z = DEEPNORM_ALPHA * x + mod_ref[5:6, :] * _dot(act_ref[...], wd_ref[...])
    o_ref[...] = _layer_norm(z, g_ref[...], b_ref[...])


def _conv_ffn_ln(x, mod, w_up, conv_w, conv_b, w_down, ln_g, ln_b):
    bsz, n_tok, dm = x.shape
    tm = _row_tile(n_tok)
    n_halo = n_tok // HALO
    per = tm // HALO
    const2 = lambda b, i: (0, 0)
    return pl.pallas_call(
        _ffn_kernel,
        out_shape=jax.ShapeDtypeStruct((bsz, n_tok, dm), F32),
        grid=(bsz, n_tok // tm),
        in_specs=[
            pl.BlockSpec((None, tm, dm), lambda b, i: (b, i, 0)),
            pl.BlockSpec((None, HALO, dm), lambda b, i: (b, jnp.maximum(i * per - 1, 0), 0)),
            pl.BlockSpec((None, HALO, dm), lambda b, i: (b, jnp.minimum((i + 1) * per, n_halo - 1), 0)),
            pl.BlockSpec((None, N_MOD, dm), lambda b, i: (b, 0, 0)),
            pl.BlockSpec(w_up.shape, const2),
            pl.BlockSpec(conv_w.shape, const2),
            pl.BlockSpec((1, 2 * FF_DIM), const2),
            pl.BlockSpec(w_down.shape, const2),
            pl.BlockSpec((1, dm), const2),
            pl.BlockSpec((1, dm), const2),
        ],
        out_specs=pl.BlockSpec((None, tm, dm), lambda b, i: (b, i, 0)),
        scratch_shapes=[pltpu.VMEM((tm, FF_DIM), BF)],
        compiler_params=_cparams(2),
        name="conv_ffn_layernorm",
    )(x, x, x, mod, w_up, conv_w, conv_b.reshape(1, -1), w_down, ln_g.reshape(1, dm), ln_b.reshape(1, dm))


ATT_SUB = 256
ATT_TILE = 1024


def _attn_tile(n_tok):
    return ATT_TILE if n_tok % ATT_TILE == 0 else n_tok


def _softmax_pv(q, keys, values, biases):
    scores = []
    for k, bias in zip(keys, biases):
        s = _dot_nt(q, k)
        scores.append(s if bias is None else s + bias)
    m = functools.reduce(jnp.maximum, [jnp.max(s, axis=-1, keepdims=True) for s in scores])
    probs = [jnp.exp(s - m) for s in scores]
    denom = functools.reduce(lambda a, b: a + b, [jnp.sum(p, axis=-1, keepdims=True) for p in probs])
    out = functools.reduce(lambda a, b: a + b, [_dot(p.astype(BF), v) for p, v in zip(probs, values)])
    return out / denom


def _attn_gqa_kernel(q_ref, *rest, n_seg):
    k_refs, v_refs, o_ref = rest[:n_seg], rest[n_seg:2 * n_seg], rest[2 * n_seg]
    grp = A_HEADS // A_KV_HEADS
    kv_pos = (pl.program_id(1) * 2 // grp) % 2
    half = _idiv(_lane_iota(), HEAD_DIM)
    keys = [r[...] for r in k_refs]
    values = [r[...] for r in v_refs]
    for r0 in range(0, q_ref.shape[0], ATT_SUB):
        q = q_ref[r0:r0 + ATT_SUB, :].astype(F32)
        q_sw = pltpu.roll(q, HEAD_DIM, axis=1)
        outs = []
        for e in range(2):
            aligned = kv_pos == e
            q_e = jnp.where(aligned, q, q_sw)
            q_e = jnp.where(half == kv_pos, q_e, 0.0).astype(BF)
            o_e = _softmax_pv(q_e, keys, values, [None] * n_seg)
            outs.append(jnp.where(aligned, o_e, pltpu.roll(o_e, HEAD_DIM, axis=1)))
        o_ref[r0:r0 + ATT_SUB, :] = jnp.where(half == 0, outs[0], outs[1]).astype(BF)


def _attn_gqa(q_src, kv_srcs):
    bsz, n_q_tok, _ = q_src.shape
    tq = _attn_tile(n_q_tok)
    n_qb = A_HEADS * HEAD_DIM // LANES
    n_kb = A_KV_HEADS * HEAD_DIM // LANES
    per_kb = n_qb // n_kb
    n_seg = len(kv_srcs)
    k_specs = [pl.BlockSpec((None, s.shape[1], LANES), lambda b, p, i: (b, 0, n_qb + p // per_kb))
               for s in kv_srcs]
    v_specs = [pl.BlockSpec((None, s.shape[1], LANES), lambda b, p, i: (b, 0, n_qb + n_kb + p // per_kb))
               for s in kv_srcs]
    return pl.pallas_call(
        functools.partial(_attn_gqa_kernel, n_seg=n_seg),
        out_shape=jax.ShapeDtypeStruct((bsz, n_q_tok, n_qb * LANES), BF),
        grid=(bsz, n_qb, n_q_tok // tq),
        in_specs=[pl.BlockSpec((None, tq, LANES), lambda b, p, i: (b, i, p))] + k_specs + v_specs,
        out_specs=pl.BlockSpec((None, tq, LANES), lambda b, p, i: (b, i, p)),
        compiler_params=_cparams(3),
        name="gqa_attention",
    )(q_src, *kv_srcs, *kv_srcs)


def _attn_diff_kernel(q_ref, lam_ref, subln_ref, *rest, n_seg, lam_init):
    k_refs, v_refs, o_ref = rest[:n_seg], rest[n_seg:2 * n_seg], rest[2 * n_seg]
    lam_v = lam_ref[...]
    lam = (jnp.exp(jnp.sum(lam_v[0:1] * lam_v[1:2], axis=-1, keepdims=True))
           - jnp.exp(jnp.sum(lam_v[2:3] * lam_v[3:4], axis=-1, keepdims=True)) + lam_init)
    half = _idiv(_lane_iota(), HEAD_DIM)
    keys = [r[...] for r in k_refs]
    values = [r[...] for r in v_refs]
    for r0 in range(0, q_ref.shape[0], ATT_SUB):
        q = q_ref[r0:r0 + ATT_SUB, :]
        zero = jnp.zeros_like(q)
        o1 = _softmax_pv(jnp.where(half == 0, q, zero), keys, values, [None] * n_seg)
        o2 = _softmax_pv(jnp.where(half == 1, q, zero), keys, values, [None] * n_seg)
        o = o1 - lam * o2
        ms = jnp.mean(o * o, axis=-1, keepdims=True)
        o = o * lax.rsqrt(ms + NORM_EPS) * subln_ref[...] * (1.0 - lam_init)
        o_ref[r0:r0 + ATT_SUB, :] = o.astype(BF)


def _attn_diff(q_src, kv_srcs, lam_vecs, subln, lam_init):
    bsz, n_q_tok, _ = q_src.shape
    tq = _attn_tile(n_q_tok)
    nh = B_HEADS
    n_seg = len(kv_srcs)
    k_specs = [pl.BlockSpec((None, s.shape[1], LANES), lambda b, h, i: (b, 0, nh + h)) for s in kv_srcs]
    v_specs = [pl.BlockSpec((None, s.shape[1], LANES), lambda b, h, i: (b, 0, 2 * nh + h)) for s in kv_srcs]
    return pl.pallas_call(
        functools.partial(_attn_diff_kernel, n_seg=n_seg, lam_init=lam_init),
        out_shape=jax.ShapeDtypeStruct((bsz, n_q_tok, nh * LANES), BF),
        grid=(bsz, nh, n_q_tok // tq),
        in_specs=[pl.BlockSpec((None, tq, LANES), lambda b, h, i: (b, i, h)),
                  pl.BlockSpec(lam_vecs.shape, lambda b, h, i: (0, 0)),
                  pl.BlockSpec((1, LANES), lambda b, h, i: (0, 0))] + k_specs + v_specs,
        out_specs=pl.BlockSpec((None, tq, LANES), lambda b, h, i: (b, i, h)),
        compiler_params=_cparams(3),
        name="differential_attention",
    )(q_src, lam_vecs, subln.reshape(1, LANES), *kv_srcs, *kv_srcs)


NB_QROWS = 4
NB_BAND = 12


def _nb_band_start(g, rows):
    return int(np.clip(NB_QROWS * g - NB_QROWS, 0, rows - NB_BAND))


def _nb_config(g, rows):
    n_g = rows // NB_QROWS
    return 0 if g == 0 else (2 if g == n_g - 1 else 1)


def _nb_bias_table(rpb, rows):
    nh = rpb.shape[0]
    wr = min(WIN_R, rows)
    cq = np.arange(GRID_W)
    c0 = np.clip(cq - WIN_C // 2, 0, GRID_W - WIN_C)
    col_in = (cq[None, :] >= c0[:, None]) & (cq[None, :] < c0[:, None] + WIN_C)
    dc_idx = np.clip(cq[None, :] - cq[:, None], -(WIN_C - 1), WIN_C - 1) + WIN_C - 1
    n_dr = 2 * WIN_R - 1
    tiles = rpb[:, :, dc_idx]
    tiles = jnp.where(col_in[None, None], tiles, MASK_VALUE)
    tiles = jnp.concatenate([tiles, jnp.full((nh, 1, GRID_W, GRID_W), MASK_VALUE, F32)], axis=1)
    n_g = rows // NB_QROWS
    sel = np.full((3, NB_QROWS, NB_BAND), n_dr, dtype=np.int32)
    for cfg, g in ((0, 0), (1, 1), (2, n_g - 1)):
        start = _nb_band_start(g, rows)
        for a in range(NB_QROWS):
            r = NB_QROWS * g + a
            r0 = int(np.clip(r - wr // 2, 0, rows - wr))
            for jb in range(NB_BAND):
                kr = start + jb
                if r0 <= kr < r0 + wr:
                    sel[cfg, a, jb] = kr - r + (WIN_R - 1)
    big = tiles[:, sel]
    big = big.transpose(0, 1, 2, 4, 3, 5)
    return big.reshape(nh, 3, NB_QROWS * GRID_W, NB_BAND * GRID_W)


def _attn_nb_kernel(q_ref, kl_ref, vl_ref, kc_ref, vc_ref, bias_ref, o_ref, *, rows):
    half = _idiv(_lane_iota(), HEAD_DIM)
    kc = kc_ref[...]
    vc = vc_ref[...]
    tq = NB_QROWS * GRID_W
    tk = NB_BAND * GRID_W
    for g in range(rows // NB_QROWS):
        ks = _nb_band_start(g, rows) * GRID_W
        cfg = _nb_config(g, rows)
        q = q_ref[g * tq:(g + 1) * tq, :]
        kb = kl_ref[ks:ks + tk, :]
        vb = vl_ref[ks:ks + tk, :]
        zero = jnp.zeros_like(q)
        outs = []
        for e in range(2):
            q_e = jnp.where(half == e, q, zero)
            outs.append(_softmax_pv(q_e, [kb, kc], [vb, vc], [bias_ref[e, cfg], None]))
        o_ref[g * tq:(g + 1) * tq, :] = jnp.where(half == 0, outs[0], outs[1]).astype(BF)


def _attn_nb(qkv_lat, qkv_ctx, bias):
    bsz, n_tok, _ = qkv_lat.shape
    n_ctx = qkv_ctx.shape[1]
    rows = n_tok // GRID_W
    npair = D_HEADS * HEAD_DIM // LANES
    return pl.pallas_call(
        functools.partial(_attn_nb_kernel, rows=rows),
        out_shape=jax.ShapeDtypeStruct((bsz, n_tok, npair * LANES), BF),
        grid=(npair, bsz),
        in_specs=[
            pl.BlockSpec((None, n_tok, LANES), lambda p, b: (b, 0, p)),
            pl.BlockSpec((None, n_tok, LANES), lambda p, b: (b, 0, npair + p)),
            pl.BlockSpec((None, n_tok, LANES), lambda p, b: (b, 0, 2 * npair + p)),
            pl.BlockSpec((None, n_ctx, LANES), lambda p, b: (b, 0, npair + p)),
            pl.BlockSpec((None, n_ctx, LANES), lambda p, b: (b, 0, 2 * npair + p)),
            pl.BlockSpec((2,) + bias.shape[1:], lambda p, b: (p, 0, 0, 0)),
        ],
        out_specs=pl.BlockSpec((None, n_tok, LANES), lambda p, b: (b, 0, p)),
        compiler_params=_cparams(2),
        name="neighbourhood_attention",
    )(qkv_lat, qkv_lat, qkv_lat, qkv_ctx, qkv_ctx, bias)


def _attn_ctx_kernel(q_ref, k_ref, v_ref, o_ref):
    half = _idiv(_lane_iota(), HEAD_DIM)
    q = q_ref[...]
    zero = jnp.zeros_like(q)
    outs = [_softmax_pv(jnp.where(half == e, q, zero), [k_ref[...]], [v_ref[...]], [None]) for e in range(2)]
    o_ref[...] = jnp.where(half == 0, outs[0], outs[1]).astype(BF)


def _attn_ctx(qkv_ctx, npair):
    bsz, n_ctx, _ = qkv_ctx.shape
    return pl.pallas_call(
        _attn_ctx_kernel,
        out_shape=jax.ShapeDtypeStruct((bsz, n_ctx, npair * LANES), BF),
        grid=(bsz, npair),
        in_specs=[
            pl.BlockSpec((None, n_ctx, LANES), lambda b, p: (b, 0, p)),
            pl.BlockSpec((None, n_ctx, LANES), lambda b, p: (b, 0, npair + p)),
            pl.BlockSpec((None, n_ctx, LANES), lambda b, p: (b, 0, 2 * npair + p)),
        ],
        out_specs=pl.BlockSpec((None, n_ctx, LANES), lambda b, p: (b, 0, p)),
        compiler_params=_cparams(2),
        name="context_attention",
    )(qkv_ctx, qkv_ctx, qkv_ctx)


def _gla_chunk(q, k, v, g, st, rev, tri, row, col):
    nsub = C_CHUNK // C_SUB
    b = _split_dot_left(tri, g)
    b_tot = b[0:1] if rev else b[C_CHUNK - 1:C_CHUNK]
    o = _dot_nt((q * jnp.exp(b)).astype(BF), st.astype(BF))
    sub = _idiv(row, C_SUB)
    ref_rows = jnp.zeros_like(b)
    for a in range(nsub):
        if rev:
            if a == nsub - 1:
                continue
            bnd = b[(a + 1) * C_SUB:(a + 1) * C_SUB + 1]
        else:
            if a == 0:
                continue
            bnd = b[a * C_SUB - 1:a * C_SUB]
        ref_rows = jnp.where(sub == a, bnd, ref_rows)
    q_t = (q * jnp.exp(jnp.minimum(b - ref_rows, 0.0))).astype(BF)
    a_mat = jnp.zeros((C_CHUNK, C_CHUNK), F32)
    col_sub = _idiv(col, C_SUB)
    for a in range(nsub):
        if rev:
            if a == nsub - 1:
                continue
            bnd = b[(a + 1) * C_SUB:(a + 1) * C_SUB + 1]
            use = (sub == a) & (col_sub > a)
        else:
            if a == 0:
                continue
            bnd = b[a * C_SUB - 1:a * C_SUB]
            use = (sub == a) & (col_sub < a)
        k_t = (k * jnp.exp(jnp.minimum(bnd - b, 0.0))).astype(BF)
        a_mat = jnp.where(use, _dot_nt(q_t, k_t), a_mat)
    rin = _imod(row, C_SUB)
    for d in range(C_SUB):
        shift = (C_CHUNK - d) % C_CHUNK if rev else d
        k_d = k if d == 0 else pltpu.roll(k, shift, axis=0)
        b_d = b if d == 0 else pltpu.roll(b, shift, axis=0)
        term = q * k_d * jnp.exp(jnp.minimum(b - b_d, 0.0))
        diag = jnp.sum(term, axis=-1, keepdims=True)
        if rev:
            use = (col == row + d) & (rin + d < C_SUB)
        else:
            use = (col == row - d) & (rin >= d)
        a_mat = jnp.where(use, diag, a_mat)
    o = o + _dot(a_mat.astype(BF), v)
    k_end = (k * jnp.exp(b_tot - b)).astype(BF)
    st_new = st * jnp.exp(b_tot) + _dot_tn(v, k_end)
    return o, st_new


def _gla_kernel(ql_ref, kl_ref, vl_ref, ogl_ref, gfl_ref, gbl_ref,
                qc_ref, kc_ref, vc_ref, ogc_ref, gfc_ref, gbc_ref, ng_ref,
                ol_ref, oc_ref, fl_ref, bl_ref, fc_ref, bc_ref, *, with_ctx_out):
    dk = ql_ref.shape[-1]
    dv = vl_ref.shape[-1]
    n_lat = ql_ref.shape[0] // C_CHUNK
    n_ctx = qc_ref.shape[0] // C_CHUNK
    row = lax.broadcasted_iota(jnp.int32, (C_CHUNK, 1), 0)
    col = lax.broadcasted_iota(jnp.int32, (1, C_CHUNK), 1)
    tri_f = (col <= row).astype(BF)
    tri_b = (col >= row).astype(BF)

    def load(q_ref, k_ref, v_ref, g_ref, c):
        sl = pl.ds(pl.multiple_of(c * C_CHUNK, C_CHUNK), C_CHUNK)
        return q_ref[sl, :].astype(F32), k_ref[sl, :].astype(F32), v_ref[sl, :], g_ref[sl, :]

    def chunk_at(c):
        return pl.ds(pl.multiple_of(c * C_CHUNK, C_CHUNK), C_CHUNK)

    def scan_both(refs_f, refs_b, f_ref, b_ref, n_chunks, states):
        def body(j, carry):
            st_f, st_b = carry
            cb = n_chunks - 1 - j
            q, k, v, g = load(*refs_f, j)
            o, st_f = _gla_chunk(q, k, v, g, st_f, False, tri_f, row, col)
            f_ref[chunk_at(j), :] = o
            q, k, v, g = load(*refs_b, cb)
            o, st_b = _gla_chunk(q, k, v, g, st_b, True, tri_b, row, col)
            b_ref[chunk_at(cb), :] = o
            return st_f, st_b
        return lax.fori_loop(0, n_chunks, body, states, unroll=2)

    def finish(f_ref, b_ref, og_ref, out_ref):
        def body(i, carry):
            sl = pl.ds(pl.multiple_of(i * GLA_FINISH_ROWS, GLA_FINISH_ROWS), GLA_FINISH_ROWS)
            tot = f_ref[sl, :] + b_ref[sl, :]
            ms = jnp.mean(tot * tot, axis=-1, keepdims=True)
            y = tot * lax.rsqrt(ms + NORM_EPS) * ng_ref[...]
            og = og_ref[sl, :].astype(F32)
            out_ref[sl, :] = (y * (og * _sigmoid(og))).astype(BF)
            return carry
        lax.fori_loop(0, f_ref.shape[0] // GLA_FINISH_ROWS, body, 0)

    zero = jnp.zeros((dv, dk), F32)
    lat_f = (ql_ref, kl_ref, vl_ref, gfl_ref)
    lat_b = (ql_ref, kl_ref, vl_ref, gbl_ref)
    ctx_f = (qc_ref, kc_ref, vc_ref, gfc_ref)
    ctx_b = (qc_ref, kc_ref, vc_ref, gbc_ref)
    states = scan_both(ctx_f, ctx_b, fc_ref, bc_ref, n_ctx, (zero, zero))
    scan_both(lat_f, lat_b, fl_ref, bl_ref, n_lat, states)
    finish(fl_ref, bl_ref, ogl_ref, ol_ref)
    if with_ctx_out:
        finish(fc_ref, bc_ref, ogc_ref, oc_ref)
    else:
        oc_ref[...] = jnp.zeros_like(oc_ref)


def _gla(proj_lat, gates_lat, proj_ctx, gates_ctx, norm_g, with_ctx_out):
    bsz, n_lat, _ = proj_lat.shape
    n_ctx = proj_ctx.shape[1]
    nh = C_HEADS
    dk = 512 // nh
    dv = 1024 // nh
    kb, vb = 512 // dk, 1024 // dv

    def specs(n_tok):
        return [
            pl.BlockSpec((None, n_tok, dk), lambda b, h: (b, 0, h)),
            pl.BlockSpec((None, n_tok, dk), lambda b, h: (b, 0, kb + h)),
            pl.BlockSpec((None, n_tok, dv), lambda b, h: (b, 0, (2 * kb * dk) // dv + h)),
            pl.BlockSpec((None, n_tok, dv), lambda b, h: (b, 0, (2 * kb * dk) // dv + vb + h)),
            pl.BlockSpec((None, n_tok, dk), lambda b, h: (b, 0, h)),
            pl.BlockSpec((None, n_tok, dk), lambda b, h: (b, 0, kb + h)),
        ]

    out_l, out_c = pl.pallas_call(
        functools.partial(_gla_kernel, with_ctx_out=with_ctx_out),
        out_shape=(jax.ShapeDtypeStruct((bsz, n_lat, nh * dv), BF),
                   jax.ShapeDtypeStruct((bsz, n_ctx, nh * dv), BF)),
        grid=(bsz, nh),
        in_specs=specs(n_lat) + specs(n_ctx) + [pl.BlockSpec((1, dv), lambda b, h: (0, 0))],
        out_specs=(pl.BlockSpec((None, n_lat, dv), lambda b, h: (b, 0, h)),
                   pl.BlockSpec((None, n_ctx, dv), lambda b, h: (b, 0, h))),
        scratch_shapes=[pltpu.VMEM((n_lat, dv), F32), pltpu.VMEM((n_lat, dv), F32),
                        pltpu.VMEM((n_ctx, dv), F32), pltpu.VMEM((n_ctx, dv), F32)],
        compiler_params=_cparams(2),
        name="gated_linear_attention",
    )(proj_lat, proj_lat, proj_lat, proj_lat, gates_lat, gates_lat,
      proj_ctx, proj_ctx, proj_ctx, proj_ctx, gates_ctx, gates_ctx, norm_g.reshape(1, dv))
    return out_l, out_c


def kernel(x, c, ctx, c_ctx, mod_w, mod_b, ln1_g, ln1_b, ffn_w_up, ffn_conv_w, ffn_conv_b, ffn_w_down, ln2_g, ln2_b, a_w_qkv, a_q_norm, a_k_norm, a_w_o, b_w_qkv, b_lambda_q1, b_lambda_k1, b_lambda_q2, b_lambda_k2, b_subln, b_w_o, c_w_in, c_w_gate_fwd, c_b_gate_fwd, c_w_gate_bwd, c_b_gate_bwd, c_norm, c_w_o, d_w_qkv, d_rpb, d_w_o):
    bsz, n_lat, dm = x.shape
    n_ctx = ctx.shape[1]
    scale = HEAD_DIM ** -0.5

    pad_rows = (-(bsz + 1)) % 8
    cond = jnp.concatenate([c, c_ctx[None, :], jnp.zeros((pad_rows, dm), F32)], axis=0)
    mod_all = _modulation(cond, mod_w, mod_b)

    x_lat, x_ctx = x, ctx
    for i in range(DEPTH):
        kind, j = i % N_MIXERS, i // N_MIXERS
        last = i == DEPTH - 1
        m_lat = mod_all[i, :bsz].reshape(bsz, N_MOD, dm)
        m_ctx = jnp.broadcast_to(mod_all[i, bsz].reshape(1, N_MOD, dm), (bsz, N_MOD, dm))

        if kind == 0:
            w = a_w_qkv[j].astype(BF)
            gains = (a_q_norm[j], a_k_norm[j])
            nq, nk = A_HEADS * HEAD_DIM // LANES, A_KV_HEADS * HEAD_DIM // LANES
            p_lat = _proj_qkv(x_lat, m_lat, w, n_q=nq, n_k=nk, n_v=nk, q_scale=scale, norm_gains=gains, rope=True)
            p_ctx = _proj_qkv(x_ctx, m_ctx, w, n_q=nq, n_k=nk, n_v=nk, q_scale=scale, norm_gains=gains)
            a_lat = _attn_gqa(p_lat, [p_ctx, p_lat])
            a_ctx = None if last else _attn_gqa(p_ctx, [p_ctx])
            w_o = a_w_o[j]
        elif kind == 1:
            w = b_w_qkv[j].astype(BF)
            nb = 2 * B_HEADS * HEAD_DIM // LANES
            p_lat = _proj_qkv(x_lat, m_lat, w, n_q=nb, n_k=nb, n_v=nb, q_scale=scale, rope=True)
            p_ctx = _proj_qkv(x_ctx, m_ctx, w, n_q=nb, n_k=nb, n_v=nb, q_scale=scale)
            lam_vecs = jnp.stack([b_lambda_q1[j], b_lambda_k1[j], b_lambda_q2[j], b_lambda_k2[j]])
            lam_init = 0.8 - 0.6 * math.exp(-0.3 * i)
            a_lat = _attn_diff(p_lat, [p_ctx, p_lat], lam_vecs, b_subln[j], lam_init)
            a_ctx = None if last else _attn_diff(p_ctx, [p_ctx], lam_vecs, b_subln[j], lam_init)
            w_o = b_w_o[j]
        elif kind == 2:
            w_in = c_w_in[j]
            n_main = w_in.shape[1] - 2 * C_GATE_RANK
            w_main = w_in[:, :n_main].astype(BF)
            w_z = jnp.pad(w_in[:, n_main:], ((0, 0), (0, LANES - 2 * C_GATE_RANK))).astype(BF)
            kd = c_w_gate_fwd.shape[-1]
            w_gate = jnp.zeros((LANES, 2 * kd), F32)
            w_gate = w_gate.at[:C_GATE_RANK, :kd].set(c_w_gate_fwd[j])
            w_gate = w_gate.at[C_GATE_RANK:2 * C_GATE_RANK, kd:].set(c_w_gate_bwd[j]).astype(BF)
            b_gate = jnp.concatenate([c_b_gate_fwd[j], c_b_gate_bwd[j]]).reshape(1, 2 * kd)
            q_scale = (kd // C_HEADS) ** -0.5
            nq = kd // LANES
            p_lat, g_lat = _proj_gla(x_lat, m_lat, w_main, w_z, w_gate, b_gate, q_scale=q_scale, n_q=nq)
            p_ctx, g_ctx = _proj_gla(x_ctx, m_ctx, w_main, w_z, w_gate, b_gate, q_scale=q_scale, n_q=nq)
            a_lat, a_ctx = _gla(p_lat, g_lat, p_ctx, g_ctx, c_norm[j], not last)
            w_o = c_w_o[j]
        else:
            w = d_w_qkv[j].astype(BF)
            nb = D_HEADS * HEAD_DIM // LANES
            p_lat = _proj_qkv(x_lat, m_lat, w, n_q=nb, n_k=nb, n_v=nb, q_scale=scale)
            p_ctx = _proj_qkv(x_ctx, m_ctx, w, n_q=nb, n_k=nb, n_v=nb, q_scale=scale)
            bias = _nb_bias_table(d_rpb[j], n_lat // GRID_W)
            a_lat = _attn_nb(p_lat, p_ctx, bias)
            a_ctx = None if last else _attn_ctx(p_ctx, nb)
            w_o = d_w_o[j]

        w_o = w_o.astype(BF)
        w_up = ffn_w_up[i].astype(BF)
        w_dn = ffn_w_down[i].astype(BF)
        x_lat = _out_proj_ln(a_lat, w_o, x_lat, m_lat, ln1_g[i], ln1_b[i], 2)
        x_lat = _conv_ffn_ln(x_lat, m_lat, w_up, ffn_conv_w[i], ffn_conv_b[i], w_dn, ln2_g[i], ln2_b[i])
        if not last:
            x_ctx = _out_proj_ln(a_ctx, w_o, x_ctx, m_ctx, ln1_g[i], ln1_b[i], 2)
            x_ctx = _conv_ffn_ln(x_ctx, m_ctx, w_up, ffn_conv_w[i], ffn_conv_b[i], w_dn, ln2_g[i], ln2_b[i])
    return x_lat
```

```python
import functools
import math

import jax
import jax.numpy as jnp
import numpy as np
from jax import lax
from jax.experimental import pallas as pl
from jax.experimental.pallas import tpu as pltpu

F32 = jnp.float32
BF = jnp.bfloat16

DEPTH = 4
N_MIXERS = 4
N_MOD = 6
GRID_W = 64
ROPE_THETA = 10000.0
NORM_EPS = 1e-6
LN_EPS = 1e-5
HEAD_DIM = 64
A_HEADS, A_KV_HEADS = 16, 4
B_HEADS = 8
C_HEADS = 4
C_GATE_RANK = 16
C_GATE_NORMALIZER = 16.0
C_CHUNK = 64
C_SUB = 16
GLA_FINISH_ROWS = 256
D_HEADS = 16
WIN_R, WIN_C = 8, 16
FF_DIM = 2816
DEEPNORM_ALPHA = (2 * DEPTH) ** 0.25
MASK_VALUE = -1e30
LOG2E = math.log2(math.e)

LANES = 128
MXU_WIDTH = 256
VMEM_LIMIT_BYTES = 56 * 1024 * 1024


def _cparams(n_axes):
    return pltpu.CompilerParams(
        dimension_semantics=("arbitrary",) * n_axes,
        vmem_limit_bytes=VMEM_LIMIT_BYTES)


def _dot(a, b):
    return jnp.dot(a, b, preferred_element_type=F32)


def _dot_nt(a, b):
    return lax.dot_general(a, b, (((1,), (1,)), ((), ())), preferred_element_type=F32)


def _dot_tn(a, b):
    return lax.dot_general(a, b, (((0,), (0,)), ((), ())), preferred_element_type=F32)


def _split_dot(x, m_bf16):
    hi = x.astype(BF)
    lo = (x - hi.astype(F32)).astype(BF)
    return _dot(hi, m_bf16) + _dot(lo, m_bf16)


def _split_dot_left(m_bf16, x):
    hi = x.astype(BF)
    lo = (x - hi.astype(F32)).astype(BF)
    return _dot(m_bf16, hi) + _dot(m_bf16, lo)


def _sigmoid(x):
    return 1.0 / (1.0 + jnp.exp(-x))


def _modulate(x_ref, mod_ref, shift_row, scale_row):
    x = x_ref[...]
    return x * (1.0 + mod_ref[scale_row:scale_row + 1, :]) + mod_ref[shift_row:shift_row + 1, :]


def _layer_norm(z, g, b):
    mu = jnp.mean(z, axis=-1, keepdims=True)
    zc = z - mu
    var = jnp.mean(zc * zc, axis=-1, keepdims=True)
    return zc * lax.rsqrt(var + LN_EPS) * g + b


def _rope(y, cos_t, sin_t, lo_half):
    swap = jnp.where(lo_half, pltpu.roll(y, LANES - 32, axis=1), pltpu.roll(y, 32, axis=1))
    return y * cos_t + swap * sin_t


def _idiv(x, n):
    return lax.shift_right_logical(x, int(n).bit_length() - 1)


def _imod(x, n):
    return x & (n - 1)


def _lane_iota(shape=(1, LANES)):
    return lax.broadcasted_iota(jnp.int32, shape, len(shape) - 1)


def _mod_kernel(cond_ref, w_ref, b_ref, o_ref):
    cnd = cond_ref[...]
    act = cnd * _sigmoid(cnd)
    o_ref[...] = _dot(act.astype(BF), w_ref[...].astype(BF)) + b_ref[...]


def _modulation(cond, mod_w, mod_b):
    depth, dm, n = mod_w.shape
    rows = cond.shape[0]
    tn = 1536
    return pl.pallas_call(
        _mod_kernel,
        out_shape=jax.ShapeDtypeStruct((depth, rows, n), F32),
        grid=(depth, n // tn),
        in_specs=[
            pl.BlockSpec((rows, dm), lambda i, j: (0, 0)),
            pl.BlockSpec((None, dm, tn), lambda i, j: (i, 0, j)),
            pl.BlockSpec((None, 1, tn), lambda i, j: (i, 0, j)),
        ],
        out_specs=pl.BlockSpec((None, rows, tn), lambda i, j: (i, 0, j)),
        compiler_params=_cparams(2),
        name="adaln_modulation",
    )(cond, mod_w, mod_b.reshape(depth, 1, n))


def _row_tile(n_tok):
    return 512 if n_tok % 512 == 0 else n_tok


def _proj_call(kernel, x, mod, w, extras, extra_specs, out_shapes, out_specs, name):
    bsz, n_tok, dm = x.shape
    tm = _row_tile(n_tok)
    in_specs = [
        pl.BlockSpec((None, tm, dm), lambda b, i: (b, i, 0)),
        pl.BlockSpec((None, N_MOD, dm), lambda b, i: (b, 0, 0)),
        pl.BlockSpec(w.shape, lambda b, i: (0, 0)),
    ] + list(extra_specs(tm))
    return pl.pallas_call(
        kernel,
        out_shape=out_shapes,
        grid=(bsz, n_tok // tm),
        in_specs=in_specs,
        out_specs=out_specs(tm),
        compiler_params=_cparams(2),
        name=name,
    )(x, mod, w, *extras)


def _full_spec(arr):
    nd = arr.ndim
    return pl.BlockSpec(arr.shape, lambda b, i: (0,) * nd)


def _seg_matrix():
    idx = np.arange(LANES) // HEAD_DIM
    return jnp.asarray((idx[:, None] == idx[None, :]).astype(np.float32) / HEAD_DIM, dtype=BF)


def _rope_tables(n_tok):
    t = jnp.arange(n_tok)
    row = (t // GRID_W).astype(F32)
    col = (t % GRID_W).astype(F32)
    n_freq = HEAD_DIM // 4
    inv = 1.0 / (ROPE_THETA ** (jnp.arange(n_freq, dtype=F32) / n_freq))
    ang = jnp.concatenate([row[:, None] * inv, col[:, None] * inv], axis=-1)
    cos, sin = jnp.cos(ang), jnp.sin(ang)
    cos_t = jnp.concatenate([cos, cos, cos, cos], axis=-1)
    sin_t = jnp.concatenate([-sin, sin, -sin, sin], axis=-1)
    return cos_t, sin_t


def _proj_qkv_kernel(x_ref, mod_ref, w_ref, *rest, n_q, n_k, n_v, norm, rope, q_scale):
    rest = list(rest)
    seg_ref = gq_ref = gk_ref = cos_ref = sin_ref = None
    if norm:
        seg_ref, gq_ref, gk_ref = rest[:3]
        rest = rest[3:]
    if rope:
        cos_ref, sin_ref = rest[:2]
        rest = rest[2:]
    (o_ref,) = rest
    hb = _modulate(x_ref, mod_ref, 0, 1).astype(BF)
    lo_half = _imod(_lane_iota(), HEAD_DIM) < (HEAD_DIM // 2)
    n_blocks = n_q + n_k + n_v
    per_dot = MXU_WIDTH // LANES
    for c0 in range(0, n_blocks, per_dot):
        nb = min(per_dot, n_blocks - c0)
        y2 = _dot(hb, w_ref[:, c0 * LANES:(c0 + nb) * LANES])
        for j in range(nb):
            c = c0 + j
            y = y2[:, j * LANES:(j + 1) * LANES]
            if c < n_q + n_k:
                if norm:
                    ms = _split_dot(y * y, seg_ref[...])
                    gain = gq_ref[...] if c < n_q else gk_ref[...]
                    y = y * lax.rsqrt(ms + NORM_EPS) * gain
                if rope:
                    y = _rope(y, cos_ref[...], sin_ref[...], lo_half)
                if c < n_q and q_scale != 1.0:
                    y = y * q_scale
            o_ref[:, c * LANES:(c + 1) * LANES] = y.astype(BF)


def _proj_qkv(x, mod, w, *, n_q, n_k, n_v, q_scale, norm_gains=None, rope=False):
    bsz, n_tok, _ = x.shape
    n_out = w.shape[1]
    extras, spec_fns = [], []
    if norm_gains is not None:
        gq, gk = norm_gains
        extras += [_seg_matrix(), jnp.tile(gq, 2).reshape(1, LANES), jnp.tile(gk, 2).reshape(1, LANES)]
        spec_fns += [lambda tm, a=a: _full_spec(a) for a in extras]
    if rope:
        cos_t, sin_t = _rope_tables(n_tok)
        extras += [cos_t, sin_t]
        spec_fns += [lambda tm: pl.BlockSpec((tm, LANES), lambda b, i: (i, 0))] * 2
    kern = functools.partial(_proj_qkv_kernel, n_q=n_q, n_k=n_k, n_v=n_v,
                             norm=norm_gains is not None, rope=rope, q_scale=q_scale)
    return _proj_call(
        kern, x, mod, w, extras, lambda tm: [f(tm) for f in spec_fns],
        jax.ShapeDtypeStruct((bsz, n_tok, n_out), BF),
        lambda tm: pl.BlockSpec((None, tm, n_out), lambda b, i: (b, i, 0)),
        "mixer_qkv_projection")


def _log_sigmoid(x):
    return jnp.minimum(x, 0.0) - jnp.log(1.0 + jnp.exp(-jnp.abs(x)))


def _proj_gla_kernel(x_ref, mod_ref, w_ref, wz_ref, wg_ref, bg_ref, o_ref, g_ref, *, q_scale, n_q):
    hb = _modulate(x_ref, mod_ref, 0, 1).astype(BF)
    n_blocks = o_ref.shape[-1] // LANES
    per_dot = MXU_WIDTH // LANES
    for c0 in range(0, n_blocks, per_dot):
        y2 = _dot(hb, w_ref[:, c0 * LANES:(c0 + per_dot) * LANES])
        if c0 < n_q:
            y2 = y2 * q_scale
        o_ref[:, c0 * LANES:(c0 + per_dot) * LANES] = y2.astype(BF)
    z = _dot(hb, wz_ref[...]).astype(BF)
    n_gate = g_ref.shape[-1]
    for c0 in range(0, n_gate, MXU_WIDTH):
        pre = _dot(z, wg_ref[:, c0:c0 + MXU_WIDTH]) + bg_ref[:, c0:c0 + MXU_WIDTH]
        g_ref[:, c0:c0 + MXU_WIDTH] = _log_sigmoid(pre) * (1.0 / C_GATE_NORMALIZER)


def _proj_gla(x, mod, w_main, w_z, w_gate, b_gate, *, q_scale, n_q):
    bsz, n_tok, _ = x.shape
    n_out, n_gate = w_main.shape[1], w_gate.shape[1]
    extras = [w_z, w_gate, b_gate]
    kern = functools.partial(_proj_gla_kernel, q_scale=q_scale, n_q=n_q)
    return _proj_call(
        kern, x, mod, w_main, extras, lambda tm: [_full_spec(a) for a in extras],
        (jax.ShapeDtypeStruct((bsz, n_tok, n_out), BF), jax.ShapeDtypeStruct((bsz, n_tok, n_gate), F32)),
        lambda tm: (pl.BlockSpec((None, tm, n_out), lambda b, i: (b, i, 0)),
                    pl.BlockSpec((None, tm, n_gate), lambda b, i: (b, i, 0))),
        "gla_projection")


FFN_CHUNK = 256
FFN_TILE = 512
FFN_SUB = 512
HALO = 16


def _mix_ffn_kernel(a_ref, ap_ref, an_ref, x_ref, xp_ref, xn_ref, mod_ref, wo_ref, g1_ref, b1_ref,
                    wu_ref, cw_ref, cb_ref, wd_ref, g2_ref, b2_ref, o_ref, act_ref):
    tm = x_ref.shape[0]
    i = pl.program_id(1)
    n_i = pl.num_programs(1)
    ts = min(tm, FFN_SUB)
    n_sub = tm // ts
    rows = lax.broadcasted_iota(jnp.int32, (ts, 1), 0)
    first_row = rows == 0
    last_row = rows == ts - 1

    for s in range(n_sub):
        r0 = s * ts
        if s == 0:
            a_prev, x_prev, has_prev = ap_ref[...], xp_ref[...], (i > 0).astype(F32)
        else:
            a_prev, x_prev, has_prev = a_ref[r0 - HALO:r0, :], x_ref[r0 - HALO:r0, :], 1.0
        if s == n_sub - 1:
            a_next, x_next, has_next = an_ref[...], xn_ref[...], (i < n_i - 1).astype(F32)
        else:
            a_next, x_next, has_next = a_ref[r0 + ts:r0 + ts + HALO, :], x_ref[r0 + ts:r0 + ts + HALO, :], 1.0
        a_ext = jnp.concatenate([a_ref[r0:r0 + ts, :], a_prev, a_next], axis=0)
        x_ext = jnp.concatenate([x_ref[r0:r0 + ts, :], x_prev, x_next], axis=0)
        z1 = DEEPNORM_ALPHA * x_ext + mod_ref[2:3, :] * _dot(a_ext, wo_ref[...])
        x_mid = _layer_norm(z1, g1_ref[...], b1_ref[...])
        hb = (x_mid * (1.0 + mod_ref[4:5, :]) + mod_ref[3:4, :]).astype(BF)

        def conv(u_ext, c0):
            u = u_ext[:ts]
            prev = u_ext[ts + HALO - 1:ts + HALO] * has_prev
            nxt = u_ext[ts + HALO:ts + HALO + 1] * has_next
            u_dn = jnp.where(first_row, prev, pltpu.roll(u, 1, axis=0))
            u_up = jnp.where(last_row, nxt, pltpu.roll(u, ts - 1, axis=0))
            w0 = cw_ref[0:1, c0:c0 + FFN_CHUNK]
            w1 = cw_ref[1:2, c0:c0 + FFN_CHUNK]
            w2 = cw_ref[2:3, c0:c0 + FFN_CHUNK]
            return cb_ref[:, c0:c0 + FFN_CHUNK] + u_dn * w0 + u * w1 + u_up * w2

        for c in range(FF_DIM // FFN_CHUNK):
            cg = c * FFN_CHUNK
            cv = FF_DIM + cg
            gate = conv(_dot(hb, wu_ref[:, cg:cg + FFN_CHUNK]), cg)
            val = conv(_dot(hb, wu_ref[:, cv:cv + FFN_CHUNK]), cv)
            act_ref[r0:r0 + ts, cg:cg + FFN_CHUNK] = (gate * _sigmoid(gate) * val).astype(BF)
        z2 = DEEPNORM_ALPHA * x_mid[:ts] + mod_ref[5:6, :] * _dot(act_ref[r0:r0 + ts, :], wd_ref[...])
        o_ref[r0:r0 + ts, :] = _layer_norm(z2, g2_ref[...], b2_ref[...])


def _mix_ffn(a, x, mod, w_o, ln1_g, ln1_b, w_up, conv_w, conv_b, w_down, ln2_g, ln2_b):
    bsz, n_tok, dm = x.shape
    k_in = a.shape[-1]
    tm = FFN_TILE if n_tok % FFN_TILE == 0 else n_tok
    n_halo = n_tok // HALO
    per = tm // HALO
    tile = lambda b, i: (b, i, 0)
    prev = lambda b, i: (b, jnp.maximum(i * per - 1, 0), 0)
    nxt = lambda b, i: (b, jnp.minimum((i + 1) * per, n_halo - 1), 0)
    const2 = lambda b, i: (0, 0)
    resident = functools.partial(pl.BlockSpec, index_map=const2, pipeline_mode=pl.Buffered(1))
    row = lambda v: v.reshape(1, -1)
    return pl.pallas_call(
        _mix_ffn_kernel,
        out_shape=jax.ShapeDtypeStruct((bsz, n_tok, dm), F32),
        grid=(bsz, n_tok // tm),
        in_specs=[
            pl.BlockSpec((None, tm, k_in), tile),
            pl.BlockSpec((None, HALO, k_in), prev),
            pl.BlockSpec((None, HALO, k_in), nxt),
            pl.BlockSpec((None, tm, dm), tile),
            pl.BlockSpec((None, HALO, dm), prev),
            pl.BlockSpec((None, HALO, dm), nxt),
            pl.BlockSpec((None, N_MOD, dm), lambda b, i: (b, 0, 0)),
            resident(w_o.shape),
            pl.BlockSpec((1, dm), const2),
            pl.BlockSpec((1, dm), const2),
            resident(w_up.shape),
            pl.BlockSpec(conv_w.shape, const2),
            pl.BlockSpec((1, 2 * FF_DIM), const2),
            resident(w_down.shape),
            pl.BlockSpec((1, dm), const2),
            pl.BlockSpec((1, dm), const2),
        ],
        out_specs=pl.BlockSpec((None, tm, dm), tile),
        scratch_shapes=[pltpu.VMEM((tm, FF_DIM), BF)],
        compiler_params=_cparams(2),
        name="outproj_convffn_layernorm",
    )(a, a, a, x, x, x, mod, w_o, row(ln1_g), row(ln1_b), w_up, conv_w, row(conv_b), w_down,
      row(ln2_g), row(ln2_b))


ATT_SUB = 512
ATT_TILE = 2048


def _attn_tile(n_tok):
    return ATT_TILE if n_tok % ATT_TILE == 0 else n_tok


def _softmax_parts(q, keys, biases):
    scores = []
    for k, bias in zip(keys, biases):
        s = _dot_nt(q, k)
        scores.append(s if bias is None else s + bias)
    m = functools.reduce(jnp.maximum, [jnp.max(s, axis=-1, keepdims=True) for s in scores])
    probs = [jnp.exp2(s - m) for s in scores]
    denom = functools.reduce(lambda a, b: a + b, [jnp.sum(p, axis=-1, keepdims=True) for p in probs])
    return probs, denom


def _pv(probs, values):
    return functools.reduce(lambda a, b: a + b, [_dot(p.astype(BF), v) for p, v in zip(probs, values)])


def _softmax_pv(q, keys, values, biases):
    probs, denom = _softmax_parts(q, keys, biases)
    return _pv(probs, values) / denom


def _attn_gqa_kernel(q_ref, *rest, n_seg):
    k_refs, v_refs, o_ref = rest[:n_seg], rest[n_seg:2 * n_seg], rest[2 * n_seg]
    grp = A_HEADS // A_KV_HEADS
    kv_pos = (pl.program_id(1) * 2 // grp) % 2
    half = _idiv(_lane_iota(), HEAD_DIM)
    keys = [r[...] for r in k_refs]
    values = [r[...] for r in v_refs]
    for r0 in range(0, q_ref.shape[0], ATT_SUB):
        q = q_ref[r0:r0 + ATT_SUB, :].astype(F32)
        q_sw = pltpu.roll(q, HEAD_DIM, axis=1)
        outs = []
        for e in range(2):
            aligned = kv_pos == e
            q_e = jnp.where(aligned, q, q_sw)
            q_e = jnp.where(half == kv_pos, q_e, 0.0).astype(BF)
            o_e = _softmax_pv(q_e, keys, values, [None] * n_seg)
            outs.append(jnp.where(aligned, o_e, pltpu.roll(o_e, HEAD_DIM, axis=1)))
        o_ref[r0:r0 + ATT_SUB, :] = jnp.where(half == 0, outs[0], outs[1]).astype(BF)


def _attn_gqa(q_src, kv_srcs):
    bsz, n_q_tok, _ = q_src.shape
    tq = _attn_tile(n_q_tok)
    n_qb = A_HEADS * HEAD_DIM // LANES
    n_kb = A_KV_HEADS * HEAD_DIM // LANES
    per_kb = n_qb // n_kb
    n_seg = len(kv_srcs)
    k_specs = [pl.BlockSpec((None, s.shape[1], LANES), lambda b, p, i: (b, 0, n_qb + p // per_kb))
               for s in kv_srcs]
    v_specs = [pl.BlockSpec((None, s.shape[1], LANES), lambda b, p, i: (b, 0, n_qb + n_kb + p // per_kb))
               for s in kv_srcs]
    return pl.pallas_call(
        functools.partial(_attn_gqa_kernel, n_seg=n_seg),
        out_shape=jax.ShapeDtypeStruct((bsz, n_q_tok, n_qb * LANES), BF),
        grid=(bsz, n_qb, n_q_tok // tq),
        in_specs=[pl.BlockSpec((None, tq, LANES), lambda b, p, i: (b, i, p))] + k_specs + v_specs,
        out_specs=pl.BlockSpec((None, tq, LANES), lambda b, p, i: (b, i, p)),
        compiler_params=_cparams(3),
        name="gqa_attention",
    )(q_src, *kv_srcs, *kv_srcs)


def _attn_diff_kernel(q_ref, lam_ref, subln_ref, *rest, n_seg, lam_init):
    k_refs, v_refs, o_ref = rest[:n_seg], rest[n_seg:2 * n_seg], rest[2 * n_seg]
    lam_v = lam_ref[...]
    lam = (jnp.exp(jnp.sum(lam_v[0:1] * lam_v[1:2], axis=-1, keepdims=True))
           - jnp.exp(jnp.sum(lam_v[2:3] * lam_v[3:4], axis=-1, keepdims=True)) + lam_init)
    half = _idiv(_lane_iota(), HEAD_DIM)
    keys = [r[...] for r in k_refs]
    values = [r[...] for r in v_refs]
    for r0 in range(0, q_ref.shape[0], ATT_SUB):
        q = q_ref[r0:r0 + ATT_SUB, :]
        zero = jnp.zeros_like(q)
        o1 = _softmax_pv(jnp.where(half == 0, q, zero), keys, values, [None] * n_seg)
        o2 = _softmax_pv(jnp.where(half == 1, q, zero), keys, values, [None] * n_seg)
        o = o1 - lam * o2
        ms = jnp.mean(o * o, axis=-1, keepdims=True)
        o = o * lax.rsqrt(ms + NORM_EPS) * subln_ref[...] * (1.0 - lam_init)
        o_ref[r0:r0 + ATT_SUB, :] = o.astype(BF)


def _attn_diff(q_src, kv_srcs, lam_vecs, subln, lam_init):
    bsz, n_q_tok, _ = q_src.shape
    tq = _attn_tile(n_q_tok)
    nh = B_HEADS
    n_seg = len(kv_srcs)
    k_specs = [pl.BlockSpec((None, s.shape[1], LANES), lambda b, h, i: (b, 0, nh + h)) for s in kv_srcs]
    v_specs = [pl.BlockSpec((None, s.shape[1], LANES), lambda b, h, i: (b, 0, 2 * nh + h)) for s in kv_srcs]
    return pl.pallas_call(
        functools.partial(_attn_diff_kernel, n_seg=n_seg, lam_init=lam_init),
        out_shape=jax.ShapeDtypeStruct((bsz, n_q_tok, nh * LANES), BF),
        grid=(bsz, nh, n_q_tok // tq),
        in_specs=[pl.BlockSpec((None, tq, LANES), lambda b, h, i: (b, i, h)),
                  pl.BlockSpec(lam_vecs.shape, lambda b, h, i: (0, 0)),
                  pl.BlockSpec((1, LANES), lambda b, h, i: (0, 0))] + k_specs + v_specs,
        out_specs=pl.BlockSpec((None, tq, LANES), lambda b, h, i: (b, i, h)),
        compiler_params=_cparams(3),
        name="differential_attention",
    )(q_src, lam_vecs, subln.reshape(1, LANES), *kv_srcs, *kv_srcs)


NB_QROWS = 4
NB_BAND = 12


def _nb_band_start(g, rows):
    return int(np.clip(NB_QROWS * g - NB_QROWS, 0, rows - NB_BAND))


def _nb_config(g, rows):
    n_g = rows // NB_QROWS
    return 0 if g == 0 else (2 if g == n_g - 1 else 1)


def _nb_bias_table(rpb, rows):
    nh = rpb.shape[0]
    wr = min(WIN_R, rows)
    cq = np.arange(GRID_W)
    c0 = np.clip(cq - WIN_C // 2, 0, GRID_W - WIN_C)
    col_in = (cq[None, :] >= c0[:, None]) & (cq[None, :] < c0[:, None] + WIN_C)
    dc_idx = np.clip(cq[None, :] - cq[:, None], -(WIN_C - 1), WIN_C - 1) + WIN_C - 1
    n_dr = 2 * WIN_R - 1
    tiles = rpb[:, :, dc_idx]
    tiles = jnp.where(col_in[None, None], tiles * LOG2E, MASK_VALUE)
    tiles = jnp.concatenate([tiles, jnp.full((nh, 1, GRID_W, GRID_W), MASK_VALUE, F32)], axis=1)
    n_g = rows // NB_QROWS
    sel = np.full((3, NB_QROWS, NB_BAND), n_dr, dtype=np.int32)
    for cfg, g in ((0, 0), (1, 1), (2, n_g - 1)):
        start = _nb_band_start(g, rows)
        for a in range(NB_QROWS):
            r = NB_QROWS * g + a
            r0 = int(np.clip(r - wr // 2, 0, rows - wr))
            for jb in range(NB_BAND):
                kr = start + jb
                if r0 <= kr < r0 + wr:
                    sel[cfg, a, jb] = kr - r + (WIN_R - 1)
    big = tiles[:, sel]
    big = big.transpose(0, 1, 2, 4, 3, 5)
    return big.reshape(nh, 3, NB_QROWS * GRID_W, NB_BAND * GRID_W)


def _attn_nb_kernel(q_ref, kl_ref, vl_ref, kc_ref, vc_ref, bias_ref, o_ref, *, rows):
    half = _idiv(_lane_iota(), HEAD_DIM)
    kc = kc_ref[...]
    vc = vc_ref[...]
    tq = NB_QROWS * GRID_W
    tk = NB_BAND * GRID_W
    for g in range(rows // NB_QROWS):
        ks = _nb_band_start(g, rows) * GRID_W
        cfg = _nb_config(g, rows)
        q = q_ref[g * tq:(g + 1) * tq, :]
        kb = kl_ref[ks:ks + tk, :]
        vb = vl_ref[ks:ks + tk, :]
        zero = jnp.zeros_like(q)
        outs = []
        for e in range(2):
            q_e = jnp.where(half == e, q, zero)
            outs.append(_softmax_pv(q_e, [kb, kc], [vb, vc], [bias_ref[e, cfg], None]))
        o_ref[g * tq:(g + 1) * tq, :] = jnp.where(half == 0, outs[0], outs[1]).astype(BF)


def _attn_nb(qkv_lat, qkv_ctx, bias):
    bsz, n_tok, _ = qkv_lat.shape
    n_ctx = qkv_ctx.shape[1]
    rows = n_tok // GRID_W
    npair = D_HEADS * HEAD_DIM // LANES
    return pl.pallas_call(
        functools.partial(_attn_nb_kernel, rows=rows),
        out_shape=jax.ShapeDtypeStruct((bsz, n_tok, npair * LANES), BF),
        grid=(npair, bsz),
        in_specs=[
            pl.BlockSpec((None, n_tok, LANES), lambda p, b: (b, 0, p)),
            pl.BlockSpec((None, n_tok, LANES), lambda p, b: (b, 0, npair + p)),
            pl.BlockSpec((None, n_tok, LANES), lambda p, b: (b, 0, 2 * npair + p)),
            pl.BlockSpec((None, n_ctx, LANES), lambda p, b: (b, 0, npair + p)),
            pl.BlockSpec((None, n_ctx, LANES), lambda p, b: (b, 0, 2 * npair + p)),
            pl.BlockSpec((2,) + bias.shape[1:], lambda p, b: (p, 0, 0, 0)),
        ],
        out_specs=pl.BlockSpec((None, n_tok, LANES), lambda p, b: (b, 0, p)),
        compiler_params=_cparams(2),
        name="neighbourhood_attention",
    )(qkv_lat, qkv_lat, qkv_lat, qkv_ctx, qkv_ctx, bias)


def _attn_ctx_kernel(q_ref, k_ref, v_ref, o_ref):
    half = _idiv(_lane_iota(), HEAD_DIM)
    q = q_ref[...]
    zero = jnp.zeros_like(q)
    outs = [_softmax_pv(jnp.where(half == e, q, zero), [k_ref[...]], [v_ref[...]], [None]) for e in range(2)]
    o_ref[...] = jnp.where(half == 0, outs[0], outs[1]).astype(BF)


def _attn_ctx(qkv_ctx, npair):
    bsz, n_ctx, _ = qkv_ctx.shape
    return pl.pallas_call(
        _attn_ctx_kernel,
        out_shape=jax.ShapeDtypeStruct((bsz, n_ctx, npair * LANES), BF),
        grid=(bsz, npair),
        in_specs=[
            pl.BlockSpec((None, n_ctx, LANES), lambda b, p: (b, 0, p)),
            pl.BlockSpec((None, n_ctx, LANES), lambda b, p: (b, 0, npair + p)),
            pl.BlockSpec((None, n_ctx, LANES), lambda b, p: (b, 0, 2 * npair + p)),
        ],
        out_specs=pl.BlockSpec((None, n_ctx, LANES), lambda b, p: (b, 0, p)),
        compiler_params=_cparams(2),
        name="context_attention",
    )(qkv_ctx, qkv_ctx, qkv_ctx)


def _gla_geometry(rev):
    row = lax.broadcasted_iota(jnp.int32, (C_CHUNK, 1), 0)
    col = lax.broadcasted_iota(jnp.int32, (1, C_CHUNK), 1)
    sub = _idiv(row, C_SUB)
    col_sub = _idiv(col, C_SUB)
    if rev:
        tri, cross, dist = col >= row, col_sub > sub, col - row
    else:
        tri, cross, dist = col <= row, col_sub < sub, row - col
    off_blk = jnp.where(cross, sub, -1)
    dmat = jnp.where(tri, jnp.where(col_sub == sub, dist, -1), -1)
    return tri.astype(BF), sub, off_blk, dmat


def _gla_chunk(q, k, v, g, st, rev, geom):
    tri, sub, off_blk, dmat = geom
    nsub = C_CHUNK // C_SUB
    b = _split_dot_left(tri, g) * LOG2E
    b_tot = b[0:1] if rev else b[C_CHUNK - 1:C_CHUNK]
    o = _dot_nt((q * jnp.exp2(b)).astype(BF), st.astype(BF))
    blocks = range(nsub - 1) if rev else range(1, nsub)
    bounds = {}
    ref_rows = jnp.zeros_like(b)
    for a in blocks:
        r = (a + 1) * C_SUB if rev else a * C_SUB - 1
        bounds[a] = b[r:r + 1]
        ref_rows = jnp.where(sub == a, bounds[a], ref_rows)
    q_t = (q * jnp.exp2(b - ref_rows)).astype(BF)
    a_mat = jnp.zeros((C_CHUNK, C_CHUNK), F32)
    for a in blocks:
        k_t = (k * jnp.exp2(jnp.minimum(bounds[a] - b, 0.0))).astype(BF)
        a_mat = jnp.where(off_blk == a, _dot_nt(q_t, k_t), a_mat)
    for d in range(C_SUB):
        shift = (C_CHUNK - d) % C_CHUNK if rev else d
        k_d = k if d == 0 else pltpu.roll(k, shift, axis=0)
        b_d = b if d == 0 else pltpu.roll(b, shift, axis=0)
        term = q * k_d * jnp.exp2(b - b_d)
        diag = jnp.sum(term, axis=-1, keepdims=True)
        a_mat = jnp.where(dmat == d, diag, a_mat)
    o = o + _dot(a_mat.astype(BF), v)
    k_end = (k * jnp.exp2(b_tot - b)).astype(BF)
    st_new = st * jnp.exp2(b_tot) + _dot_tn(v, k_end)
    return o, st_new


def _gla_kernel(ql_ref, kl_ref, vl_ref, ogl_ref, gfl_ref, gbl_ref,
                qc_ref, kc_ref, vc_ref, ogc_ref, gfc_ref, gbc_ref, ng_ref,
                ol_ref, oc_ref, fl_ref, bl_ref, fc_ref, bc_ref, *, with_ctx_out):
    dk = ql_ref.shape[-1]
    dv = vl_ref.shape[-1]
    n_lat = ql_ref.shape[0] // C_CHUNK
    n_ctx = qc_ref.shape[0] // C_CHUNK
    geom_f = _gla_geometry(False)
    geom_b = _gla_geometry(True)

    def load(q_ref, k_ref, v_ref, g_ref, c):
        sl = pl.ds(pl.multiple_of(c * C_CHUNK, C_CHUNK), C_CHUNK)
        return q_ref[sl, :].astype(F32), k_ref[sl, :].astype(F32), v_ref[sl, :], g_ref[sl, :]

    def chunk_at(c):
        return pl.ds(pl.multiple_of(c * C_CHUNK, C_CHUNK), C_CHUNK)

    def scan_both(refs_f, refs_b, f_ref, b_ref, n_chunks, states):
        def body(j, carry):
            st_f, st_b = carry
            cb = n_chunks - 1 - j
            q, k, v, g = load(*refs_f, j)
            o, st_f = _gla_chunk(q, k, v, g, st_f, False, geom_f)
            f_ref[chunk_at(j), :] = o
            q, k, v, g = load(*refs_b, cb)
            o, st_b = _gla_chunk(q, k, v, g, st_b, True, geom_b)
            b_ref[chunk_at(cb), :] = o
            return st_f, st_b
        return lax.fori_loop(0, n_chunks, body, states, unroll=2)

    def finish(f_ref, b_ref, og_ref, out_ref):
        def body(i, carry):
            sl = pl.ds(pl.multiple_of(i * GLA_FINISH_ROWS, GLA_FINISH_ROWS), GLA_FINISH_ROWS)
            tot = f_ref[sl, :] + b_ref[sl, :]
            ms = jnp.mean(tot * tot, axis=-1, keepdims=True)
            y = tot * lax.rsqrt(ms + NORM_EPS) * ng_ref[...]
            og = og_ref[sl, :].astype(F32)
            out_ref[sl, :] = (y * (og * _sigmoid(og))).astype(BF)
            return carry
        lax.fori_loop(0, f_ref.shape[0] // GLA_FINISH_ROWS, body, 0)

    zero = jnp.zeros((dv, dk), F32)
    lat_f = (ql_ref, kl_ref, vl_ref, gfl_ref)
    lat_b = (ql_ref, kl_ref, vl_ref, gbl_ref)
    ctx_f = (qc_ref, kc_ref, vc_ref, gfc_ref)
    ctx_b = (qc_ref, kc_ref, vc_ref, gbc_ref)
    states = scan_both(ctx_f, ctx_b, fc_ref, bc_ref, n_ctx, (zero, zero))
    scan_both(lat_f, lat_b, fl_ref, bl_ref, n_lat, states)
    finish(fl_ref, bl_ref, ogl_ref, ol_ref)
    if with_ctx_out:
        finish(fc_ref, bc_ref, ogc_ref, oc_ref)
    else:
        oc_ref[...] = jnp.zeros_like(oc_ref)


def _gla(proj_lat, gates_lat, proj_ctx, gates_ctx, norm_g, with_ctx_out):
    bsz, n_lat, _ = proj_lat.shape
    n_ctx = proj_ctx.shape[1]
    nh = C_HEADS
    dk = 512 // nh
    dv = 1024 // nh
    kb, vb = 512 // dk, 1024 // dv

    def specs(n_tok):
        return [
            pl.BlockSpec((None, n_tok, dk), lambda b, h: (b, 0, h)),
            pl.BlockSpec((None, n_tok, dk), lambda b, h: (b, 0, kb + h)),
            pl.BlockSpec((None, n_tok, dv), lambda b, h: (b, 0, (2 * kb * dk) // dv + h)),
            pl.BlockSpec((None, n_tok, dv), lambda b, h: (b, 0, (2 * kb * dk) // dv + vb + h)),
            pl.BlockSpec((None, n_tok, dk), lambda b, h: (b, 0, h)),
            pl.BlockSpec((None, n_tok, dk), lambda b, h: (b, 0, kb + h)),
        ]

    out_l, out_c = pl.pallas_call(
        functools.partial(_gla_kernel, with_ctx_out=with_ctx_out),
        out_shape=(jax.ShapeDtypeStruct((bsz, n_lat, nh * dv), BF),
                   jax.ShapeDtypeStruct((bsz, n_ctx, nh * dv), BF)),
        grid=(bsz, nh),
        in_specs=specs(n_lat) + specs(n_ctx) + [pl.BlockSpec((1, dv), lambda b, h: (0, 0))],
        out_specs=(pl.BlockSpec((None, n_lat, dv), lambda b, h: (b, 0, h)),
                   pl.BlockSpec((None, n_ctx, dv), lambda b, h: (b, 0, h))),
        scratch_shapes=[pltpu.VMEM((n_lat, dv), F32), pltpu.VMEM((n_lat, dv), F32),
                        pltpu.VMEM((n_ctx, dv), F32), pltpu.VMEM((n_ctx, dv), F32)],
        compiler_params=_cparams(2),
        name="gated_linear_attention",
    )(proj_lat, proj_lat, proj_lat, proj_lat, gates_lat, gates_lat,
      proj_ctx, proj_ctx, proj_ctx, proj_ctx, gates_ctx, gates_ctx, norm_g.reshape(1, dv))
    return out_l, out_c


def kernel(x, c, ctx, c_ctx, mod_w, mod_b, ln1_g, ln1_b, ffn_w_up, ffn_conv_w, ffn_conv_b, ffn_w_down, ln2_g, ln2_b, a_w_qkv, a_q_norm, a_k_norm, a_w_o, b_w_qkv, b_lambda_q1, b_lambda_k1, b_lambda_q2, b_lambda_k2, b_subln, b_w_o, c_w_in, c_w_gate_fwd, c_b_gate_fwd, c_w_gate_bwd, c_b_gate_bwd, c_norm, c_w_o, d_w_qkv, d_rpb, d_w_o):
    bsz, n_lat, dm = x.shape
    n_ctx = ctx.shape[1]
    scale = HEAD_DIM ** -0.5 * LOG2E

    pad_rows = (-(bsz + 1)) % 8
    cond = jnp.concatenate([c, c_ctx[None, :], jnp.zeros((pad_rows, dm), F32)], axis=0)
    mod_all = _modulation(cond, mod_w, mod_b)

    x_lat, x_ctx = x, ctx
    for i in range(DEPTH):
        kind, j = i % N_MIXERS, i // N_MIXERS
        last = i == DEPTH - 1
        m_lat = mod_all[i, :bsz].reshape(bsz, N_MOD, dm)
        m_ctx = jnp.broadcast_to(mod_all[i, bsz].reshape(1, N_MOD, dm), (bsz, N_MOD, dm))

        if kind == 0:
            w = a_w_qkv[j].astype(BF)
            gains = (a_q_norm[j], a_k_norm[j])
            nq, nk = A_HEADS * HEAD_DIM // LANES, A_KV_HEADS * HEAD_DIM // LANES
            p_lat = _proj_qkv(x_lat, m_lat, w, n_q=nq, n_k=nk, n_v=nk, q_scale=scale, norm_gains=gains, rope=True)
            p_ctx = _proj_qkv(x_ctx, m_ctx, w, n_q=nq, n_k=nk, n_v=nk, q_scale=scale, norm_gains=gains)
            a_lat = _attn_gqa(p_lat, [p_ctx, p_lat])
            a_ctx = None if last else _attn_gqa(p_ctx, [p_ctx])
            w_o = a_w_o[j]
        elif kind == 1:
            w = b_w_qkv[j].astype(BF)
            nb = 2 * B_HEADS * HEAD_DIM // LANES
            p_lat = _proj_qkv(x_lat, m_lat, w, n_q=nb, n_k=nb, n_v=nb, q_scale=scale, rope=True)
            p_ctx = _proj_qkv(x_ctx, m_ctx, w, n_q=nb, n_k=nb, n_v=nb, q_scale=scale)
            lam_vecs = jnp.stack([b_lambda_q1[j], b_lambda_k1[j], b_lambda_q2[j], b_lambda_k2[j]])
            lam_init = 0.8 - 0.6 * math.exp(-0.3 * i)
            a_lat = _attn_diff(p_lat, [p_ctx, p_lat], lam_vecs, b_subln[j], lam_init)
            a_ctx = None if last else _attn_diff(p_ctx, [p_ctx], lam_vecs, b_subln[j], lam_init)
            w_o = b_w_o[j]
        elif kind == 2:
            w_in = c_w_in[j]
            n_main = w_in.shape[1] - 2 * C_GATE_RANK
            w_main = w_in[:, :n_main].astype(BF)
            w_z = jnp.pad(w_in[:, n_main:], ((0, 0), (0, LANES - 2 * C_GATE_RANK))).astype(BF)
            kd = c_w_gate_fwd.shape[-1]
            w_gate = jnp.zeros((LANES, 2 * kd), F32)
            w_gate = w_gate.at[:C_GATE_RANK, :kd].set(c_w_gate_fwd[j])
            w_gate = w_gate.at[C_GATE_RANK:2 * C_GATE_RANK, kd:].set(c_w_gate_bwd[j]).astype(BF)
            b_gate = jnp.concatenate([c_b_gate_fwd[j], c_b_gate_bwd[j]]).reshape(1, 2 * kd)
            q_scale = (kd // C_HEADS) ** -0.5
            nq = kd // LANES
            p_lat, g_lat = _proj_gla(x_lat, m_lat, w_main, w_z, w_gate, b_gate, q_scale=q_scale, n_q=nq)
            p_ctx, g_ctx = _proj_gla(x_ctx, m_ctx, w_main, w_z, w_gate, b_gate, q_scale=q_scale, n_q=nq)
            a_lat, a_ctx = _gla(p_lat, g_lat, p_ctx, g_ctx, c_norm[j], not last)
            w_o = c_w_o[j]
        else:
            w = d_w_qkv[j].astype(BF)
            nb = D_HEADS * HEAD_DIM // LANES
            p_lat = _proj_qkv(x_lat, m_lat, w, n_q=nb, n_k=nb, n_v=nb, q_scale=scale)
            p_ctx = _proj_qkv(x_ctx, m_ctx, w, n_q=nb, n_k=nb, n_v=nb, q_scale=scale)
            bias = _nb_bias_table(d_rpb[j], n_lat // GRID_W)
            a_lat = _attn_nb(p_lat, p_ctx, bias)
            a_ctx = None if last else _attn_ctx(p_ctx, nb)
            w_o = d_w_o[j]

        w_o = w_o.astype(BF)
        w_up = ffn_w_up[i].astype(BF)
        w_dn = ffn_w_down[i].astype(BF)
        ffn = (w_o, ln1_g[i], ln1_b[i], w_up, ffn_conv_w[i], ffn_conv_b[i], w_dn, ln2_g[i], ln2_b[i])
        x_lat = _mix_ffn(a_lat, x_lat, m_lat, *ffn)
        if not last:
            x_ctx = _mix_ffn(a_ctx, x_ctx, m_ctx, *ffn)
    return x_lat
```

```python
import functools
import math

import jax
import jax.numpy as jnp
import numpy as np
from jax import lax
from jax.experimental import pallas as pl
from jax.experimental.pallas import tpu as pltpu

F32 = jnp.float32
BF = jnp.bfloat16

DEPTH = 4
N_MIXERS = 4
N_MOD = 6
GRID_W = 64
ROPE_THETA = 10000.0
NORM_EPS = 1e-6
LN_EPS = 1e-5
HEAD_DIM = 64
A_HEADS, A_KV_HEADS = 16, 4
B_HEADS = 8
C_HEADS = 4
C_GATE_RANK = 16
C_GATE_NORMALIZER = 16.0
C_CHUNK = 64
C_SUB = 16
GLA_FINISH_ROWS = 256
D_HEADS = 16
WIN_R, WIN_C = 8, 16
FF_DIM = 2816
DEEPNORM_ALPHA = (2 * DEPTH) ** 0.25
MASK_VALUE = -1e30
LOG2E = math.log2(math.e)

LANES = 128
MXU_WIDTH = 256
VMEM_LIMIT_BYTES = 56 * 1024 * 1024


def _cparams(n_axes):
    return pltpu.CompilerParams(
        dimension_semantics=("arbitrary",) * n_axes,
        vmem_limit_bytes=VMEM_LIMIT_BYTES)


def _dot(a, b):
    return jnp.dot(a, b, preferred_element_type=F32)


def _dot_nt(a, b):
    return lax.dot_general(a, b, (((1,), (1,)), ((), ())), preferred_element_type=F32)


def _dot_tn(a, b):
    return lax.dot_general(a, b, (((0,), (0,)), ((), ())), preferred_element_type=F32)


def _split_dot(x, m_bf16):
    hi = x.astype(BF)
    lo = (x - hi.astype(F32)).astype(BF)
    return _dot(hi, m_bf16) + _dot(lo, m_bf16)


def _split_dot_left(m_bf16, x):
    hi = x.astype(BF)
    lo = (x - hi.astype(F32)).astype(BF)
    return _dot(m_bf16, hi) + _dot(m_bf16, lo)


def _sigmoid(x):
    return 1.0 / (1.0 + jnp.exp(-x))


def _modulate(x_ref, mod_ref, shift_row, scale_row):
    x = x_ref[...]
    return x * (1.0 + mod_ref[scale_row:scale_row + 1, :]) + mod_ref[shift_row:shift_row + 1, :]


def _layer_norm(z, g, b):
    mu = jnp.mean(z, axis=-1, keepdims=True)
    zc = z - mu
    var = jnp.mean(zc * zc, axis=-1, keepdims=True)
    return zc * lax.rsqrt(var + LN_EPS) * g + b


def _rope(y, cos_t, sin_t, lo_half):
    swap = jnp.where(lo_half, pltpu.roll(y, LANES - 32, axis=1), pltpu.roll(y, 32, axis=1))
    return y * cos_t + swap * sin_t


def _idiv(x, n):
    return lax.shift_right_logical(x, int(n).bit_length() - 1)


def _imod(x, n):
    return x & (n - 1)


def _lane_iota(shape=(1, LANES)):
    return lax.broadcasted_iota(jnp.int32, shape, len(shape) - 1)


def _mod_kernel(cond_ref, w_ref, b_ref, o_ref):
    cnd = cond_ref[...]
    act = cnd * _sigmoid(cnd)
    o_ref[...] = _dot(act.astype(BF), w_ref[...].astype(BF)) + b_ref[...]


def _modulation(cond, mod_w, mod_b):
    depth, dm, n = mod_w.shape
    rows = cond.shape[0]
    tn = 1536
    return pl.pallas_call(
        _mod_kernel,
        out_shape=jax.ShapeDtypeStruct((depth, rows, n), F32),
        grid=(depth, n // tn),
        in_specs=[
            pl.BlockSpec((rows, dm), lambda i, j: (0, 0)),
            pl.BlockSpec((None, dm, tn), lambda i, j: (i, 0, j)),
            pl.BlockSpec((None, 1, tn), lambda i, j: (i, 0, j)),
        ],
        out_specs=pl.BlockSpec((None, rows, tn), lambda i, j: (i, 0, j)),
        compiler_params=_cparams(2),
        name="adaln_modulation",
    )(cond, mod_w, mod_b.reshape(depth, 1, n))


def _row_tile(n_tok):
    return 512 if n_tok % 512 == 0 else n_tok


def _proj_call(kernel, x, mod, w, extras, extra_specs, out_shapes, out_specs, name):
    bsz, n_tok, dm = x.shape
    tm = _row_tile(n_tok)
    in_specs = [
        pl.BlockSpec((None, tm, dm), lambda b, i: (b, i, 0)),
        pl.BlockSpec((None, N_MOD, dm), lambda b, i: (b, 0, 0)),
        pl.BlockSpec(w.shape, lambda b, i: (0, 0)),
    ] + list(extra_specs(tm))
    return pl.pallas_call(
        kernel,
        out_shape=out_shapes,
        grid=(bsz, n_tok // tm),
        in_specs=in_specs,
        out_specs=out_specs(tm),
        compiler_params=_cparams(2),
        name=name,
    )(x, mod, w, *extras)


def _full_spec(arr):
    nd = arr.ndim
    return pl.BlockSpec(arr.shape, lambda b, i: (0,) * nd)


def _seg_matrix():
    idx = np.arange(LANES) // HEAD_DIM
    return jnp.asarray((idx[:, None] == idx[None, :]).astype(np.float32) / HEAD_DIM, dtype=BF)


def _rope_tables(n_tok):
    t = jnp.arange(n_tok)
    row = (t // GRID_W).astype(F32)
    col = (t % GRID_W).astype(F32)
    n_freq = HEAD_DIM // 4
    inv = 1.0 / (ROPE_THETA ** (jnp.arange(n_freq, dtype=F32) / n_freq))
    ang = jnp.concatenate([row[:, None] * inv, col[:, None] * inv], axis=-1)
    cos, sin = jnp.cos(ang), jnp.sin(ang)
    cos_t = jnp.concatenate([cos, cos, cos, cos], axis=-1)
    sin_t = jnp.concatenate([-sin, sin, -sin, sin], axis=-1)
    return cos_t, sin_t


def _proj_qkv_kernel(x_ref, mod_ref, w_ref, *rest, n_q, n_k, n_v, norm, rope, q_scale):
    rest = list(rest)
    seg_ref = gq_ref = gk_ref = cos_ref = sin_ref = None
    if norm:
        seg_ref, gq_ref, gk_ref = rest[:3]
        rest = rest[3:]
    if rope:
        cos_ref, sin_ref = rest[:2]
        rest = rest[2:]
    (o_ref,) = rest
    hb = _modulate(x_ref, mod_ref, 0, 1).astype(BF)
    lo_half = _imod(_lane_iota(), HEAD_DIM) < (HEAD_DIM // 2)
    n_blocks = n_q + n_k + n_v
    per_dot = MXU_WIDTH // LANES
    for c0 in range(0, n_blocks, per_dot):
        nb = min(per_dot, n_blocks - c0)
        y2 = _dot(hb, w_ref[:, c0 * LANES:(c0 + nb) * LANES])
        for j in range(nb):
            c = c0 + j
            y = y2[:, j * LANES:(j + 1) * LANES]
            if c < n_q + n_k:
                if norm:
                    ms = _split_dot(y * y, seg_ref[...])
                    gain = gq_ref[...] if c < n_q else gk_ref[...]
                    y = y * lax.rsqrt(ms + NORM_EPS) * gain
                if rope:
                    y = _rope(y, cos_ref[...], sin_ref[...], lo_half)
                if c < n_q and q_scale != 1.0:
                    y = y * q_scale
            o_ref[:, c * LANES:(c + 1) * LANES] = y.astype(BF)


def _proj_qkv(x, mod, w, *, n_q, n_k, n_v, q_scale, norm_gains=None, rope=False):
    bsz, n_tok, _ = x.shape
    n_out = w.shape[1]
    extras, spec_fns = [], []
    if norm_gains is not None:
        gq, gk = norm_gains
        extras += [_seg_matrix(), jnp.tile(gq, 2).reshape(1, LANES), jnp.tile(gk, 2).reshape(1, LANES)]
        spec_fns += [lambda tm, a=a: _full_spec(a) for a in extras]
    if rope:
        cos_t, sin_t = _rope_tables(n_tok)
        extras += [cos_t, sin_t]
        spec_fns += [lambda tm: pl.BlockSpec((tm, LANES), lambda b, i: (i, 0))] * 2
    kern = functools.partial(_proj_qkv_kernel, n_q=n_q, n_k=n_k, n_v=n_v,
                             norm=norm_gains is not None, rope=rope, q_scale=q_scale)
    return _proj_call(
        kern, x, mod, w, extras, lambda tm: [f(tm) for f in spec_fns],
        jax.ShapeDtypeStruct((bsz, n_tok, n_out), BF),
        lambda tm: pl.BlockSpec((None, tm, n_out), lambda b, i: (b, i, 0)),
        "mixer_qkv_projection")


def _log_sigmoid(x):
    return jnp.minimum(x, 0.0) - jnp.log(1.0 + jnp.exp(-jnp.abs(x)))


def _proj_gla_kernel(x_ref, mod_ref, w_ref, wz_ref, wg_ref, bg_ref, o_ref, g_ref, *, q_scale, n_q):
    hb = _modulate(x_ref, mod_ref, 0, 1).astype(BF)
    n_blocks = o_ref.shape[-1] // LANES
    per_dot = MXU_WIDTH // LANES
    for c0 in range(0, n_blocks, per_dot):
        y2 = _dot(hb, w_ref[:, c0 * LANES:(c0 + per_dot) * LANES])
        if c0 < n_q:
            y2 = y2 * q_scale
        o_ref[:, c0 * LANES:(c0 + per_dot) * LANES] = y2.astype(BF)
    z = _dot(hb, wz_ref[...]).astype(BF)
    n_gate = g_ref.shape[-1]
    for c0 in range(0, n_gate, MXU_WIDTH):
        pre = _dot(z, wg_ref[:, c0:c0 + MXU_WIDTH]) + bg_ref[:, c0:c0 + MXU_WIDTH]
        g_ref[:, c0:c0 + MXU_WIDTH] = _log_sigmoid(pre) * (1.0 / C_GATE_NORMALIZER)


def _proj_gla(x, mod, w_main, w_z, w_gate, b_gate, *, q_scale, n_q):
    bsz, n_tok, _ = x.shape
    n_out, n_gate = w_main.shape[1], w_gate.shape[1]
    extras = [w_z, w_gate, b_gate]
    kern = functools.partial(_proj_gla_kernel, q_scale=q_scale, n_q=n_q)
    return _proj_call(
        kern, x, mod, w_main, extras, lambda tm: [_full_spec(a) for a in extras],
        (jax.ShapeDtypeStruct((bsz, n_tok, n_out), BF), jax.ShapeDtypeStruct((bsz, n_tok, n_gate), F32)),
        lambda tm: (pl.BlockSpec((None, tm, n_out), lambda b, i: (b, i, 0)),
                    pl.BlockSpec((None, tm, n_gate), lambda b, i: (b, i, 0))),
        "gla_projection")


FFN_CHUNK = 256
FFN_TILE = 512
FFN_SUB = 512
HALO = 16


def _mix_ffn_kernel(a_ref, ap_ref, an_ref, x_ref, xp_ref, xn_ref, mod_ref, wo_ref, g1_ref, b1_ref,
                    wu_ref, cw_ref, cb_ref, wd_ref, g2_ref, b2_ref, o_ref, act_ref):
    tm = x_ref.shape[0]
    i = pl.program_id(1)
    n_i = pl.num_programs(1)
    ts = min(tm, FFN_SUB)
    n_sub = tm // ts
    n_chunks = FF_DIM // FFN_CHUNK
    rows = lax.broadcasted_iota(jnp.int32, (ts, 1), 0)
    first_row = rows == 0
    last_row = rows == ts - 1

    def halo(s):
        r0 = s * ts
        if s == 0:
            before = (ap_ref[...], xp_ref[...], (i > 0).astype(F32))
        else:
            before = (a_ref[r0 - HALO:r0, :], x_ref[r0 - HALO:r0, :], 1.0)
        if s == n_sub - 1:
            after = (an_ref[...], xn_ref[...], (i < n_i - 1).astype(F32))
        else:
            after = (a_ref[r0 + ts:r0 + ts + HALO, :], x_ref[r0 + ts:r0 + ts + HALO, :], 1.0)
        return before, after

    def out_proj(s):
        (a_prev, _, _), (a_next, _, _) = halo(s)
        a_ext = jnp.concatenate([a_ref[s * ts:(s + 1) * ts, :], a_prev, a_next], axis=0)
        return _dot(a_ext, wo_ref[...])

    def norm1(s, y1):
        (_, x_prev, _), (_, x_next, _) = halo(s)
        x_ext = jnp.concatenate([x_ref[s * ts:(s + 1) * ts, :], x_prev, x_next], axis=0)
        x_mid = _layer_norm(DEEPNORM_ALPHA * x_ext + mod_ref[2:3, :] * y1, g1_ref[...], b1_ref[...])
        return x_mid[:ts], (x_mid * (1.0 + mod_ref[4:5, :]) + mod_ref[3:4, :]).astype(BF)

    def up(hb, c):
        cg = c * FFN_CHUNK
        return _dot(hb, wu_ref[:, cg:cg + FFN_CHUNK]), _dot(hb, wu_ref[:, FF_DIM + cg:FF_DIM + cg + FFN_CHUNK])

    def conv_act(s, u_pair, c):
        (_, _, has_prev), (_, _, has_next) = halo(s)

        def conv(u_ext, c0):
            u = u_ext[:ts]
            prev = u_ext[ts + HALO - 1:ts + HALO] * has_prev
            nxt = u_ext[ts + HALO:ts + HALO + 1] * has_next
            u_dn = jnp.where(first_row, prev, pltpu.roll(u, 1, axis=0))
            u_up = jnp.where(last_row, nxt, pltpu.roll(u, ts - 1, axis=0))
            w0 = cw_ref[0:1, c0:c0 + FFN_CHUNK]
            w1 = cw_ref[1:2, c0:c0 + FFN_CHUNK]
            w2 = cw_ref[2:3, c0:c0 + FFN_CHUNK]
            return cb_ref[:, c0:c0 + FFN_CHUNK] + u_dn * w0 + u * w1 + u_up * w2

        cg = c * FFN_CHUNK
        gate = conv(u_pair[0], cg)
        val = conv(u_pair[1], FF_DIM + cg)
        act_ref[s * ts:(s + 1) * ts, cg:cg + FFN_CHUNK] = (gate * _sigmoid(gate) * val).astype(BF)

    def norm2(s, x_mid, y2):
        z2 = DEEPNORM_ALPHA * x_mid + mod_ref[5:6, :] * y2
        o_ref[s * ts:(s + 1) * ts, :] = _layer_norm(z2, g2_ref[...], b2_ref[...])

    y1 = [out_proj(s) for s in range(n_sub)]
    deferred = None
    for s in range(n_sub):
        x_mid, hb = norm1(s, y1[s])
        u_pair = up(hb, 0)
        for c in range(n_chunks):
            following = up(hb, c + 1) if c + 1 < n_chunks else None
            conv_act(s, u_pair, c)
            u_pair = following
        y2 = _dot(act_ref[s * ts:(s + 1) * ts, :], wd_ref[...])
        if deferred is not None:
            norm2(*deferred)
        deferred = (s, x_mid, y2)
    norm2(*deferred)


def _mix_ffn(a, x, mod, w_o, ln1_g, ln1_b, w_up, conv_w, conv_b, w_down, ln2_g, ln2_b):
    bsz, n_tok, dm = x.shape
    k_in = a.shape[-1]
    tm = FFN_TILE if n_tok % FFN_TILE == 0 else n_tok
    n_halo = n_tok // HALO
    per = tm // HALO
    tile = lambda b, i: (b, i, 0)
    prev = lambda b, i: (b, jnp.maximum(i * per - 1, 0), 0)
    nxt = lambda b, i: (b, jnp.minimum((i + 1) * per, n_halo - 1), 0)
    const2 = lambda b, i: (0, 0)
    resident = functools.partial(pl.BlockSpec, index_map=const2, pipeline_mode=pl.Buffered(1))
    row = lambda v: v.reshape(1, -1)
    return pl.pallas_call(
        _mix_ffn_kernel,
        out_shape=jax.ShapeDtypeStruct((bsz, n_tok, dm), F32),
        grid=(bsz, n_tok // tm),
        in_specs=[
            pl.BlockSpec((None, tm, k_in), tile),
            pl.BlockSpec((None, HALO, k_in), prev),
            pl.BlockSpec((None, HALO, k_in), nxt),
            pl.BlockSpec((None, tm, dm), tile),
            pl.BlockSpec((None, HALO, dm), prev),
            pl.BlockSpec((None, HALO, dm), nxt),
            pl.BlockSpec((None, N_MOD, dm), lambda b, i: (b, 0, 0)),
            resident(w_o.shape),
            pl.BlockSpec((1, dm), const2),
            pl.BlockSpec((1, dm), const2),
            resident(w_up.shape),
            pl.BlockSpec(conv_w.shape, const2),
            pl.BlockSpec((1, 2 * FF_DIM), const2),
            resident(w_down.shape),
            pl.BlockSpec((1, dm), const2),
            pl.BlockSpec((1, dm), const2),
        ],
        out_specs=pl.BlockSpec((None, tm, dm), tile),
        scratch_shapes=[pltpu.VMEM((tm, FF_DIM), BF)],
        compiler_params=_cparams(2),
        name="outproj_convffn_layernorm",
    )(a, a, a, x, x, x, mod, w_o, row(ln1_g), row(ln1_b), w_up, conv_w, row(conv_b), w_down,
      row(ln2_g), row(ln2_b))


ATT_SUB = 512
ATT_TILE = 2048


def _attn_tile(n_tok):
    return ATT_TILE if n_tok % ATT_TILE == 0 else n_tok


def _transpose_values(v_ref):
    return v_ref[...].astype(F32).T.astype(BF)


def _scores_t(q, keys, biases_t):
    return [_dot_nt(k, q) if bias is None else _dot_nt(k, q) + bias for k, bias in zip(keys, biases_t)]


def _probs_t(scores):
    m = functools.reduce(jnp.maximum, [jnp.max(s, axis=0, keepdims=True) for s in scores])
    probs = [jnp.exp2(s - m) for s in scores]
    denom = functools.reduce(lambda a, b: a + b, [jnp.sum(p, axis=0, keepdims=True) for p in probs])
    return [p.astype(BF) for p in probs], denom


def _pv_t(probs, denom, values_t):
    out_t = functools.reduce(lambda a, b: a + b, [_dot(vt, p) for p, vt in zip(probs, values_t)])
    return (out_t / denom).T


def _softmax_pv_t(scores, values_t):
    return _pv_t(*_probs_t(scores), values_t)


def _attend_pipelined(items, emit):
    n = len(items)
    scores = {j: _scores_t(*items[j][:3]) for j in range(min(2, n))}
    probs = {0: _probs_t(scores.pop(0))}
    for i in range(n):
        if i + 2 < n:
            scores[i + 2] = _scores_t(*items[i + 2][:3])
        if i + 1 < n:
            probs[i + 1] = _probs_t(scores.pop(i + 1))
        emit(i, _pv_t(*probs.pop(i), items[i][3]))


def _attn_gqa_kernel(q_ref, *rest, n_seg):
    k_refs, v_refs, o_ref = rest[:n_seg], rest[n_seg:2 * n_seg], rest[2 * n_seg]
    grp = A_HEADS // A_KV_HEADS
    kv_pos = (pl.program_id(1) * 2 // grp) % 2
    half = _idiv(_lane_iota(), HEAD_DIM)
    keys = [r[...] for r in k_refs]
    values = [_transpose_values(r) for r in v_refs]
    items = []
    for r0 in range(0, q_ref.shape[0], ATT_SUB):
        q = q_ref[r0:r0 + ATT_SUB, :].astype(F32)
        q_sw = pltpu.roll(q, HEAD_DIM, axis=1)
        for e in range(2):
            q_e = jnp.where(kv_pos == e, q, q_sw)
            q_e = jnp.where(half == kv_pos, q_e, 0.0).astype(BF)
            items.append((q_e, keys, [None] * n_seg, values))
    outs = {}

    def emit(i, o):
        r0, e = (i // 2) * ATT_SUB, i % 2
        outs[e] = jnp.where(kv_pos == e, o, pltpu.roll(o, HEAD_DIM, axis=1))
        if e == 1:
            o_ref[r0:r0 + ATT_SUB, :] = jnp.where(half == 0, outs[0], outs[1]).astype(BF)

    _attend_pipelined(items, emit)


def _attn_gqa(q_src, kv_srcs):
    bsz, n_q_tok, _ = q_src.shape
    tq = _attn_tile(n_q_tok)
    n_qb = A_HEADS * HEAD_DIM // LANES
    n_kb = A_KV_HEADS * HEAD_DIM // LANES
    per_kb = n_qb // n_kb
    n_seg = len(kv_srcs)
    k_specs = [pl.BlockSpec((None, s.shape[1], LANES), lambda b, p, i: (b, 0, n_qb + p // per_kb))
               for s in kv_srcs]
    v_specs = [pl.BlockSpec((None, s.shape[1], LANES), lambda b, p, i: (b, 0, n_qb + n_kb + p // per_kb))
               for s in kv_srcs]
    return pl.pallas_call(
        functools.partial(_attn_gqa_kernel, n_seg=n_seg),
        out_shape=jax.ShapeDtypeStruct((bsz, n_q_tok, n_qb * LANES), BF),
        grid=(bsz, n_qb, n_q_tok // tq),
        in_specs=[pl.BlockSpec((None, tq, LANES), lambda b, p, i: (b, i, p))] + k_specs + v_specs,
        out_specs=pl.BlockSpec((None, tq, LANES), lambda b, p, i: (b, i, p)),
        compiler_params=_cparams(3),
        name="gqa_attention",
    )(q_src, *kv_srcs, *kv_srcs)


def _attn_diff_kernel(q_ref, lam_ref, subln_ref, *rest, n_seg, lam_init):
    k_refs, v_refs, o_ref = rest[:n_seg], rest[n_seg:2 * n_seg], rest[2 * n_seg]
    lam_v = lam_ref[...]
    lam = (jnp.exp(jnp.sum(lam_v[0:1] * lam_v[1:2], axis=-1, keepdims=True))
           - jnp.exp(jnp.sum(lam_v[2:3] * lam_v[3:4], axis=-1, keepdims=True)) + lam_init)
    half = _idiv(_lane_iota(), HEAD_DIM)
    keys = [r[...] for r in k_refs]
    values = [_transpose_values(r) for r in v_refs]
    items = []
    for r0 in range(0, q_ref.shape[0], ATT_SUB):
        q = q_ref[r0:r0 + ATT_SUB, :]
        for e in range(2):
            items.append((jnp.where(half == e, q, jnp.zeros_like(q)), keys, [None] * n_seg, values))
    outs = {}

    def emit(i, o):
        r0, e = (i // 2) * ATT_SUB, i % 2
        outs[e] = o
        if e == 1:
            o = outs[0] - lam * outs[1]
            ms = jnp.mean(o * o, axis=-1, keepdims=True)
            o = o * lax.rsqrt(ms + NORM_EPS) * subln_ref[...] * (1.0 - lam_init)
            o_ref[r0:r0 + ATT_SUB, :] = o.astype(BF)

    _attend_pipelined(items, emit)


def _attn_diff(q_src, kv_srcs, lam_vecs, subln, lam_init):
    bsz, n_q_tok, _ = q_src.shape
    tq = _attn_tile(n_q_tok)
    nh = B_HEADS
    n_seg = len(kv_srcs)
    k_specs = [pl.BlockSpec((None, s.shape[1], LANES), lambda b, h, i: (b, 0, nh + h)) for s in kv_srcs]
    v_specs = [pl.BlockSpec((None, s.shape[1], LANES), lambda b, h, i: (b, 0, 2 * nh + h)) for s in kv_srcs]
    return pl.pallas_call(
        functools.partial(_attn_diff_kernel, n_seg=n_seg, lam_init=lam_init),
        out_shape=jax.ShapeDtypeStruct((bsz, n_q_tok, nh * LANES), BF),
        grid=(bsz, nh, n_q_tok // tq),
        in_specs=[pl.BlockSpec((None, tq, LANES), lambda b, h, i: (b, i, h)),
                  pl.BlockSpec(lam_vecs.shape, lambda b, h, i: (0, 0)),
                  pl.BlockSpec((1, LANES), lambda b, h, i: (0, 0))] + k_specs + v_specs,
        out_specs=pl.BlockSpec((None, tq, LANES), lambda b, h, i: (b, i, h)),
        compiler_params=_cparams(3),
        name="differential_attention",
    )(q_src, lam_vecs, subln.reshape(1, LANES), *kv_srcs, *kv_srcs)


NB_QROWS = 4
NB_BAND = 12


def _nb_band_start(g, rows):
    return int(np.clip(NB_QROWS * g - NB_QROWS, 0, rows - NB_BAND))


def _nb_config(g, rows):
    n_g = rows // NB_QROWS
    return 0 if g == 0 else (2 if g == n_g - 1 else 1)


def _nb_bias_table(rpb, rows):
    nh = rpb.shape[0]
    wr = min(WIN_R, rows)
    cq = np.arange(GRID_W)
    c0 = np.clip(cq - WIN_C // 2, 0, GRID_W - WIN_C)
    col_in = (cq[None, :] >= c0[:, None]) & (cq[None, :] < c0[:, None] + WIN_C)
    dc_idx = np.clip(cq[None, :] - cq[:, None], -(WIN_C - 1), WIN_C - 1) + WIN_C - 1
    n_dr = 2 * WIN_R - 1
    tiles = rpb[:, :, dc_idx]
    tiles = jnp.where(col_in[None, None], tiles * LOG2E, MASK_VALUE)
    tiles = jnp.concatenate([tiles, jnp.full((nh, 1, GRID_W, GRID_W), MASK_VALUE, F32)], axis=1)
    n_g = rows // NB_QROWS
    sel = np.full((3, NB_QROWS, NB_BAND), n_dr, dtype=np.int32)
    for cfg, g in ((0, 0), (1, 1), (2, n_g - 1)):
        start = _nb_band_start(g, rows)
        for a in range(NB_QROWS):
            r = NB_QROWS * g + a
            r0 = int(np.clip(r - wr // 2, 0, rows - wr))
            for jb in range(NB_BAND):
                kr = start + jb
                if r0 <= kr < r0 + wr:
                    sel[cfg, a, jb] = kr - r + (WIN_R - 1)
    big = tiles[:, sel]
    big = big.transpose(0, 1, 3, 5, 2, 4)
    return big.reshape(nh, 3, NB_BAND * GRID_W, NB_QROWS * GRID_W)


def _attn_nb_kernel(q_ref, kl_ref, vl_ref, kc_ref, vc_ref, bias_ref, o_ref, *, rows):
    half = _idiv(_lane_iota(), HEAD_DIM)
    kc = kc_ref[...]
    vc = _transpose_values(vc_ref)
    vl = _transpose_values(vl_ref)
    tq = NB_QROWS * GRID_W
    tk = NB_BAND * GRID_W
    items = []
    for g in range(rows // NB_QROWS):
        ks = _nb_band_start(g, rows) * GRID_W
        cfg = _nb_config(g, rows)
        q = q_ref[g * tq:(g + 1) * tq, :]
        kb = kl_ref[ks:ks + tk, :]
        vb = vl[:, ks:ks + tk]
        for e in range(2):
            q_e = jnp.where(half == e, q, jnp.zeros_like(q))
            items.append((q_e, [kb, kc], [bias_ref[e, cfg], None], [vb, vc]))
    outs = {}

    def emit(i, o):
        g, e = i // 2, i % 2
        outs[e] = o
        if e == 1:
            o_ref[g * tq:(g + 1) * tq, :] = jnp.where(half == 0, outs[0], outs[1]).astype(BF)

    _attend_pipelined(items, emit)


def _attn_nb(qkv_lat, qkv_ctx, bias):
    bsz, n_tok, _ = qkv_lat.shape
    n_ctx = qkv_ctx.shape[1]
    rows = n_tok // GRID_W
    npair = D_HEADS * HEAD_DIM // LANES
    return pl.pallas_call(
        functools.partial(_attn_nb_kernel, rows=rows),
        out_shape=jax.ShapeDtypeStruct((bsz, n_tok, npair * LANES), BF),
        grid=(npair, bsz),
        in_specs=[
            pl.BlockSpec((None, n_tok, LANES), lambda p, b: (b, 0, p)),
            pl.BlockSpec((None, n_tok, LANES), lambda p, b: (b, 0, npair + p)),
            pl.BlockSpec((None, n_tok, LANES), lambda p, b: (b, 0, 2 * npair + p)),
            pl.BlockSpec((None, n_ctx, LANES), lambda p, b: (b, 0, npair + p)),
            pl.BlockSpec((None, n_ctx, LANES), lambda p, b: (b, 0, 2 * npair + p)),
            pl.BlockSpec((2,) + bias.shape[1:], lambda p, b: (p, 0, 0, 0)),
        ],
        out_specs=pl.BlockSpec((None, n_tok, LANES), lambda p, b: (b, 0, p)),
        compiler_params=_cparams(2),
        name="neighbourhood_attention",
    )(qkv_lat, qkv_lat, qkv_lat, qkv_ctx, qkv_ctx, bias)


def _attn_ctx_kernel(q_ref, k_ref, v_ref, o_ref):
    half = _idiv(_lane_iota(), HEAD_DIM)
    q = q_ref[...]
    zero = jnp.zeros_like(q)
    keys, values = [k_ref[...]], [_transpose_values(v_ref)]
    outs = [_softmax_pv_t(_scores_t(jnp.where(half == e, q, zero), keys, [None]), values) for e in range(2)]
    o_ref[...] = jnp.where(half == 0, outs[0], outs[1]).astype(BF)


def _attn_ctx(qkv_ctx, npair):
    bsz, n_ctx, _ = qkv_ctx.shape
    return pl.pallas_call(
        _attn_ctx_kernel,
        out_shape=jax.ShapeDtypeStruct((bsz, n_ctx, npair * LANES), BF),
        grid=(bsz, npair),
        in_specs=[
            pl.BlockSpec((None, n_ctx, LANES), lambda b, p: (b, 0, p)),
            pl.BlockSpec((None, n_ctx, LANES), lambda b, p: (b, 0, npair + p)),
            pl.BlockSpec((None, n_ctx, LANES), lambda b, p: (b, 0, 2 * npair + p)),
        ],
        out_specs=pl.BlockSpec((None, n_ctx, LANES), lambda b, p: (b, 0, p)),
        compiler_params=_cparams(2),
        name="context_attention",
    )(qkv_ctx, qkv_ctx, qkv_ctx)


def _gla_geometry(rev):
    row = lax.broadcasted_iota(jnp.int32, (C_CHUNK, 1), 0)
    col = lax.broadcasted_iota(jnp.int32, (1, C_CHUNK), 1)
    sub = _idiv(row, C_SUB)
    col_sub = _idiv(col, C_SUB)
    if rev:
        tri, cross, dist = col >= row, col_sub > sub, col - row
    else:
        tri, cross, dist = col <= row, col_sub < sub, row - col
    off_blk = jnp.where(cross, sub, -1)
    dmat = jnp.where(tri, jnp.where(col_sub == sub, dist, -1), -1)
    return tri.astype(BF), sub, off_blk, dmat


def _gla_chunk(q, k, v, g, st, rev, geom):
    tri, sub, off_blk, dmat = geom
    nsub = C_CHUNK // C_SUB
    b = _split_dot_left(tri, g) * LOG2E
    b_tot = b[0:1] if rev else b[C_CHUNK - 1:C_CHUNK]
    o = _dot_nt((q * jnp.exp2(b)).astype(BF), st.astype(BF))
    blocks = range(nsub - 1) if rev else range(1, nsub)
    bounds = {}
    ref_rows = jnp.zeros_like(b)
    for a in blocks:
        r = (a + 1) * C_SUB if rev else a * C_SUB - 1
        bounds[a] = b[r:r + 1]
        ref_rows = jnp.where(sub == a, bounds[a], ref_rows)
    q_t = (q * jnp.exp2(b - ref_rows)).astype(BF)
    a_mat = jnp.zeros((C_CHUNK, C_CHUNK), F32)
    for a in blocks:
        k_t = (k * jnp.exp2(jnp.minimum(bounds[a] - b, 0.0))).astype(BF)
        a_mat = jnp.where(off_blk == a, _dot_nt(q_t, k_t), a_mat)
    for d in range(C_SUB):
        shift = (C_CHUNK - d) % C_CHUNK if rev else d
        k_d = k if d == 0 else pltpu.roll(k, shift, axis=0)
        b_d = b if d == 0 else pltpu.roll(b, shift, axis=0)
        term = q * k_d * jnp.exp2(b - b_d)
        diag = jnp.sum(term, axis=-1, keepdims=True)
        a_mat = jnp.where(dmat == d, diag, a_mat)
    o = o + _dot(a_mat.astype(BF), v)
    k_end = (k * jnp.exp2(b_tot - b)).astype(BF)
    st_new = st * jnp.exp2(b_tot) + _dot_tn(v, k_end)
    return o, st_new


def _gla_kernel(ql_ref, kl_ref, vl_ref, ogl_ref, gfl_ref, gbl_ref,
                qc_ref, kc_ref, vc_ref, ogc_ref, gfc_ref, gbc_ref, ng_ref,
                ol_ref, oc_ref, fl_ref, bl_ref, fc_ref, bc_ref, *, with_ctx_out):
    dk = ql_ref.shape[-1]
    dv = vl_ref.shape[-1]
    n_lat = ql_ref.shape[0] // C_CHUNK
    n_ctx = qc_ref.shape[0] // C_CHUNK
    geom_f = _gla_geometry(False)
    geom_b = _gla_geometry(True)

    def load(q_ref, k_ref, v_ref, g_ref, c):
        sl = pl.ds(pl.multiple_of(c * C_CHUNK, C_CHUNK), C_CHUNK)
        return q_ref[sl, :].astype(F32), k_ref[sl, :].astype(F32), v_ref[sl, :], g_ref[sl, :]

    def chunk_at(c):
        return pl.ds(pl.multiple_of(c * C_CHUNK, C_CHUNK), C_CHUNK)

    def scan_both(refs_f, refs_b, f_ref, b_ref, n_chunks, states):
        def body(j, carry):
            st_f, st_b = carry
            cb = n_chunks - 1 - j
            q, k, v, g = load(*refs_f, j)
            o, st_f = _gla_chunk(q, k, v, g, st_f, False, geom_f)
            f_ref[chunk_at(j), :] = o
            q, k, v, g = load(*refs_b, cb)
            o, st_b = _gla_chunk(q, k, v, g, st_b, True, geom_b)
            b_ref[chunk_at(cb), :] = o
            return st_f, st_b
        return lax.fori_loop(0, n_chunks, body, states, unroll=4)

    def finish(f_ref, b_ref, og_ref, out_ref):
        def body(i, carry):
            sl = pl.ds(pl.multiple_of(i * GLA_FINISH_ROWS, GLA_FINISH_ROWS), GLA_FINISH_ROWS)
            tot = f_ref[sl, :] + b_ref[sl, :]
            ms = jnp.mean(tot * tot, axis=-1, keepdims=True)
            y = tot * lax.rsqrt(ms + NORM_EPS) * ng_ref[...]
            og = og_ref[sl, :].astype(F32)
            out_ref[sl, :] = (y * (og * _sigmoid(og))).astype(BF)
            return carry
        lax.fori_loop(0, f_ref.shape[0] // GLA_FINISH_ROWS, body, 0)

    zero = jnp.zeros((dv, dk), F32)
    lat_f = (ql_ref, kl_ref, vl_ref, gfl_ref)
    lat_b = (ql_ref, kl_ref, vl_ref, gbl_ref)
    ctx_f = (qc_ref, kc_ref, vc_ref, gfc_ref)
    ctx_b = (qc_ref, kc_ref, vc_ref, gbc_ref)
    states = scan_both(ctx_f, ctx_b, fc_ref, bc_ref, n_ctx, (zero, zero))
    scan_both(lat_f, lat_b, fl_ref, bl_ref, n_lat, states)
    finish(fl_ref, bl_ref, ogl_ref, ol_ref)
    if with_ctx_out:
        finish(fc_ref, bc_ref, ogc_ref, oc_ref)
    else:
        oc_ref[...] = jnp.zeros_like(oc_ref)


def _gla(proj_lat, gates_lat, proj_ctx, gates_ctx, norm_g, with_ctx_out):
    bsz, n_lat, _ = proj_lat.shape
    n_ctx = proj_ctx.shape[1]
    nh = C_HEADS
    dk = 512 // nh
    dv = 1024 // nh
    kb, vb = 512 // dk, 1024 // dv

    def specs(n_tok):
        return [
            pl.BlockSpec((None, n_tok, dk), lambda b, h: (b, 0, h)),
            pl.BlockSpec((None, n_tok, dk), lambda b, h: (b, 0, kb + h)),
            pl.BlockSpec((None, n_tok, dv), lambda b, h: (b, 0, (2 * kb * dk) // dv + h)),
            pl.BlockSpec((None, n_tok, dv), lambda b, h: (b, 0, (2 * kb * dk) // dv + vb + h)),
            pl.BlockSpec((None, n_tok, dk), lambda b, h: (b, 0, h)),
            pl.BlockSpec((None, n_tok, dk), lambda b, h: (b, 0, kb + h)),
        ]

    out_l, out_c = pl.pallas_call(
        functools.partial(_gla_kernel, with_ctx_out=with_ctx_out),
        out_shape=(jax.ShapeDtypeStruct((bsz, n_lat, nh * dv), BF),
                   jax.ShapeDtypeStruct((bsz, n_ctx, nh * dv), BF)),
        grid=(bsz, nh),
        in_specs=specs(n_lat) + specs(n_ctx) + [pl.BlockSpec((1, dv), lambda b, h: (0, 0))],
        out_specs=(pl.BlockSpec((None, n_lat, dv), lambda b, h: (b, 0, h)),
                   pl.BlockSpec((None, n_ctx, dv), lambda b, h: (b, 0, h))),
        scratch_shapes=[pltpu.VMEM((n_lat, dv), F32), pltpu.VMEM((n_lat, dv), F32),
                        pltpu.VMEM((n_ctx, dv), F32), pltpu.VMEM((n_ctx, dv), F32)],
        compiler_params=_cparams(2),
        name="gated_linear_attention",
    )(proj_lat, proj_lat, proj_lat, proj_lat, gates_lat, gates_lat,
      proj_ctx, proj_ctx, proj_ctx, proj_ctx, gates_ctx, gates_ctx, norm_g.reshape(1, dv))
    return out_l, out_c


def kernel(x, c, ctx, c_ctx, mod_w, mod_b, ln1_g, ln1_b, ffn_w_up, ffn_conv_w, ffn_conv_b, ffn_w_down, ln2_g, ln2_b, a_w_qkv, a_q_norm, a_k_norm, a_w_o, b_w_qkv, b_lambda_q1, b_lambda_k1, b_lambda_q2, b_lambda_k2, b_subln, b_w_o, c_w_in, c_w_gate_fwd, c_b_gate_fwd, c_w_gate_bwd, c_b_gate_bwd, c_norm, c_w_o, d_w_qkv, d_rpb, d_w_o):
    bsz, n_lat, dm = x.shape
    n_ctx = ctx.shape[1]
    scale = HEAD_DIM ** -0.5 * LOG2E

    pad_rows = (-(bsz + 1)) % 8
    cond = jnp.concatenate([c, c_ctx[None, :], jnp.zeros((pad_rows, dm), F32)], axis=0)
    mod_all = _modulation(cond, mod_w, mod_b)

    x_lat, x_ctx = x, ctx
    for i in range(DEPTH):
        kind, j = i % N_MIXERS, i // N_MIXERS
        last = i == DEPTH - 1
        m_lat = mod_all[i, :bsz].reshape(bsz, N_MOD, dm)
        m_ctx = jnp.broadcast_to(mod_all[i, bsz].reshape(1, N_MOD, dm), (bsz, N_MOD, dm))

        if kind == 0:
            w = a_w_qkv[j].astype(BF)
            gains = (a_q_norm[j], a_k_norm[j])
            nq, nk = A_HEADS * HEAD_DIM // LANES, A_KV_HEADS * HEAD_DIM // LANES
            p_lat = _proj_qkv(x_lat, m_lat, w, n_q=nq, n_k=nk, n_v=nk, q_scale=scale, norm_gains=gains, rope=True)
            p_ctx = _proj_qkv(x_ctx, m_ctx, w, n_q=nq, n_k=nk, n_v=nk, q_scale=scale, norm_gains=gains)
            a_lat = _attn_gqa(p_lat, [p_ctx, p_lat])
            a_ctx = None if last else _attn_gqa(p_ctx, [p_ctx])
            w_o = a_w_o[j]
        elif kind == 1:
            w = b_w_qkv[j].astype(BF)
            nb = 2 * B_HEADS * HEAD_DIM // LANES
            p_lat = _proj_qkv(x_lat, m_lat, w, n_q=nb, n_k=nb, n_v=nb, q_scale=scale, rope=True)
            p_ctx = _proj_qkv(x_ctx, m_ctx, w, n_q=nb, n_k=nb, n_v=nb, q_scale=scale)
            lam_vecs = jnp.stack([b_lambda_q1[j], b_lambda_k1[j], b_lambda_q2[j], b_lambda_k2[j]])
            lam_init = 0.8 - 0.6 * math.exp(-0.3 * i)
            a_lat = _attn_diff(p_lat, [p_ctx, p_lat], lam_vecs, b_subln[j], lam_init)
            a_ctx = None if last else _attn_diff(p_ctx, [p_ctx], lam_vecs, b_subln[j], lam_init)
            w_o = b_w_o[j]
        elif kind == 2:
            w_in = c_w_in[j]
            n_main = w_in.shape[1] - 2 * C_GATE_RANK
            w_main = w_in[:, :n_main].astype(BF)
            w_z = jnp.pad(w_in[:, n_main:], ((0, 0), (0, LANES - 2 * C_GATE_RANK))).astype(BF)
            kd = c_w_gate_fwd.shape[-1]
            w_gate = jnp.zeros((LANES, 2 * kd), F32)
            w_gate = w_gate.at[:C_GATE_RANK, :kd].set(c_w_gate_fwd[j])
            w_gate = w_gate.at[C_GATE_RANK:2 * C_GATE_RANK, kd:].set(c_w_gate_bwd[j]).astype(BF)
            b_gate = jnp.concatenate([c_b_gate_fwd[j], c_b_gate_bwd[j]]).reshape(1, 2 * kd)
            q_scale = (kd // C_HEADS) ** -0.5
            nq = kd // LANES
            p_lat, g_lat = _proj_gla(x_lat, m_lat, w_main, w_z, w_gate, b_gate, q_scale=q_scale, n_q=nq)
            p_ctx, g_ctx = _proj_gla(x_ctx, m_ctx, w_main, w_z, w_gate, b_gate, q_scale=q_scale, n_q=nq)
            a_lat, a_ctx = _gla(p_lat, g_lat, p_ctx, g_ctx, c_norm[j], not last)
            w_o = c_w_o[j]
        else:
            w = d_w_qkv[j].astype(BF)
            nb = D_HEADS * HEAD_DIM // LANES
            p_lat = _proj_qkv(x_lat, m_lat, w, n_q=nb, n_k=nb, n_v=nb, q_scale=scale)
            p_ctx = _proj_qkv(x_ctx, m_ctx, w, n_q=nb, n_k=nb, n_v=nb, q_scale=scale)
            bias = _nb_bias_table(d_rpb[j], n_lat // GRID_W)
            a_lat = _attn_nb(p_lat, p_ctx, bias)
            a_ctx = None if last else _attn_ctx(p_ctx, nb)
            w_o = d_w_o[j]

        w_o = w_o.astype(BF)
        w_up = ffn_w_up[i].astype(BF)
        w_dn = ffn_w_down[i].astype(BF)
        ffn = (w_o, ln1_g[i], ln1_b[i], w_up, ffn_conv_w[i], ffn_conv_b[i], w_dn, ln2_g[i], ln2_b[i])
        x_lat = _mix_ffn(a_lat, x_lat, m_lat, *ffn)
        if not last:
            x_ctx = _mix_ffn(a_ctx, x_ctx, m_ctx, *ffn)
    return x_lat
```

```python
import functools
import math

import jax
import jax.numpy as jnp
import numpy as np
from jax import lax
from jax.experimental import pallas as pl
from jax.experimental.pallas import tpu as pltpu

F32 = jnp.float32
BF = jnp.bfloat16

DEPTH = 4
N_MIXERS = 4
N_MOD = 6
GRID_W = 64
ROPE_THETA = 10000.0
NORM_EPS = 1e-6
LN_EPS = 1e-5
HEAD_DIM = 64
A_HEADS, A_KV_HEADS = 16, 4
B_HEADS = 8
C_HEADS = 4
C_GATE_RANK = 16
C_GATE_NORMALIZER = 16.0
C_CHUNK = 64
C_SUB = 16
GLA_FINISH_ROWS = 256
GLA_GROUP = 4
D_HEADS = 16
WIN_R, WIN_C = 8, 16
FF_DIM = 2816
DEEPNORM_ALPHA = (2 * DEPTH) ** 0.25
MASK_VALUE = -1e30
LOG2E = math.log2(math.e)

LANES = 128
MXU_WIDTH = 256
VMEM_LIMIT_BYTES = 56 * 1024 * 1024


def _cparams(n_axes):
    return pltpu.CompilerParams(
        dimension_semantics=("arbitrary",) * n_axes,
        vmem_limit_bytes=VMEM_LIMIT_BYTES)


def _dot(a, b):
    return jnp.dot(a, b, preferred_element_type=F32)


def _dot_nt(a, b):
    return lax.dot_general(a, b, (((1,), (1,)), ((), ())), preferred_element_type=F32)


def _dot_tn(a, b):
    return lax.dot_general(a, b, (((0,), (0,)), ((), ())), preferred_element_type=F32)


def _split_dot(x, m_bf16):
    hi = x.astype(BF)
    lo = (x - hi.astype(F32)).astype(BF)
    return _dot(hi, m_bf16) + _dot(lo, m_bf16)


def _split_dot_left(m_bf16, x):
    hi = x.astype(BF)
    lo = (x - hi.astype(F32)).astype(BF)
    return _dot(m_bf16, hi) + _dot(m_bf16, lo)


def _sigmoid(x):
    return 1.0 / (1.0 + jnp.exp(-x))


def _modulate(x_ref, mod_ref, shift_row, scale_row):
    x = x_ref[...]
    return x * (1.0 + mod_ref[scale_row:scale_row + 1, :]) + mod_ref[shift_row:shift_row + 1, :]


def _layer_norm(z, g, b):
    mu = jnp.mean(z, axis=-1, keepdims=True)
    zc = z - mu
    var = jnp.mean(zc * zc, axis=-1, keepdims=True)
    return zc * lax.rsqrt(var + LN_EPS) * g + b


def _rope(y, cos_t, sin_t, lo_half):
    swap = jnp.where(lo_half, pltpu.roll(y, LANES - 32, axis=1), pltpu.roll(y, 32, axis=1))
    return y * cos_t + swap * sin_t


def _idiv(x, n):
    return lax.shift_right_logical(x, int(n).bit_length() - 1)


def _imod(x, n):
    return x & (n - 1)


def _lane_iota(shape=(1, LANES)):
    return lax.broadcasted_iota(jnp.int32, shape, len(shape) - 1)


def _mod_kernel(cond_ref, w_ref, b_ref, o_ref):
    cnd = cond_ref[...]
    act = cnd * _sigmoid(cnd)
    o_ref[...] = _dot(act.astype(BF), w_ref[...].astype(BF)) + b_ref[...]


def _modulation(cond, mod_w, mod_b):
    depth, dm, n = mod_w.shape
    rows = cond.shape[0]
    tn = 1536
    return pl.pallas_call(
        _mod_kernel,
        out_shape=jax.ShapeDtypeStruct((depth, rows, n), F32),
        grid=(depth, n // tn),
        in_specs=[
            pl.BlockSpec((rows, dm), lambda i, j: (0, 0)),
            pl.BlockSpec((None, dm, tn), lambda i, j: (i, 0, j)),
            pl.BlockSpec((None, 1, tn), lambda i, j: (i, 0, j)),
        ],
        out_specs=pl.BlockSpec((None, rows, tn), lambda i, j: (i, 0, j)),
        compiler_params=_cparams(2),
        name="adaln_modulation",
    )(cond, mod_w, mod_b.reshape(depth, 1, n))


def _row_tile(n_tok):
    return 512 if n_tok % 512 == 0 else n_tok


def _proj_call(kernel, x, mod, w, extras, extra_specs, out_shapes, out_specs, name):
    bsz, n_tok, dm = x.shape
    tm = _row_tile(n_tok)
    in_specs = [
        pl.BlockSpec((None, tm, dm), lambda b, i: (b, i, 0)),
        pl.BlockSpec((None, N_MOD, dm), lambda b, i: (b, 0, 0)),
        pl.BlockSpec(w.shape, lambda b, i: (0, 0)),
    ] + list(extra_specs(tm))
    return pl.pallas_call(
        kernel,
        out_shape=out_shapes,
        grid=(bsz, n_tok // tm),
        in_specs=in_specs,
        out_specs=out_specs(tm),
        compiler_params=_cparams(2),
        name=name,
    )(x, mod, w, *extras)


def _full_spec(arr):
    nd = arr.ndim
    return pl.BlockSpec(arr.shape, lambda b, i: (0,) * nd)


def _seg_matrix():
    idx = np.arange(LANES) // HEAD_DIM
    return jnp.asarray((idx[:, None] == idx[None, :]).astype(np.float32) / HEAD_DIM, dtype=BF)


def _rope_tables(n_tok):
    t = jnp.arange(n_tok)
    row = (t // GRID_W).astype(F32)
    col = (t % GRID_W).astype(F32)
    n_freq = HEAD_DIM // 4
    inv = 1.0 / (ROPE_THETA ** (jnp.arange(n_freq, dtype=F32) / n_freq))
    ang = jnp.concatenate([row[:, None] * inv, col[:, None] * inv], axis=-1)
    cos, sin = jnp.cos(ang), jnp.sin(ang)
    cos_t = jnp.concatenate([cos, cos, cos, cos], axis=-1)
    sin_t = jnp.concatenate([-sin, sin, -sin, sin], axis=-1)
    return cos_t, sin_t


def _proj_qkv_kernel(x_ref, mod_ref, w_ref, *rest, n_q, n_k, n_v, norm, rope, q_scale):
    rest = list(rest)
    seg_ref = gq_ref = gk_ref = cos_ref = sin_ref = None
    if norm:
        seg_ref, gq_ref, gk_ref = rest[:3]
        rest = rest[3:]
    if rope:
        cos_ref, sin_ref = rest[:2]
        rest = rest[2:]
    (o_ref,) = rest
    hb = _modulate(x_ref, mod_ref, 0, 1).astype(BF)
    lo_half = _imod(_lane_iota(), HEAD_DIM) < (HEAD_DIM // 2)
    n_blocks = n_q + n_k + n_v
    per_dot = MXU_WIDTH // LANES
    for c0 in range(0, n_blocks, per_dot):
        nb = min(per_dot, n_blocks - c0)
        y2 = _dot(hb, w_ref[:, c0 * LANES:(c0 + nb) * LANES])
        for j in range(nb):
            c = c0 + j
            y = y2[:, j * LANES:(j + 1) * LANES]
            if c < n_q + n_k:
                if norm:
                    ms = _split_dot(y * y, seg_ref[...])
                    gain = gq_ref[...] if c < n_q else gk_ref[...]
                    y = y * lax.rsqrt(ms + NORM_EPS) * gain
                if rope:
                    y = _rope(y, cos_ref[...], sin_ref[...], lo_half)
                if c < n_q and q_scale != 1.0:
                    y = y * q_scale
            o_ref[:, c * LANES:(c + 1) * LANES] = y.astype(BF)


def _proj_qkv(x, mod, w, *, n_q, n_k, n_v, q_scale, norm_gains=None, rope=False):
    bsz, n_tok, _ = x.shape
    n_out = w.shape[1]
    extras, spec_fns = [], []
    if norm_gains is not None:
        gq, gk = norm_gains
        extras += [_seg_matrix(), jnp.tile(gq, 2).reshape(1, LANES), jnp.tile(gk, 2).reshape(1, LANES)]
        spec_fns += [lambda tm, a=a: _full_spec(a) for a in extras]
    if rope:
        cos_t, sin_t = _rope_tables(n_tok)
        extras += [cos_t, sin_t]
        spec_fns += [lambda tm: pl.BlockSpec((tm, LANES), lambda b, i: (i, 0))] * 2
    kern = functools.partial(_proj_qkv_kernel, n_q=n_q, n_k=n_k, n_v=n_v,
                             norm=norm_gains is not None, rope=rope, q_scale=q_scale)
    return _proj_call(
        kern, x, mod, w, extras, lambda tm: [f(tm) for f in spec_fns],
        jax.ShapeDtypeStruct((bsz, n_tok, n_out), BF),
        lambda tm: pl.BlockSpec((None, tm, n_out), lambda b, i: (b, i, 0)),
        "mixer_qkv_projection")


def _log_sigmoid(x):
    return jnp.minimum(x, 0.0) - jnp.log(1.0 + jnp.exp(-jnp.abs(x)))


def _proj_gla_kernel(x_ref, mod_ref, w_ref, wz_ref, wg_ref, bg_ref, o_ref, g_ref, *, q_scale, n_q):
    hb = _modulate(x_ref, mod_ref, 0, 1).astype(BF)
    n_blocks = o_ref.shape[-1] // LANES
    per_dot = MXU_WIDTH // LANES
    for c0 in range(0, n_blocks, per_dot):
        y2 = _dot(hb, w_ref[:, c0 * LANES:(c0 + per_dot) * LANES])
        if c0 < n_q:
            y2 = y2 * q_scale
        o_ref[:, c0 * LANES:(c0 + per_dot) * LANES] = y2.astype(BF)
    z = _dot(hb, wz_ref[...]).astype(BF)
    n_gate = g_ref.shape[-1]
    for c0 in range(0, n_gate, MXU_WIDTH):
        pre = _dot(z, wg_ref[:, c0:c0 + MXU_WIDTH]) + bg_ref[:, c0:c0 + MXU_WIDTH]
        g_ref[:, c0:c0 + MXU_WIDTH] = _log_sigmoid(pre) * (1.0 / C_GATE_NORMALIZER)


def _proj_gla(x, mod, w_main, w_z, w_gate, b_gate, *, q_scale, n_q):
    bsz, n_tok, _ = x.shape
    n_out, n_gate = w_main.shape[1], w_gate.shape[1]
    extras = [w_z, w_gate, b_gate]
    kern = functools.partial(_proj_gla_kernel, q_scale=q_scale, n_q=n_q)
    return _proj_call(
        kern, x, mod, w_main, extras, lambda tm: [_full_spec(a) for a in extras],
        (jax.ShapeDtypeStruct((bsz, n_tok, n_out), BF), jax.ShapeDtypeStruct((bsz, n_tok, n_gate), F32)),
        lambda tm: (pl.BlockSpec((None, tm, n_out), lambda b, i: (b, i, 0)),
                    pl.BlockSpec((None, tm, n_gate), lambda b, i: (b, i, 0))),
        "gla_projection")


FFN_CHUNK = 256
FFN_TILE = 512
FFN_SUB = 512
HALO = 16


def _mix_ffn_kernel(a_ref, ap_ref, an_ref, x_ref, xp_ref, xn_ref, mod_ref, wo_ref, g1_ref, b1_ref,
                    wu_ref, cw_ref, cb_ref, wd_ref, g2_ref, b2_ref, o_ref, act_ref):
    tm = x_ref.shape[0]
    i = pl.program_id(1)
    n_i = pl.num_programs(1)
    ts = min(tm, FFN_SUB)
    n_sub = tm // ts
    n_chunks = FF_DIM // FFN_CHUNK
    rows = lax.broadcasted_iota(jnp.int32, (ts, 1), 0)
    first_row = rows == 0
    last_row = rows == ts - 1

    def halo(s):
        r0 = s * ts
        if s == 0:
            before = (ap_ref[...], xp_ref[...], (i > 0).astype(F32))
        else:
            before = (a_ref[r0 - HALO:r0, :], x_ref[r0 - HALO:r0, :], 1.0)
        if s == n_sub - 1:
            after = (an_ref[...], xn_ref[...], (i < n_i - 1).astype(F32))
        else:
            after = (a_ref[r0 + ts:r0 + ts + HALO, :], x_ref[r0 + ts:r0 + ts + HALO, :], 1.0)
        return before, after

    def out_proj(s):
        (a_prev, _, _), (a_next, _, _) = halo(s)
        a_ext = jnp.concatenate([a_ref[s * ts:(s + 1) * ts, :], a_prev, a_next], axis=0)
        return _dot(a_ext, wo_ref[...])

    def norm1(s, y1):
        (_, x_prev, _), (_, x_next, _) = halo(s)
        x_ext = jnp.concatenate([x_ref[s * ts:(s + 1) * ts, :], x_prev, x_next], axis=0)
        x_mid = _layer_norm(DEEPNORM_ALPHA * x_ext + mod_ref[2:3, :] * y1, g1_ref[...], b1_ref[...])
        return x_mid[:ts], (x_mid * (1.0 + mod_ref[4:5, :]) + mod_ref[3:4, :]).astype(BF)

    def up(hb, c):
        cg = c * FFN_CHUNK
        return _dot(hb, wu_ref[:, cg:cg + FFN_CHUNK]), _dot(hb, wu_ref[:, FF_DIM + cg:FF_DIM + cg + FFN_CHUNK])

    def conv_act(s, u_pair, c):
        (_, _, has_prev), (_, _, has_next) = halo(s)

        def conv(u_ext, c0):
            u = u_ext[:ts]
            prev = u_ext[ts + HALO - 1:ts + HALO] * has_prev
            nxt = u_ext[ts + HALO:ts + HALO + 1] * has_next
            u_dn = jnp.where(first_row, prev, pltpu.roll(u, 1, axis=0))
            u_up = jnp.where(last_row, nxt, pltpu.roll(u, ts - 1, axis=0))
            w0 = cw_ref[0:1, c0:c0 + FFN_CHUNK]
            w1 = cw_ref[1:2, c0:c0 + FFN_CHUNK]
            w2 = cw_ref[2:3, c0:c0 + FFN_CHUNK]
            return cb_ref[:, c0:c0 + FFN_CHUNK] + u_dn * w0 + u * w1 + u_up * w2

        cg = c * FFN_CHUNK
        gate = conv(u_pair[0], cg)
        val = conv(u_pair[1], FF_DIM + cg)
        act_ref[s * ts:(s + 1) * ts, cg:cg + FFN_CHUNK] = (gate * _sigmoid(gate) * val).astype(BF)

    def norm2(s, x_mid, y2):
        z2 = DEEPNORM_ALPHA * x_mid + mod_ref[5:6, :] * y2
        o_ref[s * ts:(s + 1) * ts, :] = _layer_norm(z2, g2_ref[...], b2_ref[...])

    y1 = [out_proj(s) for s in range(n_sub)]
    deferred = None
    for s in range(n_sub):
        x_mid, hb = norm1(s, y1[s])
        u_pair = up(hb, 0)
        for c in range(n_chunks):
            following = up(hb, c + 1) if c + 1 < n_chunks else None
            conv_act(s, u_pair, c)
            u_pair = following
        y2 = _dot(act_ref[s * ts:(s + 1) * ts, :], wd_ref[...])
        if deferred is not None:
            norm2(*deferred)
        deferred = (s, x_mid, y2)
    norm2(*deferred)


def _mix_ffn(a, x, mod, w_o, ln1_g, ln1_b, w_up, conv_w, conv_b, w_down, ln2_g, ln2_b):
    bsz, n_tok, dm = x.shape
    k_in = a.shape[-1]
    tm = FFN_TILE if n_tok % FFN_TILE == 0 else n_tok
    n_halo = n_tok // HALO
    per = tm // HALO
    tile = lambda b, i: (b, i, 0)
    prev = lambda b, i: (b, jnp.maximum(i * per - 1, 0), 0)
    nxt = lambda b, i: (b, jnp.minimum((i + 1) * per, n_halo - 1), 0)
    const2 = lambda b, i: (0, 0)
    resident = functools.partial(pl.BlockSpec, index_map=const2, pipeline_mode=pl.Buffered(1))
    row = lambda v: v.reshape(1, -1)
    return pl.pallas_call(
        _mix_ffn_kernel,
        out_shape=jax.ShapeDtypeStruct((bsz, n_tok, dm), F32),
        grid=(bsz, n_tok // tm),
        in_specs=[
            pl.BlockSpec((None, tm, k_in), tile),
            pl.BlockSpec((None, HALO, k_in), prev),
            pl.BlockSpec((None, HALO, k_in), nxt),
            pl.BlockSpec((None, tm, dm), tile),
            pl.BlockSpec((None, HALO, dm), prev),
            pl.BlockSpec((None, HALO, dm), nxt),
            pl.BlockSpec((None, N_MOD, dm), lambda b, i: (b, 0, 0)),
            resident(w_o.shape),
            pl.BlockSpec((1, dm), const2),
            pl.BlockSpec((1, dm), const2),
            resident(w_up.shape),
            pl.BlockSpec(conv_w.shape, const2),
            pl.BlockSpec((1, 2 * FF_DIM), const2),
            resident(w_down.shape),
            pl.BlockSpec((1, dm), const2),
            pl.BlockSpec((1, dm), const2),
        ],
        out_specs=pl.BlockSpec((None, tm, dm), tile),
        scratch_shapes=[pltpu.VMEM((tm, FF_DIM), BF)],
        compiler_params=_cparams(2),
        name="outproj_convffn_layernorm",
    )(a, a, a, x, x, x, mod, w_o, row(ln1_g), row(ln1_b), w_up, conv_w, row(conv_b), w_down,
      row(ln2_g), row(ln2_b))


ATT_SUB = 512
ATT_TILE = 2048


def _attn_tile(n_tok):
    return ATT_TILE if n_tok % ATT_TILE == 0 else n_tok


def _transpose_values(v_ref):
    return v_ref[...].astype(F32).T.astype(BF)


def _scores_t(q, keys, biases_t):
    return [_dot_nt(k, q) if bias is None else _dot_nt(k, q) + bias for k, bias in zip(keys, biases_t)]


def _probs_t(scores):
    m = functools.reduce(jnp.maximum, [jnp.max(s, axis=0, keepdims=True) for s in scores])
    probs = [jnp.exp2(s - m) for s in scores]
    denom = functools.reduce(lambda a, b: a + b, [jnp.sum(p, axis=0, keepdims=True) for p in probs])
    return [p.astype(BF) for p in probs], denom


def _pv_t(probs, denom, values_t):
    out_t = functools.reduce(lambda a, b: a + b, [_dot(vt, p) for p, vt in zip(probs, values_t)])
    return (out_t / denom).T


def _softmax_pv_t(scores, values_t):
    return _pv_t(*_probs_t(scores), values_t)


def _attend_pipelined(items, emit):
    n = len(items)
    scores = {j: _scores_t(*items[j][:3]) for j in range(min(2, n))}
    probs = {0: _probs_t(scores.pop(0))}
    for i in range(n):
        if i + 2 < n:
            scores[i + 2] = _scores_t(*items[i + 2][:3])
        if i + 1 < n:
            probs[i + 1] = _probs_t(scores.pop(i + 1))
        emit(i, _pv_t(*probs.pop(i), items[i][3]))


def _attn_gqa_kernel(q_ref, *rest, n_seg, n_blk):
    k_refs, v_refs, o_ref = rest[:n_seg], rest[n_seg:2 * n_seg], rest[2 * n_seg]
    grp = A_HEADS // A_KV_HEADS
    half = _idiv(_lane_iota(), HEAD_DIM)
    values_t = [_transpose_values(r) for r in v_refs]
    items, place = [], []
    for blk in range(n_blk):
        head = (pl.program_id(1) if n_blk == 1 else blk) * 2
        kv_pos = (head // grp) % 2
        kv_blk = 0 if n_blk == 1 else blk * 2 // grp // 2
        keys = [r[:, kv_blk * LANES:(kv_blk + 1) * LANES] for r in k_refs]
        values = [vt[kv_blk * LANES:(kv_blk + 1) * LANES, :] for vt in values_t]
        for r0 in range(0, q_ref.shape[0], ATT_SUB):
            q = q_ref[r0:r0 + ATT_SUB, blk * LANES:(blk + 1) * LANES].astype(F32)
            q_sw = pltpu.roll(q, HEAD_DIM, axis=1)
            for e in range(2):
                q_e = jnp.where(kv_pos == e, q, q_sw)
                q_e = jnp.where(half == kv_pos, q_e, 0.0).astype(BF)
                items.append((q_e, keys, [None] * n_seg, values))
                place.append((blk, r0, e, kv_pos))
    outs = {}

    def emit(i, o):
        blk, r0, e, kv_pos = place[i]
        outs[e] = jnp.where(kv_pos == e, o, pltpu.roll(o, HEAD_DIM, axis=1))
        if e == 1:
            o_ref[r0:r0 + ATT_SUB, blk * LANES:(blk + 1) * LANES] = (
                jnp.where(half == 0, outs[0], outs[1]).astype(BF))

    _attend_pipelined(items, emit)


def _blocks_per_step(n_q_tok, n_blocks):
    return n_blocks if n_q_tok < ATT_SUB else 1


def _attn_gqa(q_src, kv_srcs):
    bsz, n_q_tok, _ = q_src.shape
    tq = _attn_tile(n_q_tok)
    n_qb = A_HEADS * HEAD_DIM // LANES
    n_kb = A_KV_HEADS * HEAD_DIM // LANES
    per_kb = n_qb // n_kb
    n_seg = len(kv_srcs)
    n_blk = _blocks_per_step(n_q_tok, n_qb)
    kv_w = LANES if n_blk == 1 else n_kb * LANES
    k0, v0 = n_qb * LANES // kv_w, (n_qb + n_kb) * LANES // kv_w
    k_specs = [pl.BlockSpec((None, s.shape[1], kv_w), lambda b, p, i: (b, 0, k0 + p * n_blk // per_kb))
               for s in kv_srcs]
    v_specs = [pl.BlockSpec((None, s.shape[1], kv_w), lambda b, p, i: (b, 0, v0 + p * n_blk // per_kb))
               for s in kv_srcs]
    return pl.pallas_call(
        functools.partial(_attn_gqa_kernel, n_seg=n_seg, n_blk=n_blk),
        out_shape=jax.ShapeDtypeStruct((bsz, n_q_tok, n_qb * LANES), BF),
        grid=(bsz, n_qb // n_blk, n_q_tok // tq),
        in_specs=[pl.BlockSpec((None, tq, n_blk * LANES), lambda b, p, i: (b, i, p))] + k_specs + v_specs,
        out_specs=pl.BlockSpec((None, tq, n_blk * LANES), lambda b, p, i: (b, i, p)),
        compiler_params=_cparams(3),
        name="gqa_attention",
    )(q_src, *kv_srcs, *kv_srcs)


def _attn_diff_kernel(q_ref, lam_ref, subln_ref, *rest, n_seg, lam_init):
    k_refs, v_refs, o_ref = rest[:n_seg], rest[n_seg:2 * n_seg], rest[2 * n_seg]
    lam_v = lam_ref[...]
    lam = (jnp.exp(jnp.sum(lam_v[0:1] * lam_v[1:2], axis=-1, keepdims=True))
           - jnp.exp(jnp.sum(lam_v[2:3] * lam_v[3:4], axis=-1, keepdims=True)) + lam_init)
    half = _idiv(_lane_iota(), HEAD_DIM)
    values_t = [_transpose_values(r) for r in v_refs]
    n_blk = q_ref.shape[1] // LANES
    items, place = [], []
    for blk in range(n_blk):
        lanes = slice(blk * LANES, (blk + 1) * LANES)
        keys = [r[:, lanes] for r in k_refs]
        values = [vt[lanes, :] for vt in values_t]
        for r0 in range(0, q_ref.shape[0], ATT_SUB):
            q = q_ref[r0:r0 + ATT_SUB, lanes]
            for e in range(2):
                items.append((jnp.where(half == e, q, jnp.zeros_like(q)), keys, [None] * n_seg, values))
                place.append((lanes, r0, e))
    outs = {}

    def emit(i, o):
        lanes, r0, e = place[i]
        outs[e] = o
        if e == 1:
            o = outs[0] - lam * outs[1]
            ms = jnp.mean(o * o, axis=-1, keepdims=True)
            o = o * lax.rsqrt(ms + NORM_EPS) * subln_ref[...] * (1.0 - lam_init)
            o_ref[r0:r0 + ATT_SUB, lanes] = o.astype(BF)

    _attend_pipelined(items, emit)


def _attn_diff(q_src, kv_srcs, lam_vecs, subln, lam_init):
    bsz, n_q_tok, _ = q_src.shape
    tq = _attn_tile(n_q_tok)
    nh = B_HEADS
    n_seg = len(kv_srcs)
    n_blk = _blocks_per_step(n_q_tok, nh)
    w = n_blk * LANES
    steps = nh // n_blk
    k_specs = [pl.BlockSpec((None, s.shape[1], w), lambda b, h, i: (b, 0, steps + h)) for s in kv_srcs]
    v_specs = [pl.BlockSpec((None, s.shape[1], w), lambda b, h, i: (b, 0, 2 * steps + h)) for s in kv_srcs]
    return pl.pallas_call(
        functools.partial(_attn_diff_kernel, n_seg=n_seg, lam_init=lam_init),
        out_shape=jax.ShapeDtypeStruct((bsz, n_q_tok, nh * LANES), BF),
        grid=(bsz, steps, n_q_tok // tq),
        in_specs=[pl.BlockSpec((None, tq, w), lambda b, h, i: (b, i, h)),
                  pl.BlockSpec(lam_vecs.shape, lambda b, h, i: (0, 0)),
                  pl.BlockSpec((1, LANES), lambda b, h, i: (0, 0))] + k_specs + v_specs,
        out_specs=pl.BlockSpec((None, tq, w), lambda b, h, i: (b, i, h)),
        compiler_params=_cparams(3),
        name="differential_attention",
    )(q_src, lam_vecs, subln.reshape(1, LANES), *kv_srcs, *kv_srcs)


NB_QROWS = 4
NB_BAND = 12


def _nb_band_start(g, rows):
    return int(np.clip(NB_QROWS * g - NB_QROWS, 0, rows - NB_BAND))


def _nb_config(g, rows):
    n_g = rows // NB_QROWS
    return 0 if g == 0 else (2 if g == n_g - 1 else 1)


def _nb_bias_table(rpb, rows):
    nh = rpb.shape[0]
    wr = min(WIN_R, rows)
    cq = np.arange(GRID_W)
    c0 = np.clip(cq - WIN_C // 2, 0, GRID_W - WIN_C)
    col_in = (cq[None, :] >= c0[:, None]) & (cq[None, :] < c0[:, None] + WIN_C)
    dc_idx = np.clip(cq[None, :] - cq[:, None], -(WIN_C - 1), WIN_C - 1) + WIN_C - 1
    n_dr = 2 * WIN_R - 1
    tiles = rpb[:, :, dc_idx]
    tiles = jnp.where(col_in[None, None], tiles * LOG2E, MASK_VALUE)
    tiles = jnp.concatenate([tiles, jnp.full((nh, 1, GRID_W, GRID_W), MASK_VALUE, F32)], axis=1)
    n_g = rows // NB_QROWS
    sel = np.full((3, NB_QROWS, NB_BAND), n_dr, dtype=np.int32)
    for cfg, g in ((0, 0), (1, 1), (2, n_g - 1)):
        start = _nb_band_start(g, rows)
        for a in range(NB_QROWS):
            r = NB_QROWS * g + a
            r0 = int(np.clip(r - wr // 2, 0, rows - wr))
            for jb in range(NB_BAND):
                kr = start + jb
                if r0 <= kr < r0 + wr:
                    sel[cfg, a, jb] = kr - r + (WIN_R - 1)
    big = tiles[:, sel]
    big = big.transpose(0, 1, 3, 5, 2, 4)
    return big.reshape(nh, 3, NB_BAND * GRID_W, NB_QROWS * GRID_W)


def _attn_nb_kernel(q_ref, kl_ref, vl_ref, kc_ref, vc_ref, bias_ref, o_ref, *, rows):
    half = _idiv(_lane_iota(), HEAD_DIM)
    kc = kc_ref[...]
    vc = _transpose_values(vc_ref)
    vl = _transpose_values(vl_ref)
    tq = NB_QROWS * GRID_W
    tk = NB_BAND * GRID_W
    items = []
    for g in range(rows // NB_QROWS):
        ks = _nb_band_start(g, rows) * GRID_W
        cfg = _nb_config(g, rows)
        q = q_ref[g * tq:(g + 1) * tq, :]
        kb = kl_ref[ks:ks + tk, :]
        vb = vl[:, ks:ks + tk]
        for e in range(2):
            q_e = jnp.where(half == e, q, jnp.zeros_like(q))
            items.append((q_e, [kb, kc], [bias_ref[e, cfg], None], [vb, vc]))
    outs = {}

    def emit(i, o):
        g, e = i // 2, i % 2
        outs[e] = o
        if e == 1:
            o_ref[g * tq:(g + 1) * tq, :] = jnp.where(half == 0, outs[0], outs[1]).astype(BF)

    _attend_pipelined(items, emit)


def _attn_nb(qkv_lat, qkv_ctx, bias):
    bsz, n_tok, _ = qkv_lat.shape
    n_ctx = qkv_ctx.shape[1]
    rows = n_tok // GRID_W
    npair = D_HEADS * HEAD_DIM // LANES
    return pl.pallas_call(
        functools.partial(_attn_nb_kernel, rows=rows),
        out_shape=jax.ShapeDtypeStruct((bsz, n_tok, npair * LANES), BF),
        grid=(npair, bsz),
        in_specs=[
            pl.BlockSpec((None, n_tok, LANES), lambda p, b: (b, 0, p)),
            pl.BlockSpec((None, n_tok, LANES), lambda p, b: (b, 0, npair + p)),
            pl.BlockSpec((None, n_tok, LANES), lambda p, b: (b, 0, 2 * npair + p)),
            pl.BlockSpec((None, n_ctx, LANES), lambda p, b: (b, 0, npair + p)),
            pl.BlockSpec((None, n_ctx, LANES), lambda p, b: (b, 0, 2 * npair + p)),
            pl.BlockSpec((2,) + bias.shape[1:], lambda p, b: (p, 0, 0, 0)),
        ],
        out_specs=pl.BlockSpec((None, n_tok, LANES), lambda p, b: (b, 0, p)),
        compiler_params=_cparams(2),
        name="neighbourhood_attention",
    )(qkv_lat, qkv_lat, qkv_lat, qkv_ctx, qkv_ctx, bias)


def _attn_ctx_kernel(q_ref, k_ref, v_ref, o_ref):
    half = _idiv(_lane_iota(), HEAD_DIM)
    q = q_ref[...]
    zero = jnp.zeros_like(q)
    keys, values = [k_ref[...]], [_transpose_values(v_ref)]
    outs = [_softmax_pv_t(_scores_t(jnp.where(half == e, q, zero), keys, [None]), values) for e in range(2)]
    o_ref[...] = jnp.where(half == 0, outs[0], outs[1]).astype(BF)


def _attn_ctx(qkv_ctx, npair):
    bsz, n_ctx, _ = qkv_ctx.shape
    return pl.pallas_call(
        _attn_ctx_kernel,
        out_shape=jax.ShapeDtypeStruct((bsz, n_ctx, npair * LANES), BF),
        grid=(bsz, npair),
        in_specs=[
            pl.BlockSpec((None, n_ctx, LANES), lambda b, p: (b, 0, p)),
            pl.BlockSpec((None, n_ctx, LANES), lambda b, p: (b, 0, npair + p)),
            pl.BlockSpec((None, n_ctx, LANES), lambda b, p: (b, 0, 2 * npair + p)),
        ],
        out_specs=pl.BlockSpec((None, n_ctx, LANES), lambda b, p: (b, 0, p)),
        compiler_params=_cparams(2),
        name="context_attention",
    )(qkv_ctx, qkv_ctx, qkv_ctx)


def _gla_geometry(rev):
    row = lax.broadcasted_iota(jnp.int32, (C_CHUNK, 1), 0)
    col = lax.broadcasted_iota(jnp.int32, (1, C_CHUNK), 1)
    sub = _idiv(row, C_SUB)
    col_sub = _idiv(col, C_SUB)
    if rev:
        tri, cross, dist = col >= row, col_sub > sub, col - row
    else:
        tri, cross, dist = col <= row, col_sub < sub, row - col
    off_blk = jnp.where(cross, sub, -1)
    dmat = jnp.where(tri, jnp.where(col_sub == sub, dist, -1), -1)
    return tri.astype(BF), sub, off_blk, dmat


def _gla_decay(g, geom):
    return _split_dot_left(geom[0], g) * LOG2E


def _gla_factors(q, k, b, rev, geom):
    sub = geom[1]
    nsub = C_CHUNK // C_SUB
    b_tot = b[0:1] if rev else b[C_CHUNK - 1:C_CHUNK]
    blocks = range(nsub - 1) if rev else range(1, nsub)
    bounds = {}
    ref_rows = jnp.zeros_like(b)
    for a in blocks:
        r = (a + 1) * C_SUB if rev else a * C_SUB - 1
        bounds[a] = b[r:r + 1]
        ref_rows = jnp.where(sub == a, bounds[a], ref_rows)
    q_in = (q * jnp.exp2(b)).astype(BF)
    q_t = (q * jnp.exp2(b - ref_rows)).astype(BF)
    k_ts = {a: (k * jnp.exp2(jnp.minimum(bounds[a] - b, 0.0))).astype(BF) for a in blocks}
    k_end = (k * jnp.exp2(b_tot - b)).astype(BF)
    return q_in, q_t, k_ts, k_end, jnp.exp2(b_tot)


def _gla_cross(factors, v, geom):
    _, q_t, k_ts, k_end, _ = factors
    a_mat = jnp.zeros((C_CHUNK, C_CHUNK), F32)
    for a, k_t in k_ts.items():
        a_mat = jnp.where(geom[2] == a, _dot_nt(q_t, k_t), a_mat)
    return a_mat, _dot_tn(v, k_end)


def _gla_same(q, k, b, a_mat, rev, geom):
    dmat = geom[3]
    for d in range(C_SUB):
        shift = (C_CHUNK - d) % C_CHUNK if rev else d
        k_d = k if d == 0 else pltpu.roll(k, shift, axis=0)
        b_d = b if d == 0 else pltpu.roll(b, shift, axis=0)
        term = q * k_d * jnp.exp2(b - b_d)
        a_mat = jnp.where(dmat == d, jnp.sum(term, axis=-1, keepdims=True), a_mat)
    return a_mat


def _gla_advance(factors, a_mat, v, st, increment):
    q_in, _, _, _, decay = factors
    o = _dot_nt(q_in, st.astype(BF)) + _dot(a_mat.astype(BF), v)
    return o, st * decay + increment


def _gla_kernel(ql_ref, kl_ref, vl_ref, ogl_ref, gfl_ref, gbl_ref,
                qc_ref, kc_ref, vc_ref, ogc_ref, gfc_ref, gbc_ref, ng_ref,
                ol_ref, oc_ref, fl_ref, bl_ref, fc_ref, bc_ref, *, with_ctx_out):
    dk = ql_ref.shape[-1]
    dv = vl_ref.shape[-1]
    n_lat = ql_ref.shape[0] // C_CHUNK
    n_ctx = qc_ref.shape[0] // C_CHUNK
    geom_f = _gla_geometry(False)
    geom_b = _gla_geometry(True)

    def load(q_ref, k_ref, v_ref, g_ref, c):
        sl = pl.ds(pl.multiple_of(c * C_CHUNK, C_CHUNK), C_CHUNK)
        return q_ref[sl, :].astype(F32), k_ref[sl, :].astype(F32), v_ref[sl, :], g_ref[sl, :]

    def chunk_at(c):
        return pl.ds(pl.multiple_of(c * C_CHUNK, C_CHUNK), C_CHUNK)

    def scan_both(refs_f, refs_b, f_ref, b_ref, n_chunks, states):
        def body(step, carry):
            st_f, st_b = carry
            work = []
            for u in range(GLA_GROUP):
                j = step * GLA_GROUP + u
                work.append((False, geom_f, refs_f, j, f_ref))
                work.append((True, geom_b, refs_b, n_chunks - 1 - j, b_ref))
            data = [load(*refs, c) for _, _, refs, c, _ in work]
            decays = [_gla_decay(d[3], w[1]) for d, w in zip(data, work)]
            factors = [_gla_factors(d[0], d[1], b, w[0], w[1]) for d, b, w in zip(data, decays, work)]
            cross = [_gla_cross(f, d[2], w[1]) for f, d, w in zip(factors, data, work)]
            a_mats = [_gla_same(d[0], d[1], b, x[0], w[0], w[1])
                      for d, b, x, w in zip(data, decays, cross, work)]
            for idx, (rev, _, _, c, out_ref) in enumerate(work):
                st = st_b if rev else st_f
                o, st = _gla_advance(factors[idx], a_mats[idx], data[idx][2], st, cross[idx][1])
                out_ref[chunk_at(c), :] = o
                if rev:
                    st_b = st
                else:
                    st_f = st
            return st_f, st_b
        return lax.fori_loop(0, n_chunks // GLA_GROUP, body, states)

    def finish(f_ref, b_ref, og_ref, out_ref):
        def body(i, carry):
            sl = pl.ds(pl.multiple_of(i * GLA_FINISH_ROWS, GLA_FINISH_ROWS), GLA_FINISH_ROWS)
            tot = f_ref[sl, :] + b_ref[sl, :]
            ms = jnp.mean(tot * tot, axis=-1, keepdims=True)
            y = tot * lax.rsqrt(ms + NORM_EPS) * ng_ref[...]
            og = og_ref[sl, :].astype(F32)
            out_ref[sl, :] = (y * (og * _sigmoid(og))).astype(BF)
            return carry
        lax.fori_loop(0, f_ref.shape[0] // GLA_FINISH_ROWS, body, 0)

    zero = jnp.zeros((dv, dk), F32)
    lat_f = (ql_ref, kl_ref, vl_ref, gfl_ref)
    lat_b = (ql_ref, kl_ref, vl_ref, gbl_ref)
    ctx_f = (qc_ref, kc_ref, vc_ref, gfc_ref)
    ctx_b = (qc_ref, kc_ref, vc_ref, gbc_ref)
    states = scan_both(ctx_f, ctx_b, fc_ref, bc_ref, n_ctx, (zero, zero))
    scan_both(lat_f, lat_b, fl_ref, bl_ref, n_lat, states)
    finish(fl_ref, bl_ref, ogl_ref, ol_ref)
    if with_ctx_out:
        finish(fc_ref, bc_ref, ogc_ref, oc_ref)
    else:
        oc_ref[...] = jnp.zeros_like(oc_ref)


def _gla(proj_lat, gates_lat, proj_ctx, gates_ctx, norm_g, with_ctx_out):
    bsz, n_lat, _ = proj_lat.shape
    n_ctx = proj_ctx.shape[1]
    nh = C_HEADS
    dk = 512 // nh
    dv = 1024 // nh
    kb, vb = 512 // dk, 1024 // dv

    def specs(n_tok):
        return [
            pl.BlockSpec((None, n_tok, dk), lambda b, h: (b, 0, h)),
            pl.BlockSpec((None, n_tok, dk), lambda b, h: (b, 0, kb + h)),
            pl.BlockSpec((None, n_tok, dv), lambda b, h: (b, 0, (2 * kb * dk) // dv + h)),
            pl.BlockSpec((None, n_tok, dv), lambda b, h: (b, 0, (2 * kb * dk) // dv + vb + h)),
            pl.BlockSpec((None, n_tok, dk), lambda b, h: (b, 0, h)),
            pl.BlockSpec((None, n_tok, dk), lambda b, h: (b, 0, kb + h)),
        ]

    out_l, out_c = pl.pallas_call(
        functools.partial(_gla_kernel, with_ctx_out=with_ctx_out),
        out_shape=(jax.ShapeDtypeStruct((bsz, n_lat, nh * dv), BF),
                   jax.ShapeDtypeStruct((bsz, n_ctx, nh * dv), BF)),
        grid=(bsz, nh),
        in_specs=specs(n_lat) + specs(n_ctx) + [pl.BlockSpec((1, dv), lambda b, h: (0, 0))],
        out_specs=(pl.BlockSpec((None, n_lat, dv), lambda b, h: (b, 0, h)),
                   pl.BlockSpec((None, n_ctx, dv), lambda b, h: (b, 0, h))),
        scratch_shapes=[pltpu.VMEM((n_lat, dv), F32), pltpu.VMEM((n_lat, dv), F32),
                        pltpu.VMEM((n_ctx, dv), F32), pltpu.VMEM((n_ctx, dv), F32)],
        compiler_params=_cparams(2),
        name="gated_linear_attention",
    )(proj_lat, proj_lat, proj_lat, proj_lat, gates_lat, gates_lat,
      proj_ctx, proj_ctx, proj_ctx, proj_ctx, gates_ctx, gates_ctx, norm_g.reshape(1, dv))
    return out_l, out_c


def kernel(x, c, ctx, c_ctx, mod_w, mod_b, ln1_g, ln1_b, ffn_w_up, ffn_conv_w, ffn_conv_b, ffn_w_down, ln2_g, ln2_b, a_w_qkv, a_q_norm, a_k_norm, a_w_o, b_w_qkv, b_lambda_q1, b_lambda_k1, b_lambda_q2, b_lambda_k2, b_subln, b_w_o, c_w_in, c_w_gate_fwd, c_b_gate_fwd, c_w_gate_bwd, c_b_gate_bwd, c_norm, c_w_o, d_w_qkv, d_rpb, d_w_o):
    bsz, n_lat, dm = x.shape
    n_ctx = ctx.shape[1]
    scale = HEAD_DIM ** -0.5 * LOG2E

    pad_rows = (-(bsz + 1)) % 8
    cond = jnp.concatenate([c, c_ctx[None, :], jnp.zeros((pad_rows, dm), F32)], axis=0)
    mod_all = _modulation(cond, mod_w, mod_b)

    x_lat, x_ctx = x, ctx
    for i in range(DEPTH):
        kind, j = i % N_MIXERS, i // N_MIXERS
        last = i == DEPTH - 1
        m_lat = mod_all[i, :bsz].reshape(bsz, N_MOD, dm)
        m_ctx = jnp.broadcast_to(mod_all[i, bsz].reshape(1, N_MOD, dm), (bsz, N_MOD, dm))

        if kind == 0:
            w = a_w_qkv[j].astype(BF)
            gains = (a_q_norm[j], a_k_norm[j])
            nq, nk = A_HEADS * HEAD_DIM // LANES, A_KV_HEADS * HEAD_DIM // LANES
            p_lat = _proj_qkv(x_lat, m_lat, w, n_q=nq, n_k=nk, n_v=nk, q_scale=scale, norm_gains=gains, rope=True)
            p_ctx = _proj_qkv(x_ctx, m_ctx, w, n_q=nq, n_k=nk, n_v=nk, q_scale=scale, norm_gains=gains)
            a_lat = _attn_gqa(p_lat, [p_ctx, p_lat])
            a_ctx = None if last else _attn_gqa(p_ctx, [p_ctx])
            w_o = a_w_o[j]
        elif kind == 1:
            w = b_w_qkv[j].astype(BF)
            nb = 2 * B_HEADS * HEAD_DIM // LANES
            p_lat = _proj_qkv(x_lat, m_lat, w, n_q=nb, n_k=nb, n_v=nb, q_scale=scale, rope=True)
            p_ctx = _proj_qkv(x_ctx, m_ctx, w, n_q=nb, n_k=nb, n_v=nb, q_scale=scale)
            lam_vecs = jnp.stack([b_lambda_q1[j], b_lambda_k1[j], b_lambda_q2[j], b_lambda_k2[j]])
            lam_init = 0.8 - 0.6 * math.exp(-0.3 * i)
            a_lat = _attn_diff(p_lat, [p_ctx, p_lat], lam_vecs, b_subln[j], lam_init)
            a_ctx = None if last else _attn_diff(p_ctx, [p_ctx], lam_vecs, b_subln[j], lam_init)
            w_o = b_w_o[j]
        elif kind == 2:
            w_in = c_w_in[j]
            n_main = w_in.shape[1] - 2 * C_GATE_RANK
            w_main = w_in[:, :n_main].astype(BF)
            w_z = jnp.pad(w_in[:, n_main:], ((0, 0), (0, LANES - 2 * C_GATE_RANK))).astype(BF)
            kd = c_w_gate_fwd.shape[-1]
            w_gate = jnp.zeros((LANES, 2 * kd), F32)
            w_gate = w_gate.at[:C_GATE_RANK, :kd].set(c_w_gate_fwd[j])
            w_gate = w_gate.at[C_GATE_RANK:2 * C_GATE_RANK, kd:].set(c_w_gate_bwd[j]).astype(BF)
            b_gate = jnp.concatenate([c_b_gate_fwd[j], c_b_gate_bwd[j]]).reshape(1, 2 * kd)
            q_scale = (kd // C_HEADS) ** -0.5
            nq = kd // LANES
            p_lat, g_lat = _proj_gla(x_lat, m_lat, w_main, w_z, w_gate, b_gate, q_scale=q_scale, n_q=nq)
            p_ctx, g_ctx = _proj_gla(x_ctx, m_ctx, w_main, w_z, w_gate, b_gate, q_scale=q_scale, n_q=nq)
            a_lat, a_ctx = _gla(p_lat, g_lat, p_ctx, g_ctx, c_norm[j], not last)
            w_o = c_w_o[j]
        else:
            w = d_w_qkv[j].astype(BF)
            nb = D_HEADS * HEAD_DIM // LANES
            p_lat = _proj_qkv(x_lat, m_lat, w, n_q=nb, n_k=nb, n_v=nb, q_scale=scale)
            p_ctx = _proj_qkv(x_ctx, m_ctx, w, n_q=nb, n_k=nb, n_v=nb, q_scale=scale)
            bias = _nb_bias_table(d_rpb[j], n_lat // GRID_W)
            a_lat = _attn_nb(p_lat, p_ctx, bias)
            a_ctx = None if last else _attn_ctx(p_ctx, nb)
            w_o = d_w_o[j]

        w_o = w_o.astype(BF)
        w_up = ffn_w_up[i].astype(BF)
        w_dn = ffn_w_down[i].astype(BF)
        ffn = (w_o, ln1_g[i], ln1_b[i], w_up, ffn_conv_w[i], ffn_conv_b[i], w_dn, ln2_g[i], ln2_b[i])
        x_lat = _mix_ffn(a_lat, x_lat, m_lat, *ffn)
        if not last:
            x_ctx = _mix_ffn(a_ctx, x_ctx, m_ctx, *ffn)
    return x_lat
```

```python
import functools
import math

import jax
import jax.numpy as jnp
import numpy as np
from jax import lax
from jax.experimental import pallas as pl
from jax.experimental.pallas import tpu as pltpu

F32 = jnp.float32
BF = jnp.bfloat16

DEPTH = 4
N_MIXERS = 4
N_MOD = 6
GRID_W = 64
ROPE_THETA = 10000.0
NORM_EPS = 1e-6
LN_EPS = 1e-5
HEAD_DIM = 64
A_HEADS, A_KV_HEADS = 16, 4
B_HEADS = 8
C_HEADS = 4
C_GATE_RANK = 16
C_GATE_NORMALIZER = 16.0
C_CHUNK = 64
C_SUB = 16
GLA_FINISH_ROWS = 256
GLA_GROUP = 4
D_HEADS = 16
WIN_R, WIN_C = 8, 16
FF_DIM = 2816
DEEPNORM_ALPHA = (2 * DEPTH) ** 0.25
MASK_VALUE = -1e30
LOG2E = math.log2(math.e)

LANES = 128
MXU_WIDTH = 256
VMEM_LIMIT_BYTES = 56 * 1024 * 1024


def _cparams(n_axes):
    return pltpu.CompilerParams(
        dimension_semantics=("arbitrary",) * n_axes,
        vmem_limit_bytes=VMEM_LIMIT_BYTES)


def _dot(a, b):
    return jnp.dot(a, b, preferred_element_type=F32)


def _dot_nt(a, b):
    return lax.dot_general(a, b, (((1,), (1,)), ((), ())), preferred_element_type=F32)


def _dot_tn(a, b):
    return lax.dot_general(a, b, (((0,), (0,)), ((), ())), preferred_element_type=F32)


def _split_dot(x, m_bf16):
    hi = x.astype(BF)
    lo = (x - hi.astype(F32)).astype(BF)
    return _dot(hi, m_bf16) + _dot(lo, m_bf16)


def _split_dot_left(m_bf16, x):
    hi = x.astype(BF)
    lo = (x - hi.astype(F32)).astype(BF)
    return _dot(m_bf16, hi) + _dot(m_bf16, lo)


def _sigmoid(x):
    return 1.0 / (1.0 + jnp.exp(-x))


def _modulate(x_ref, mod_ref, shift_row, scale_row):
    x = x_ref[...]
    return x * (1.0 + mod_ref[scale_row:scale_row + 1, :]) + mod_ref[shift_row:shift_row + 1, :]


def _layer_norm(z, g, b):
    mu = jnp.mean(z, axis=-1, keepdims=True)
    zc = z - mu
    var = jnp.mean(zc * zc, axis=-1, keepdims=True)
    return zc * lax.rsqrt(var + LN_EPS) * g + b


def _rope(y, cos_t, sin_t, lo_half):
    swap = jnp.where(lo_half, pltpu.roll(y, LANES - 32, axis=1), pltpu.roll(y, 32, axis=1))
    return y * cos_t + swap * sin_t


def _idiv(x, n):
    return lax.shift_right_logical(x, int(n).bit_length() - 1)


def _imod(x, n):
    return x & (n - 1)


def _lane_iota(shape=(1, LANES)):
    return lax.broadcasted_iota(jnp.int32, shape, len(shape) - 1)


def _mod_kernel(cond_ref, w_ref, b_ref, o_ref):
    cnd = cond_ref[...]
    act = cnd * _sigmoid(cnd)
    o_ref[...] = _dot(act.astype(BF), w_ref[...].astype(BF)) + b_ref[...]


def _modulation(cond, mod_w, mod_b):
    depth, dm, n = mod_w.shape
    rows = cond.shape[0]
    tn = 1536
    return pl.pallas_call(
        _mod_kernel,
        out_shape=jax.ShapeDtypeStruct((depth, rows, n), F32),
        grid=(depth, n // tn),
        in_specs=[
            pl.BlockSpec((rows, dm), lambda i, j: (0, 0)),
            pl.BlockSpec((None, dm, tn), lambda i, j: (i, 0, j)),
            pl.BlockSpec((None, 1, tn), lambda i, j: (i, 0, j)),
        ],
        out_specs=pl.BlockSpec((None, rows, tn), lambda i, j: (i, 0, j)),
        compiler_params=_cparams(2),
        name="adaln_modulation",
    )(cond, mod_w, mod_b.reshape(depth, 1, n))


def _row_tile(n_tok):
    return 512 if n_tok % 512 == 0 else n_tok


def _proj_call(kernel, x, mod, w, extras, extra_specs, out_shapes, out_specs, name):
    bsz, n_tok, dm = x.shape
    tm = _row_tile(n_tok)
    in_specs = [
        pl.BlockSpec((None, tm, dm), lambda b, i: (b, i, 0)),
        pl.BlockSpec((None, N_MOD, dm), lambda b, i: (b, 0, 0)),
        pl.BlockSpec(w.shape, lambda b, i: (0, 0)),
    ] + list(extra_specs(tm))
    return pl.pallas_call(
        kernel,
        out_shape=out_shapes,
        grid=(bsz, n_tok // tm),
        in_specs=in_specs,
        out_specs=out_specs(tm),
        compiler_params=_cparams(2),
        name=name,
    )(x, mod, w, *extras)


def _full_spec(arr):
    nd = arr.ndim
    return pl.BlockSpec(arr.shape, lambda b, i: (0,) * nd)


def _seg_matrix():
    idx = np.arange(LANES) // HEAD_DIM
    return jnp.asarray((idx[:, None] == idx[None, :]).astype(np.float32) / HEAD_DIM, dtype=BF)


def _rope_tables(n_tok):
    t = jnp.arange(n_tok)
    row = (t // GRID_W).astype(F32)
    col = (t % GRID_W).astype(F32)
    n_freq = HEAD_DIM // 4
    inv = 1.0 / (ROPE_THETA ** (jnp.arange(n_freq, dtype=F32) / n_freq))
    ang = jnp.concatenate([row[:, None] * inv, col[:, None] * inv], axis=-1)
    cos, sin = jnp.cos(ang), jnp.sin(ang)
    cos_t = jnp.concatenate([cos, cos, cos, cos], axis=-1)
    sin_t = jnp.concatenate([-sin, sin, -sin, sin], axis=-1)
    return cos_t, sin_t


def _proj_qkv_kernel(x_ref, mod_ref, w_ref, *rest, n_q, n_k, n_v, norm, rope, q_scale):
    rest = list(rest)
    seg_ref = gq_ref = gk_ref = cos_ref = sin_ref = None
    if norm:
        seg_ref, gq_ref, gk_ref = rest[:3]
        rest = rest[3:]
    if rope:
        cos_ref, sin_ref = rest[:2]
        rest = rest[2:]
    (o_ref,) = rest
    hb = _modulate(x_ref, mod_ref, 0, 1).astype(BF)
    lo_half = _imod(_lane_iota(), HEAD_DIM) < (HEAD_DIM // 2)
    n_blocks = n_q + n_k + n_v
    per_dot = MXU_WIDTH // LANES
    ys = []
    for c0 in range(0, n_blocks, per_dot):
        nb = min(per_dot, n_blocks - c0)
        y2 = _dot(hb, w_ref[:, c0 * LANES:(c0 + nb) * LANES])
        ys += [y2[:, j * LANES:(j + 1) * LANES] for j in range(nb)]
    mean_sq = [_split_dot(y * y, seg_ref[...]) if norm and c < n_q + n_k else None for c, y in enumerate(ys)]
    for c, y in enumerate(ys):
        if c < n_q + n_k:
            if norm:
                gain = gq_ref[...] if c < n_q else gk_ref[...]
                y = y * lax.rsqrt(mean_sq[c] + NORM_EPS) * gain
            if rope:
                y = _rope(y, cos_ref[...], sin_ref[...], lo_half)
            if c < n_q and q_scale != 1.0:
                y = y * q_scale
        o_ref[:, c * LANES:(c + 1) * LANES] = y.astype(BF)


def _proj_qkv(x, mod, w, *, n_q, n_k, n_v, q_scale, norm_gains=None, rope=False):
    bsz, n_tok, _ = x.shape
    n_out = w.shape[1]
    extras, spec_fns = [], []
    if norm_gains is not None:
        gq, gk = norm_gains
        extras += [_seg_matrix(), jnp.tile(gq, 2).reshape(1, LANES), jnp.tile(gk, 2).reshape(1, LANES)]
        spec_fns += [lambda tm, a=a: _full_spec(a) for a in extras]
    if rope:
        cos_t, sin_t = _rope_tables(n_tok)
        extras += [cos_t, sin_t]
        spec_fns += [lambda tm: pl.BlockSpec((tm, LANES), lambda b, i: (i, 0))] * 2
    kern = functools.partial(_proj_qkv_kernel, n_q=n_q, n_k=n_k, n_v=n_v,
                             norm=norm_gains is not None, rope=rope, q_scale=q_scale)
    return _proj_call(
        kern, x, mod, w, extras, lambda tm: [f(tm) for f in spec_fns],
        jax.ShapeDtypeStruct((bsz, n_tok, n_out), BF),
        lambda tm: pl.BlockSpec((None, tm, n_out), lambda b, i: (b, i, 0)),
        "mixer_qkv_projection")


def _log_sigmoid(x):
    return jnp.minimum(x, 0.0) - jnp.log(1.0 + jnp.exp(-jnp.abs(x)))


def _proj_gla_kernel(x_ref, mod_ref, w_ref, wz_ref, wg_ref, bg_ref, o_ref, g_ref, *, q_scale, n_q):
    hb = _modulate(x_ref, mod_ref, 0, 1).astype(BF)
    n_blocks = o_ref.shape[-1] // LANES
    per_dot = MXU_WIDTH // LANES
    for c0 in range(0, n_blocks, per_dot):
        y2 = _dot(hb, w_ref[:, c0 * LANES:(c0 + per_dot) * LANES])
        if c0 < n_q:
            y2 = y2 * q_scale
        o_ref[:, c0 * LANES:(c0 + per_dot) * LANES] = y2.astype(BF)
    z = _dot(hb, wz_ref[...]).astype(BF)
    n_gate = g_ref.shape[-1]
    for c0 in range(0, n_gate, MXU_WIDTH):
        pre = _dot(z, wg_ref[:, c0:c0 + MXU_WIDTH]) + bg_ref[:, c0:c0 + MXU_WIDTH]
        g_ref[:, c0:c0 + MXU_WIDTH] = _log_sigmoid(pre) * (1.0 / C_GATE_NORMALIZER)


def _proj_gla(x, mod, w_main, w_z, w_gate, b_gate, *, q_scale, n_q):
    bsz, n_tok, _ = x.shape
    n_out, n_gate = w_main.shape[1], w_gate.shape[1]
    extras = [w_z, w_gate, b_gate]
    kern = functools.partial(_proj_gla_kernel, q_scale=q_scale, n_q=n_q)
    return _proj_call(
        kern, x, mod, w_main, extras, lambda tm: [_full_spec(a) for a in extras],
        (jax.ShapeDtypeStruct((bsz, n_tok, n_out), BF), jax.ShapeDtypeStruct((bsz, n_tok, n_gate), F32)),
        lambda tm: (pl.BlockSpec((None, tm, n_out), lambda b, i: (b, i, 0)),
                    pl.BlockSpec((None, tm, n_gate), lambda b, i: (b, i, 0))),
        "gla_projection")


FFN_CHUNK = 256
FFN_TILE = 512
HALO = 16


def _mix_ffn_kernel(a_ref, ap_ref, an_ref, x_ref, xp_ref, xn_ref, mod_ref, wo_ref, g1_ref, b1_ref,
                    wu_ref, cw_ref, cb_ref, wd_ref, g2_ref, b2_ref, o_ref, act_ref):
    tm = x_ref.shape[0]
    i = pl.program_id(1)
    has_prev = (i > 0).astype(F32)
    has_next = (i < pl.num_programs(1) - 1).astype(F32)
    n_ext = tm + 2 * HALO

    a_ext = jnp.concatenate([a_ref[...], an_ref[...], ap_ref[...]], axis=0)
    x_ext = jnp.concatenate([x_ref[...], xn_ref[...], xp_ref[...]], axis=0)
    z1 = DEEPNORM_ALPHA * x_ext + mod_ref[2:3, :] * _dot(a_ext, wo_ref[...])
    x_mid = _layer_norm(z1, g1_ref[...], b1_ref[...])
    h = x_mid * (1.0 + mod_ref[4:5, :]) + mod_ref[3:4, :]
    hb = jnp.concatenate([h[:tm], h[tm:tm + HALO] * has_next, h[tm + HALO:] * has_prev], axis=0).astype(BF)

    def up(c):
        cg = c * FFN_CHUNK
        return _dot(hb, wu_ref[:, cg:cg + FFN_CHUNK]), _dot(hb, wu_ref[:, FF_DIM + cg:FF_DIM + cg + FFN_CHUNK])

    def conv(u_ext, c0):
        u_dn = pltpu.roll(u_ext, 1, axis=0)[:tm]
        u_up = pltpu.roll(u_ext, n_ext - 1, axis=0)[:tm]
        w0 = cw_ref[0:1, c0:c0 + FFN_CHUNK]
        w1 = cw_ref[1:2, c0:c0 + FFN_CHUNK]
        w2 = cw_ref[2:3, c0:c0 + FFN_CHUNK]
        return cb_ref[:, c0:c0 + FFN_CHUNK] + u_dn * w0 + u_ext[:tm] * w1 + u_up * w2

    for c in range(FF_DIM // FFN_CHUNK):
        cg = c * FFN_CHUNK
        u_gate, u_val = up(c)
        gate = conv(u_gate, cg)
        val = conv(u_val, FF_DIM + cg)
        act_ref[:, cg:cg + FFN_CHUNK] = (gate * _sigmoid(gate) * val).astype(BF)
    z2 = DEEPNORM_ALPHA * x_mid[:tm] + mod_ref[5:6, :] * _dot(act_ref[...], wd_ref[...])
    o_ref[...] = _layer_norm(z2, g2_ref[...], b2_ref[...])


def _mix_ffn(a, x, mod, w_o, ln1_g, ln1_b, w_up, conv_w, conv_b, w_down, ln2_g, ln2_b):
    bsz, n_tok, dm = x.shape
    k_in = a.shape[-1]
    tm = FFN_TILE if n_tok % FFN_TILE == 0 else n_tok
    n_halo = n_tok // HALO
    per = tm // HALO
    tile = lambda b, i: (b, i, 0)
    prev = lambda b, i: (b, jnp.maximum(i * per - 1, 0), 0)
    nxt = lambda b, i: (b, jnp.minimum((i + 1) * per, n_halo - 1), 0)
    const2 = lambda b, i: (0, 0)
    resident = functools.partial(pl.BlockSpec, index_map=const2, pipeline_mode=pl.Buffered(1))
    row = lambda v: v.reshape(1, -1)
    return pl.pallas_call(
        _mix_ffn_kernel,
        out_shape=jax.ShapeDtypeStruct((bsz, n_tok, dm), F32),
        grid=(bsz, n_tok // tm),
        in_specs=[
            pl.BlockSpec((None, tm, k_in), tile),
            pl.BlockSpec((None, HALO, k_in), prev),
            pl.BlockSpec((None, HALO, k_in), nxt),
            pl.BlockSpec((None, tm, dm), tile),
            pl.BlockSpec((None, HALO, dm), prev),
            pl.BlockSpec((None, HALO, dm), nxt),
            pl.BlockSpec((None, N_MOD, dm), lambda b, i: (b, 0, 0)),
            resident(w_o.shape),
            pl.BlockSpec((1, dm), const2),
            pl.BlockSpec((1, dm), const2),
            resident(w_up.shape),
            pl.BlockSpec(conv_w.shape, const2),
            pl.BlockSpec((1, 2 * FF_DIM), const2),
            resident(w_down.shape),
            pl.BlockSpec((1, dm), const2),
            pl.BlockSpec((1, dm), const2),
        ],
        out_specs=pl.BlockSpec((None, tm, dm), tile),
        scratch_shapes=[pltpu.VMEM((tm, FF_DIM), BF)],
        compiler_params=_cparams(2),
        name="outproj_convffn_layernorm",
    )(a, a, a, x, x, x, mod, w_o, row(ln1_g), row(ln1_b), w_up, conv_w, row(conv_b), w_down,
      row(ln2_g), row(ln2_b))


ATT_SUB = 512
ATT_TILE = 2048


def _attn_tile(n_tok):
    return ATT_TILE if n_tok % ATT_TILE == 0 else n_tok


def _transpose_values(v_ref):
    return v_ref[...].astype(F32).T.astype(BF)


def _scores_t(q, keys, biases_t):
    return [_dot_nt(k, q) if bias is None else _dot_nt(k, q) + bias for k, bias in zip(keys, biases_t)]


def _probs_t(scores):
    m = functools.reduce(jnp.maximum, [jnp.max(s, axis=0, keepdims=True) for s in scores])
    probs = [jnp.exp2(s - m) for s in scores]
    denom = functools.reduce(lambda a, b: a + b, [jnp.sum(p, axis=0, keepdims=True) for p in probs])
    return [p.astype(BF) for p in probs], denom


def _pv_t(probs, denom, values_t):
    out_t = functools.reduce(lambda a, b: a + b, [_dot(vt, p) for p, vt in zip(probs, values_t)])
    return (out_t / denom).T


def _softmax_pv_t(scores, values_t):
    return _pv_t(*_probs_t(scores), values_t)


def _attend_pipelined(items, emit):
    n = len(items)
    scores = {j: _scores_t(*items[j][:3]) for j in range(min(2, n))}
    probs = {0: _probs_t(scores.pop(0))}
    for i in range(n):
        if i + 2 < n:
            scores[i + 2] = _scores_t(*items[i + 2][:3])
        if i + 1 < n:
            probs[i + 1] = _probs_t(scores.pop(i + 1))
        emit(i, _pv_t(*probs.pop(i), items[i][3]))


def _attn_gqa_kernel(q_ref, *rest, n_seg, n_blk):
    k_refs, v_refs, o_ref = rest[:n_seg], rest[n_seg:2 * n_seg], rest[2 * n_seg]
    grp = A_HEADS // A_KV_HEADS
    half = _idiv(_lane_iota(), HEAD_DIM)
    values_t = [_transpose_values(r) for r in v_refs]
    items, place = [], []
    for blk in range(n_blk):
        head = (pl.program_id(1) if n_blk == 1 else blk) * 2
        kv_pos = (head // grp) % 2
        kv_blk = 0 if n_blk == 1 else blk * 2 // grp // 2
        keys = [r[:, kv_blk * LANES:(kv_blk + 1) * LANES] for r in k_refs]
        values = [vt[kv_blk * LANES:(kv_blk + 1) * LANES, :] for vt in values_t]
        for r0 in range(0, q_ref.shape[0], ATT_SUB):
            q = q_ref[r0:r0 + ATT_SUB, blk * LANES:(blk + 1) * LANES].astype(F32)
            q_sw = pltpu.roll(q, HEAD_DIM, axis=1)
            for e in range(2):
                q_e = jnp.where(kv_pos == e, q, q_sw)
                q_e = jnp.where(half == kv_pos, q_e, 0.0).astype(BF)
                items.append((q_e, keys, [None] * n_seg, values))
                place.append((blk, r0, e, kv_pos))
    outs = {}

    def emit(i, o):
        blk, r0, e, kv_pos = place[i]
        outs[e] = jnp.where(kv_pos == e, o, pltpu.roll(o, HEAD_DIM, axis=1))
        if e == 1:
            o_ref[r0:r0 + ATT_SUB, blk * LANES:(blk + 1) * LANES] = (
                jnp.where(half == 0, outs[0], outs[1]).astype(BF))

    _attend_pipelined(items, emit)


def _blocks_per_step(n_q_tok, n_blocks):
    return n_blocks if n_q_tok < ATT_SUB else 1


def _attn_gqa(q_src, kv_srcs):
    bsz, n_q_tok, _ = q_src.shape
    tq = _attn_tile(n_q_tok)
    n_qb = A_HEADS * HEAD_DIM // LANES
    n_kb = A_KV_HEADS * HEAD_DIM // LANES
    per_kb = n_qb // n_kb
    n_seg = len(kv_srcs)
    n_blk = _blocks_per_step(n_q_tok, n_qb)
    kv_w = LANES if n_blk == 1 else n_kb * LANES
    k0, v0 = n_qb * LANES // kv_w, (n_qb + n_kb) * LANES // kv_w
    k_specs = [pl.BlockSpec((None, s.shape[1], kv_w), lambda b, p, i: (b, 0, k0 + p * n_blk // per_kb))
               for s in kv_srcs]
    v_specs = [pl.BlockSpec((None, s.shape[1], kv_w), lambda b, p, i: (b, 0, v0 + p * n_blk // per_kb))
               for s in kv_srcs]
    return pl.pallas_call(
        functools.partial(_attn_gqa_kernel, n_seg=n_seg, n_blk=n_blk),
        out_shape=jax.ShapeDtypeStruct((bsz, n_q_tok, n_qb * LANES), BF),
        grid=(bsz, n_qb // n_blk, n_q_tok // tq),
        in_specs=[pl.BlockSpec((None, tq, n_blk * LANES), lambda b, p, i: (b, i, p))] + k_specs + v_specs,
        out_specs=pl.BlockSpec((None, tq, n_blk * LANES), lambda b, p, i: (b, i, p)),
        compiler_params=_cparams(3),
        name="gqa_attention",
    )(q_src, *kv_srcs, *kv_srcs)


def _attn_diff_kernel(q_ref, lam_ref, subln_ref, *rest, n_seg, lam_init):
    k_refs, v_refs, o_ref = rest[:n_seg], rest[n_seg:2 * n_seg], rest[2 * n_seg]
    lam_v = lam_ref[...]
    lam = (jnp.exp(jnp.sum(lam_v[0:1] * lam_v[1:2], axis=-1, keepdims=True))
           - jnp.exp(jnp.sum(lam_v[2:3] * lam_v[3:4], axis=-1, keepdims=True)) + lam_init)
    half = _idiv(_lane_iota(), HEAD_DIM)
    values_t = [_transpose_values(r) for r in v_refs]
    n_blk = q_ref.shape[1] // LANES
    items, place = [], []
    for blk in range(n_blk):
        lanes = slice(blk * LANES, (blk + 1) * LANES)
        keys = [r[:, lanes] for r in k_refs]
        values = [vt[lanes, :] for vt in values_t]
        for r0 in range(0, q_ref.shape[0], ATT_SUB):
            q = q_ref[r0:r0 + ATT_SUB, lanes]
            for e in range(2):
                items.append((jnp.where(half == e, q, jnp.zeros_like(q)), keys, [None] * n_seg, values))
                place.append((lanes, r0, e))
    outs = {}

    def emit(i, o):
        lanes, r0, e = place[i]
        outs[e] = o
        if e == 1:
            o = outs[0] - lam * outs[1]
            ms = jnp.mean(o * o, axis=-1, keepdims=True)
            o = o * lax.rsqrt(ms + NORM_EPS) * subln_ref[...] * (1.0 - lam_init)
            o_ref[r0:r0 + ATT_SUB, lanes] = o.astype(BF)

    _attend_pipelined(items, emit)


def _attn_diff(q_src, kv_srcs, lam_vecs, subln, lam_init):
    bsz, n_q_tok, _ = q_src.shape
    tq = _attn_tile(n_q_tok)
    nh = B_HEADS
    n_seg = len(kv_srcs)
    n_blk = _blocks_per_step(n_q_tok, nh)
    w = n_blk * LANES
    steps = nh // n_blk
    k_specs = [pl.BlockSpec((None, s.shape[1], w), lambda b, h, i: (b, 0, steps + h)) for s in kv_srcs]
    v_specs = [pl.BlockSpec((None, s.shape[1], w), lambda b, h, i: (b, 0, 2 * steps + h)) for s in kv_srcs]
    return pl.pallas_call(
        functools.partial(_attn_diff_kernel, n_seg=n_seg, lam_init=lam_init),
        out_shape=jax.ShapeDtypeStruct((bsz, n_q_tok, nh * LANES), BF),
        grid=(bsz, steps, n_q_tok // tq),
        in_specs=[pl.BlockSpec((None, tq, w), lambda b, h, i: (b, i, h)),
                  pl.BlockSpec(lam_vecs.shape, lambda b, h, i: (0, 0)),
                  pl.BlockSpec((1, LANES), lambda b, h, i: (0, 0))] + k_specs + v_specs,
        out_specs=pl.BlockSpec((None, tq, w), lambda b, h, i: (b, i, h)),
        compiler_params=_cparams(3),
        name="differential_attention",
    )(q_src, lam_vecs, subln.reshape(1, LANES), *kv_srcs, *kv_srcs)


NB_QROWS = 4
NB_BAND = 12


def _nb_band_start(g, rows):
    return int(np.clip(NB_QROWS * g - NB_QROWS, 0, rows - NB_BAND))


def _nb_config(g, rows):
    n_g = rows // NB_QROWS
    return 0 if g == 0 else (2 if g == n_g - 1 else 1)


def _nb_bias_table(rpb, rows):
    nh = rpb.shape[0]
    wr = min(WIN_R, rows)
    cq = np.arange(GRID_W)
    c0 = np.clip(cq - WIN_C // 2, 0, GRID_W - WIN_C)
    col_in = (cq[None, :] >= c0[:, None]) & (cq[None, :] < c0[:, None] + WIN_C)
    dc_idx = np.clip(cq[None, :] - cq[:, None], -(WIN_C - 1), WIN_C - 1) + WIN_C - 1
    n_dr = 2 * WIN_R - 1
    tiles = rpb[:, :, dc_idx]
    tiles = jnp.where(col_in[None, None], tiles * LOG2E, MASK_VALUE)
    tiles = jnp.concatenate([tiles, jnp.full((nh, 1, GRID_W, GRID_W), MASK_VALUE, F32)], axis=1)
    n_g = rows // NB_QROWS
    sel = np.full((3, NB_QROWS, NB_BAND), n_dr, dtype=np.int32)
    for cfg, g in ((0, 0), (1, 1), (2, n_g - 1)):
        start = _nb_band_start(g, rows)
        for a in range(NB_QROWS):
            r = NB_QROWS * g + a
            r0 = int(np.clip(r - wr // 2, 0, rows - wr))
            for jb in range(NB_BAND):
                kr = start + jb
                if r0 <= kr < r0 + wr:
                    sel[cfg, a, jb] = kr - r + (WIN_R - 1)
    big = tiles[:, sel]
    big = big.transpose(0, 1, 3, 5, 2, 4)
    return big.reshape(nh, 3, NB_BAND * GRID_W, NB_QROWS * GRID_W)


def _attn_nb_kernel(q_ref, kl_ref, vl_ref, kc_ref, vc_ref, bias_ref, o_ref, *, rows):
    half = _idiv(_lane_iota(), HEAD_DIM)
    kc = kc_ref[...]
    vc = _transpose_values(vc_ref)
    vl = _transpose_values(vl_ref)
    tq = NB_QROWS * GRID_W
    tk = NB_BAND * GRID_W
    items = []
    for g in range(rows // NB_QROWS):
        ks = _nb_band_start(g, rows) * GRID_W
        cfg = _nb_config(g, rows)
        q = q_ref[g * tq:(g + 1) * tq, :]
        kb = kl_ref[ks:ks + tk, :]
        vb = vl[:, ks:ks + tk]
        for e in range(2):
            q_e = jnp.where(half == e, q, jnp.zeros_like(q))
            items.append((q_e, [kb, kc], [bias_ref[e, cfg], None], [vb, vc]))
    outs = {}

    def emit(i, o):
        g, e = i // 2, i % 2
        outs[e] = o
        if e == 1:
            o_ref[g * tq:(g + 1) * tq, :] = jnp.where(half == 0, outs[0], outs[1]).astype(BF)

    _attend_pipelined(items, emit)


def _attn_nb(qkv_lat, qkv_ctx, bias):
    bsz, n_tok, _ = qkv_lat.shape
    n_ctx = qkv_ctx.shape[1]
    rows = n_tok // GRID_W
    npair = D_HEADS * HEAD_DIM // LANES
    return pl.pallas_call(
        functools.partial(_attn_nb_kernel, rows=rows),
        out_shape=jax.ShapeDtypeStruct((bsz, n_tok, npair * LANES), BF),
        grid=(npair, bsz),
        in_specs=[
            pl.BlockSpec((None, n_tok, LANES), lambda p, b: (b, 0, p)),
            pl.BlockSpec((None, n_tok, LANES), lambda p, b: (b, 0, npair + p)),
            pl.BlockSpec((None, n_tok, LANES), lambda p, b: (b, 0, 2 * npair + p)),
            pl.BlockSpec((None, n_ctx, LANES), lambda p, b: (b, 0, npair + p)),
            pl.BlockSpec((None, n_ctx, LANES), lambda p, b: (b, 0, 2 * npair + p)),
            pl.BlockSpec((2,) + bias.shape[1:], lambda p, b: (p, 0, 0, 0)),
        ],
        out_specs=pl.BlockSpec((None, n_tok, LANES), lambda p, b: (b, 0, p)),
        compiler_params=_cparams(2),
        name="neighbourhood_attention",
    )(qkv_lat, qkv_lat, qkv_lat, qkv_ctx, qkv_ctx, bias)


def _attn_ctx_kernel(q_ref, k_ref, v_ref, o_ref):
    half = _idiv(_lane_iota(), HEAD_DIM)
    q = q_ref[...]
    zero = jnp.zeros_like(q)
    keys, values = [k_ref[...]], [_transpose_values(v_ref)]
    outs = [_softmax_pv_t(_scores_t(jnp.where(half == e, q, zero), keys, [None]), values) for e in range(2)]
    o_ref[...] = jnp.where(half == 0, outs[0], outs[1]).astype(BF)


def _attn_ctx(qkv_ctx, npair):
    bsz, n_ctx, _ = qkv_ctx.shape
    return pl.pallas_call(
        _attn_ctx_kernel,
        out_shape=jax.ShapeDtypeStruct((bsz, n_ctx, npair * LANES), BF),
        grid=(bsz, npair),
        in_specs=[
            pl.BlockSpec((None, n_ctx, LANES), lambda b, p: (b, 0, p)),
            pl.BlockSpec((None, n_ctx, LANES), lambda b, p: (b, 0, npair + p)),
            pl.BlockSpec((None, n_ctx, LANES), lambda b, p: (b, 0, 2 * npair + p)),
        ],
        out_specs=pl.BlockSpec((None, n_ctx, LANES), lambda b, p: (b, 0, p)),
        compiler_params=_cparams(2),
        name="context_attention",
    )(qkv_ctx, qkv_ctx, qkv_ctx)


def _gla_geometry(rev):
    row = lax.broadcasted_iota(jnp.int32, (C_CHUNK, 1), 0)
    col = lax.broadcasted_iota(jnp.int32, (1, C_CHUNK), 1)
    sub = _idiv(row, C_SUB)
    col_sub = _idiv(col, C_SUB)
    if rev:
        tri, cross, dist = col >= row, col_sub > sub, col - row
    else:
        tri, cross, dist = col <= row, col_sub < sub, row - col
    off_blk = jnp.where(cross, sub, -1)
    dmat = jnp.where(tri, jnp.where(col_sub == sub, dist, -1), -1)
    return tri.astype(BF), sub, off_blk, dmat


def _gla_decay(g, geom):
    return _split_dot_left(geom[0], g) * LOG2E


def _gla_factors(q, k, b, rev, geom):
    sub = geom[1]
    nsub = C_CHUNK // C_SUB
    b_tot = b[0:1] if rev else b[C_CHUNK - 1:C_CHUNK]
    blocks = range(nsub - 1) if rev else range(1, nsub)
    bounds = {}
    ref_rows = jnp.zeros_like(b)
    for a in blocks:
        r = (a + 1) * C_SUB if rev else a * C_SUB - 1
        bounds[a] = b[r:r + 1]
        ref_rows = jnp.where(sub == a, bounds[a], ref_rows)
    q_in = (q * jnp.exp2(b)).astype(BF)
    q_t = (q * jnp.exp2(b - ref_rows)).astype(BF)
    k_ts = {a: (k * jnp.exp2(jnp.minimum(bounds[a] - b, 0.0))).astype(BF) for a in blocks}
    k_end = (k * jnp.exp2(b_tot - b)).astype(BF)
    return q_in, q_t, k_ts, k_end, jnp.exp2(b_tot)


def _gla_cross(factors, v, geom):
    _, q_t, k_ts, k_end, _ = factors
    a_mat = jnp.zeros((C_CHUNK, C_CHUNK), F32)
    for a, k_t in k_ts.items():
        a_mat = jnp.where(geom[2] == a, _dot_nt(q_t, k_t), a_mat)
    return a_mat, _dot_tn(v, k_end)


def _gla_same(q, k, b, a_mat, rev, geom):
    dmat = geom[3]
    for d in range(C_SUB):
        shift = (C_CHUNK - d) % C_CHUNK if rev else d
        k_d = k if d == 0 else pltpu.roll(k, shift, axis=0)
        b_d = b if d == 0 else pltpu.roll(b, shift, axis=0)
        term = q * k_d * jnp.exp2(b - b_d)
        a_mat = jnp.where(dmat == d, jnp.sum(term, axis=-1, keepdims=True), a_mat)
    return a_mat


def _gla_advance(factors, a_mat, v, st, increment):
    q_in, _, _, _, decay = factors
    o = _dot_nt(q_in, st.astype(BF)) + _dot(a_mat.astype(BF), v)
    return o, st * decay + increment


def _gla_kernel(ql_ref, kl_ref, vl_ref, ogl_ref, gfl_ref, gbl_ref,
                qc_ref, kc_ref, vc_ref, ogc_ref, gfc_ref, gbc_ref, ng_ref,
                ol_ref, oc_ref, fl_ref, bl_ref, fc_ref, bc_ref, *, with_ctx_out):
    dk = ql_ref.shape[-1]
    dv = vl_ref.shape[-1]
    n_lat = ql_ref.shape[0] // C_CHUNK
    n_ctx = qc_ref.shape[0] // C_CHUNK
    geom_f = _gla_geometry(False)
    geom_b = _gla_geometry(True)

    def load(q_ref, k_ref, v_ref, g_ref, c):
        sl = pl.ds(pl.multiple_of(c * C_CHUNK, C_CHUNK), C_CHUNK)
        return q_ref[sl, :].astype(F32), k_ref[sl, :].astype(F32), v_ref[sl, :], g_ref[sl, :]

    def chunk_at(c):
        return pl.ds(pl.multiple_of(c * C_CHUNK, C_CHUNK), C_CHUNK)

    def scan_both(refs_f, refs_b, f_ref, b_ref, n_chunks, states):
        def body(step, carry):
            st_f, st_b = carry
            work = []
            for u in range(GLA_GROUP):
                j = step * GLA_GROUP + u
                work.append((False, geom_f, refs_f, j, f_ref))
                work.append((True, geom_b, refs_b, n_chunks - 1 - j, b_ref))
            data = [load(*refs, c) for _, _, refs, c, _ in work]
            decays = [_gla_decay(d[3], w[1]) for d, w in zip(data, work)]
            factors = [_gla_factors(d[0], d[1], b, w[0], w[1]) for d, b, w in zip(data, decays, work)]
            cross = [_gla_cross(f, d[2], w[1]) for f, d, w in zip(factors, data, work)]
            a_mats = [_gla_same(d[0], d[1], b, x[0], w[0], w[1])
                      for d, b, x, w in zip(data, decays, cross, work)]
            for idx, (rev, _, _, c, out_ref) in enumerate(work):
                st = st_b if rev else st_f
                o, st = _gla_advance(factors[idx], a_mats[idx], data[idx][2], st, cross[idx][1])
                out_ref[chunk_at(c), :] = o
                if rev:
                    st_b = st
                else:
                    st_f = st
            return st_f, st_b
        return lax.fori_loop(0, n_chunks // GLA_GROUP, body, states)

    def finish(f_ref, b_ref, og_ref, out_ref):
        def body(i, carry):
            sl = pl.ds(pl.multiple_of(i * GLA_FINISH_ROWS, GLA_FINISH_ROWS), GLA_FINISH_ROWS)
            tot = f_ref[sl, :] + b_ref[sl, :]
            ms = jnp.mean(tot * tot, axis=-1, keepdims=True)
            y = tot * lax.rsqrt(ms + NORM_EPS) * ng_ref[...]
            og = og_ref[sl, :].astype(F32)
            out_ref[sl, :] = (y * (og * _sigmoid(og))).astype(BF)
            return carry
        lax.fori_loop(0, f_ref.shape[0] // GLA_FINISH_ROWS, body, 0)

    zero = jnp.zeros((dv, dk), F32)
    lat_f = (ql_ref, kl_ref, vl_ref, gfl_ref)
    lat_b = (ql_ref, kl_ref, vl_ref, gbl_ref)
    ctx_f = (qc_ref, kc_ref, vc_ref, gfc_ref)
    ctx_b = (qc_ref, kc_ref, vc_ref, gbc_ref)
    states = scan_both(ctx_f, ctx_b, fc_ref, bc_ref, n_ctx, (zero, zero))
    scan_both(lat_f, lat_b, fl_ref, bl_ref, n_lat, states)
    finish(fl_ref, bl_ref, ogl_ref, ol_ref)
    if with_ctx_out:
        finish(fc_ref, bc_ref, ogc_ref, oc_ref)
    else:
        oc_ref[...] = jnp.zeros_like(oc_ref)


def _gla(proj_lat, gates_lat, proj_ctx, gates_ctx, norm_g, with_ctx_out):
    bsz, n_lat, _ = proj_lat.shape
    n_ctx = proj_ctx.shape[1]
    nh = C_HEADS
    dk = 512 // nh
    dv = 1024 // nh
    kb, vb = 512 // dk, 1024 // dv

    def specs(n_tok):
        return [
            pl.BlockSpec((None, n_tok, dk), lambda b, h: (b, 0, h)),
            pl.BlockSpec((None, n_tok, dk), lambda b, h: (b, 0, kb + h)),
            pl.BlockSpec((None, n_tok, dv), lambda b, h: (b, 0, (2 * kb * dk) // dv + h)),
            pl.BlockSpec((None, n_tok, dv), lambda b, h: (b, 0, (2 * kb * dk) // dv + vb + h)),
            pl.BlockSpec((None, n_tok, dk), lambda b, h: (b, 0, h)),
            pl.BlockSpec((None, n_tok, dk), lambda b, h: (b, 0, kb + h)),
        ]

    out_l, out_c = pl.pallas_call(
        functools.partial(_gla_kernel, with_ctx_out=with_ctx_out),
        out_shape=(jax.ShapeDtypeStruct((bsz, n_lat, nh * dv), BF),
                   jax.ShapeDtypeStruct((bsz, n_ctx, nh * dv), BF)),
        grid=(bsz, nh),
        in_specs=specs(n_lat) + specs(n_ctx) + [pl.BlockSpec((1, dv), lambda b, h: (0, 0))],
        out_specs=(pl.BlockSpec((None, n_lat, dv), lambda b, h: (b, 0, h)),
                   pl.BlockSpec((None, n_ctx, dv), lambda b, h: (b, 0, h))),
        scratch_shapes=[pltpu.VMEM((n_lat, dv), F32), pltpu.VMEM((n_lat, dv), F32),
                        pltpu.VMEM((n_ctx, dv), F32), pltpu.VMEM((n_ctx, dv), F32)],
        compiler_params=_cparams(2),
        name="gated_linear_attention",
    )(proj_lat, proj_lat, proj_lat, proj_lat, gates_lat, gates_lat,
      proj_ctx, proj_ctx, proj_ctx, proj_ctx, gates_ctx, gates_ctx, norm_g.reshape(1, dv))
    return out_l, out_c


def kernel(x, c, ctx, c_ctx, mod_w, mod_b, ln1_g, ln1_b, ffn_w_up, ffn_conv_w, ffn_conv_b, ffn_w_down, ln2_g, ln2_b, a_w_qkv, a_q_norm, a_k_norm, a_w_o, b_w_qkv, b_lambda_q1, b_lambda_k1, b_lambda_q2, b_lambda_k2, b_subln, b_w_o, c_w_in, c_w_gate_fwd, c_b_gate_fwd, c_w_gate_bwd, c_b_gate_bwd, c_norm, c_w_o, d_w_qkv, d_rpb, d_w_o):
    bsz, n_lat, dm = x.shape
    n_ctx = ctx.shape[1]
    scale = HEAD_DIM ** -0.5 * LOG2E

    pad_rows = (-(bsz + 1)) % 8
    cond = jnp.concatenate([c, c_ctx[None, :], jnp.zeros((pad_rows, dm), F32)], axis=0)
    mod_all = _modulation(cond, mod_w, mod_b)

    x_lat, x_ctx = x, ctx
    for i in range(DEPTH):
        kind, j = i % N_MIXERS, i // N_MIXERS
        last = i == DEPTH - 1
        m_lat = mod_all[i, :bsz].reshape(bsz, N_MOD, dm)
        m_ctx = jnp.broadcast_to(mod_all[i, bsz].reshape(1, N_MOD, dm), (bsz, N_MOD, dm))

        if kind == 0:
            w = a_w_qkv[j].astype(BF)
            gains = (a_q_norm[j], a_k_norm[j])
            nq, nk = A_HEADS * HEAD_DIM // LANES, A_KV_HEADS * HEAD_DIM // LANES
            p_lat = _proj_qkv(x_lat, m_lat, w, n_q=nq, n_k=nk, n_v=nk, q_scale=scale, norm_gains=gains, rope=True)
            p_ctx = _proj_qkv(x_ctx, m_ctx, w, n_q=nq, n_k=nk, n_v=nk, q_scale=scale, norm_gains=gains)
            a_lat = _attn_gqa(p_lat, [p_ctx, p_lat])
            a_ctx = None if last else _attn_gqa(p_ctx, [p_ctx])
            w_o = a_w_o[j]
        elif kind == 1:
            w = b_w_qkv[j].astype(BF)
            nb = 2 * B_HEADS * HEAD_DIM // LANES
            p_lat = _proj_qkv(x_lat, m_lat, w, n_q=nb, n_k=nb, n_v=nb, q_scale=scale, rope=True)
            p_ctx = _proj_qkv(x_ctx, m_ctx, w, n_q=nb, n_k=nb, n_v=nb, q_scale=scale)
            lam_vecs = jnp.stack([b_lambda_q1[j], b_lambda_k1[j], b_lambda_q2[j], b_lambda_k2[j]])
            lam_init = 0.8 - 0.6 * math.exp(-0.3 * i)
            a_lat = _attn_diff(p_lat, [p_ctx, p_lat], lam_vecs, b_subln[j], lam_init)
            a_ctx = None if last else _attn_diff(p_ctx, [p_ctx], lam_vecs, b_subln[j], lam_init)
            w_o = b_w_o[j]
        elif kind == 2:
            w_in = c_w_in[j]
            n_main = w_in.shape[1] - 2 * C_GATE_RANK
            w_main = w_in[:, :n_main].astype(BF)
            w_z = jnp.pad(w_in[:, n_main:], ((0, 0), (0, LANES - 2 * C_GATE_RANK))).astype(BF)
            kd = c_w_gate_fwd.shape[-1]
            w_gate = jnp.zeros((LANES, 2 * kd), F32)
            w_gate = w_gate.at[:C_GATE_RANK, :kd].set(c_w_gate_fwd[j])
            w_gate = w_gate.at[C_GATE_RANK:2 * C_GATE_RANK, kd:].set(c_w_gate_bwd[j]).astype(BF)
            b_gate = jnp.concatenate([c_b_gate_fwd[j], c_b_gate_bwd[j]]).reshape(1, 2 * kd)
            q_scale = (kd // C_HEADS) ** -0.5
            nq = kd // LANES
            p_lat, g_lat = _proj_gla(x_lat, m_lat, w_main, w_z, w_gate, b_gate, q_scale=q_scale, n_q=nq)
            p_ctx, g_ctx = _proj_gla(x_ctx, m_ctx, w_main, w_z, w_gate, b_gate, q_scale=q_scale, n_q=nq)
            a_lat, a_ctx = _gla(p_lat, g_lat, p_ctx, g_ctx, c_norm[j], not last)
            w_o = c_w_o[j]
        else:
            w = d_w_qkv[j].astype(BF)
            nb = D_HEADS * HEAD_DIM // LANES
            p_lat = _proj_qkv(x_lat, m_lat, w, n_q=nb, n_k=nb, n_v=nb, q_scale=scale)
            p_ctx = _proj_qkv(x_ctx, m_ctx, w, n_q=nb, n_k=nb, n_v=nb, q_scale=scale)
            bias = _nb_bias_table(d_rpb[j], n_lat // GRID_W)
            a_lat = _attn_nb(p_lat, p_ctx, bias)
            a_ctx = None if last else _attn_ctx(p_ctx, nb)
            w_o = d_w_o[j]

        w_o = w_o.astype(BF)
        w_up = ffn_w_up[i].astype(BF)
        w_dn = ffn_w_down[i].astype(BF)
        ffn = (w_o, ln1_g[i], ln1_b[i], w_up, ffn_conv_w[i], ffn_conv_b[i], w_dn, ln2_g[i], ln2_b[i])
        x_lat = _mix_ffn(a_lat, x_lat, m_lat, *ffn)
        if not last:
            x_ctx = _mix_ffn(a_ctx, x_ctx, m_ctx, *ffn)
    return x_lat
```

```python
import functools
import math

import jax
import jax.numpy as jnp
import numpy as np
from jax import lax
from jax.experimental import pallas as pl
from jax.experimental.pallas import tpu as pltpu

F32 = jnp.float32
BF = jnp.bfloat16

DEPTH = 4
N_MIXERS = 4
N_MOD = 6
GRID_W = 64
ROPE_THETA = 10000.0
NORM_EPS = 1e-6
LN_EPS = 1e-5
HEAD_DIM = 64
A_HEADS, A_KV_HEADS = 16, 4
B_HEADS = 8
C_HEADS = 4
C_GATE_RANK = 16
C_GATE_NORMALIZER = 16.0
C_CHUNK = 64
C_SUB = 16
GLA_FINISH_ROWS = 256
GLA_GROUP = 8
D_HEADS = 16
WIN_R, WIN_C = 8, 16
FF_DIM = 2816
DEEPNORM_ALPHA = (2 * DEPTH) ** 0.25
MASK_VALUE = -1e30
LOG2E = math.log2(math.e)

LANES = 128
MXU_WIDTH = 256
VMEM_LIMIT_BYTES = 56 * 1024 * 1024


def _cparams(n_axes):
    return pltpu.CompilerParams(
        dimension_semantics=("arbitrary",) * n_axes,
        vmem_limit_bytes=VMEM_LIMIT_BYTES)


def _dot(a, b):
    return jnp.dot(a, b, preferred_element_type=F32)


def _dot_nt(a, b):
    return lax.dot_general(a, b, (((1,), (1,)), ((), ())), preferred_element_type=F32)


def _dot_tn(a, b):
    return lax.dot_general(a, b, (((0,), (0,)), ((), ())), preferred_element_type=F32)


def _split_dot(x, m_bf16):
    hi = x.astype(BF)
    lo = (x - hi.astype(F32)).astype(BF)
    return _dot(hi, m_bf16) + _dot(lo, m_bf16)


def _split_dot_left(m_bf16, x):
    hi = x.astype(BF)
    lo = (x - hi.astype(F32)).astype(BF)
    return _dot(m_bf16, hi) + _dot(m_bf16, lo)


def _sigmoid(x):
    return 1.0 / (1.0 + jnp.exp(-x))


def _modulate(x_ref, mod_ref, shift_row, scale_row):
    x = x_ref[...]
    return x * (1.0 + mod_ref[scale_row:scale_row + 1, :]) + mod_ref[shift_row:shift_row + 1, :]


def _layer_norm(z, g, b):
    mu = jnp.mean(z, axis=-1, keepdims=True)
    zc = z - mu
    var = jnp.mean(zc * zc, axis=-1, keepdims=True)
    return zc * lax.rsqrt(var + LN_EPS) * g + b


def _rope(y, cos_t, sin_t, lo_half):
    swap = jnp.where(lo_half, pltpu.roll(y, LANES - 32, axis=1), pltpu.roll(y, 32, axis=1))
    return y * cos_t + swap * sin_t


def _idiv(x, n):
    return lax.shift_right_logical(x, int(n).bit_length() - 1)


def _imod(x, n):
    return x & (n - 1)


def _lane_iota(shape=(1, LANES)):
    return lax.broadcasted_iota(jnp.int32, shape, len(shape) - 1)


def _mod_kernel(cond_ref, w_ref, b_ref, o_ref):
    cnd = cond_ref[...]
    act = cnd * _sigmoid(cnd)
    o_ref[...] = _dot(act.astype(BF), w_ref[...].astype(BF)) + b_ref[...]


def _modulation(cond, mod_w, mod_b):
    depth, dm, n = mod_w.shape
    rows = cond.shape[0]
    tn = 1536
    return pl.pallas_call(
        _mod_kernel,
        out_shape=jax.ShapeDtypeStruct((depth, rows, n), F32),
        grid=(depth, n // tn),
        in_specs=[
            pl.BlockSpec((rows, dm), lambda i, j: (0, 0)),
            pl.BlockSpec((None, dm, tn), lambda i, j: (i, 0, j)),
            pl.BlockSpec((None, 1, tn), lambda i, j: (i, 0, j)),
        ],
        out_specs=pl.BlockSpec((None, rows, tn), lambda i, j: (i, 0, j)),
        compiler_params=_cparams(2),
        name="adaln_modulation",
    )(cond, mod_w, mod_b.reshape(depth, 1, n))


def _row_tile(n_tok):
    return 512 if n_tok % 512 == 0 else n_tok


def _proj_call(kernel, x, mod, w, extras, extra_specs, out_shapes, out_specs, name):
    bsz, n_tok, dm = x.shape
    tm = _row_tile(n_tok)
    in_specs = [
        pl.BlockSpec((None, tm, dm), lambda b, i: (b, i, 0)),
        pl.BlockSpec((None, N_MOD, dm), lambda b, i: (b, 0, 0)),
        pl.BlockSpec(w.shape, lambda b, i: (0, 0)),
    ] + list(extra_specs(tm))
    return pl.pallas_call(
        kernel,
        out_shape=out_shapes,
        grid=(bsz, n_tok // tm),
        in_specs=in_specs,
        out_specs=out_specs(tm),
        compiler_params=_cparams(2),
        name=name,
    )(x, mod, w, *extras)


def _full_spec(arr):
    nd = arr.ndim
    return pl.BlockSpec(arr.shape, lambda b, i: (0,) * nd)


def _seg_matrix():
    idx = np.arange(LANES) // HEAD_DIM
    return jnp.asarray((idx[:, None] == idx[None, :]).astype(np.float32) / HEAD_DIM, dtype=BF)


def _rope_tables(n_tok):
    t = jnp.arange(n_tok)
    row = (t // GRID_W).astype(F32)
    col = (t % GRID_W).astype(F32)
    n_freq = HEAD_DIM // 4
    inv = 1.0 / (ROPE_THETA ** (jnp.arange(n_freq, dtype=F32) / n_freq))
    ang = jnp.concatenate([row[:, None] * inv, col[:, None] * inv], axis=-1)
    cos, sin = jnp.cos(ang), jnp.sin(ang)
    cos_t = jnp.concatenate([cos, cos, cos, cos], axis=-1)
    sin_t = jnp.concatenate([-sin, sin, -sin, sin], axis=-1)
    return cos_t, sin_t


def _proj_qkv_kernel(x_ref, mod_ref, w_ref, *rest, n_q, n_k, n_v, norm, rope, q_scale):
    rest = list(rest)
    seg_ref = gq_ref = gk_ref = cos_ref = sin_ref = None
    if norm:
        seg_ref, gq_ref, gk_ref = rest[:3]
        rest = rest[3:]
    if rope:
        cos_ref, sin_ref = rest[:2]
        rest = rest[2:]
    (o_ref,) = rest
    hb = _modulate(x_ref, mod_ref, 0, 1).astype(BF)
    lo_half = _imod(_lane_iota(), HEAD_DIM) < (HEAD_DIM // 2)
    n_blocks = n_q + n_k + n_v
    per_dot = MXU_WIDTH // LANES
    ys = []
    for c0 in range(0, n_blocks, per_dot):
        nb = min(per_dot, n_blocks - c0)
        y2 = _dot(hb, w_ref[:, c0 * LANES:(c0 + nb) * LANES])
        ys += [y2[:, j * LANES:(j + 1) * LANES] for j in range(nb)]
    mean_sq = [_split_dot(y * y, seg_ref[...]) if norm and c < n_q + n_k else None for c, y in enumerate(ys)]
    for c, y in enumerate(ys):
        if c < n_q + n_k:
            if norm:
                gain = gq_ref[...] if c < n_q else gk_ref[...]
                y = y * lax.rsqrt(mean_sq[c] + NORM_EPS) * gain
            if rope:
                y = _rope(y, cos_ref[...], sin_ref[...], lo_half)
            if c < n_q and q_scale != 1.0:
                y = y * q_scale
        o_ref[:, c * LANES:(c + 1) * LANES] = y.astype(BF)


def _proj_qkv(x, mod, w, *, n_q, n_k, n_v, q_scale, norm_gains=None, rope=False):
    bsz, n_tok, _ = x.shape
    n_out = w.shape[1]
    extras, spec_fns = [], []
    if norm_gains is not None:
        gq, gk = norm_gains
        extras += [_seg_matrix(), jnp.tile(gq, 2).reshape(1, LANES), jnp.tile(gk, 2).reshape(1, LANES)]
        spec_fns += [lambda tm, a=a: _full_spec(a) for a in extras]
    if rope:
        cos_t, sin_t = _rope_tables(n_tok)
        extras += [cos_t, sin_t]
        spec_fns += [lambda tm: pl.BlockSpec((tm, LANES), lambda b, i: (i, 0))] * 2
    kern = functools.partial(_proj_qkv_kernel, n_q=n_q, n_k=n_k, n_v=n_v,
                             norm=norm_gains is not None, rope=rope, q_scale=q_scale)
    return _proj_call(
        kern, x, mod, w, extras, lambda tm: [f(tm) for f in spec_fns],
        jax.ShapeDtypeStruct((bsz, n_tok, n_out), BF),
        lambda tm: pl.BlockSpec((None, tm, n_out), lambda b, i: (b, i, 0)),
        "mixer_qkv_projection")


def _log_sigmoid(x):
    return jnp.minimum(x, 0.0) - jnp.log(1.0 + jnp.exp(-jnp.abs(x)))


def _proj_gla_kernel(x_ref, mod_ref, w_ref, wz_ref, wg_ref, bg_ref, o_ref, g_ref, *, q_scale, n_q):
    hb = _modulate(x_ref, mod_ref, 0, 1).astype(BF)
    n_blocks = o_ref.shape[-1] // LANES
    per_dot = MXU_WIDTH // LANES
    for c0 in range(0, n_blocks, per_dot):
        y2 = _dot(hb, w_ref[:, c0 * LANES:(c0 + per_dot) * LANES])
        if c0 < n_q:
            y2 = y2 * q_scale
        o_ref[:, c0 * LANES:(c0 + per_dot) * LANES] = y2.astype(BF)
    z = _dot(hb, wz_ref[...]).astype(BF)
    n_gate = g_ref.shape[-1]
    for c0 in range(0, n_gate, MXU_WIDTH):
        pre = _dot(z, wg_ref[:, c0:c0 + MXU_WIDTH]) + bg_ref[:, c0:c0 + MXU_WIDTH]
        g_ref[:, c0:c0 + MXU_WIDTH] = _log_sigmoid(pre) * (1.0 / C_GATE_NORMALIZER)


def _proj_gla(x, mod, w_main, w_z, w_gate, b_gate, *, q_scale, n_q):
    bsz, n_tok, _ = x.shape
    n_out, n_gate = w_main.shape[1], w_gate.shape[1]
    extras = [w_z, w_gate, b_gate]
    kern = functools.partial(_proj_gla_kernel, q_scale=q_scale, n_q=n_q)
    return _proj_call(
        kern, x, mod, w_main, extras, lambda tm: [_full_spec(a) for a in extras],
        (jax.ShapeDtypeStruct((bsz, n_tok, n_out), BF), jax.ShapeDtypeStruct((bsz, n_tok, n_gate), F32)),
        lambda tm: (pl.BlockSpec((None, tm, n_out), lambda b, i: (b, i, 0)),
                    pl.BlockSpec((None, tm, n_gate), lambda b, i: (b, i, 0))),
        "gla_projection")


FFN_CHUNK = 256
FFN_TILE = 512
HALO = 16


def _mix_ffn_kernel(a_ref, ap_ref, an_ref, x_ref, xp_ref, xn_ref, mod_ref, wo_ref, g1_ref, b1_ref,
                    wu_ref, cw_ref, cb_ref, wd_ref, g2_ref, b2_ref, o_ref, act_ref):
    tm = x_ref.shape[0]
    i = pl.program_id(1)
    has_prev = (i > 0).astype(F32)
    has_next = (i < pl.num_programs(1) - 1).astype(F32)
    n_ext = tm + 2 * HALO

    a_ext = jnp.concatenate([a_ref[...], an_ref[...], ap_ref[...]], axis=0)
    x_ext = jnp.concatenate([x_ref[...], xn_ref[...], xp_ref[...]], axis=0)
    z1 = DEEPNORM_ALPHA * x_ext + mod_ref[2:3, :] * _dot(a_ext, wo_ref[...])
    x_mid = _layer_norm(z1, g1_ref[...], b1_ref[...])
    h = x_mid * (1.0 + mod_ref[4:5, :]) + mod_ref[3:4, :]
    hb = jnp.concatenate([h[:tm], h[tm:tm + HALO] * has_next, h[tm + HALO:] * has_prev], axis=0).astype(BF)

    def up(c):
        cg = c * FFN_CHUNK
        return _dot(hb, wu_ref[:, cg:cg + FFN_CHUNK]), _dot(hb, wu_ref[:, FF_DIM + cg:FF_DIM + cg + FFN_CHUNK])

    def conv(u_ext, c0):
        u_dn = pltpu.roll(u_ext, 1, axis=0)[:tm]
        u_up = pltpu.roll(u_ext, n_ext - 1, axis=0)[:tm]
        w0 = cw_ref[0:1, c0:c0 + FFN_CHUNK]
        w1 = cw_ref[1:2, c0:c0 + FFN_CHUNK]
        w2 = cw_ref[2:3, c0:c0 + FFN_CHUNK]
        return cb_ref[:, c0:c0 + FFN_CHUNK] + u_dn * w0 + u_ext[:tm] * w1 + u_up * w2

    for c in range(FF_DIM // FFN_CHUNK):
        cg = c * FFN_CHUNK
        u_gate, u_val = up(c)
        gate = conv(u_gate, cg)
        val = conv(u_val, FF_DIM + cg)
        act_ref[:, cg:cg + FFN_CHUNK] = (gate * _sigmoid(gate) * val).astype(BF)
    z2 = DEEPNORM_ALPHA * x_mid[:tm] + mod_ref[5:6, :] * _dot(act_ref[...], wd_ref[...])
    o_ref[...] = _layer_norm(z2, g2_ref[...], b2_ref[...])


def _mix_ffn(a, x, mod, w_o, ln1_g, ln1_b, w_up, conv_w, conv_b, w_down, ln2_g, ln2_b):
    bsz, n_tok, dm = x.shape
    k_in = a.shape[-1]
    tm = FFN_TILE if n_tok % FFN_TILE == 0 else n_tok
    n_halo = n_tok // HALO
    per = tm // HALO
    tile = lambda b, i: (b, i, 0)
    prev = lambda b, i: (b, jnp.maximum(i * per - 1, 0), 0)
    nxt = lambda b, i: (b, jnp.minimum((i + 1) * per, n_halo - 1), 0)
    const2 = lambda b, i: (0, 0)
    resident = functools.partial(pl.BlockSpec, index_map=const2, pipeline_mode=pl.Buffered(1))
    row = lambda v: v.reshape(1, -1)
    return pl.pallas_call(
        _mix_ffn_kernel,
        out_shape=jax.ShapeDtypeStruct((bsz, n_tok, dm), F32),
        grid=(bsz, n_tok // tm),
        in_specs=[
            pl.BlockSpec((None, tm, k_in), tile),
            pl.BlockSpec((None, HALO, k_in), prev),
            pl.BlockSpec((None, HALO, k_in), nxt),
            pl.BlockSpec((None, tm, dm), tile),
            pl.BlockSpec((None, HALO, dm), prev),
            pl.BlockSpec((None, HALO, dm), nxt),
            pl.BlockSpec((None, N_MOD, dm), lambda b, i: (b, 0, 0)),
            resident(w_o.shape),
            pl.BlockSpec((1, dm), const2),
            pl.BlockSpec((1, dm), const2),
            resident(w_up.shape),
            pl.BlockSpec(conv_w.shape, const2),
            pl.BlockSpec((1, 2 * FF_DIM), const2),
            resident(w_down.shape),
            pl.BlockSpec((1, dm), const2),
            pl.BlockSpec((1, dm), const2),
        ],
        out_specs=pl.BlockSpec((None, tm, dm), tile),
        scratch_shapes=[pltpu.VMEM((tm, FF_DIM), BF)],
        compiler_params=_cparams(2),
        name="outproj_convffn_layernorm",
    )(a, a, a, x, x, x, mod, w_o, row(ln1_g), row(ln1_b), w_up, conv_w, row(conv_b), w_down,
      row(ln2_g), row(ln2_b))


ATT_SUB = 512
ATT_TILE = 2048


def _attn_tile(n_tok):
    return ATT_TILE if n_tok % ATT_TILE == 0 else n_tok


def _transpose_values(v_ref):
    return v_ref[...].astype(F32).T.astype(BF)


def _scores_t(q, keys, biases_t):
    return [_dot_nt(k, q) if bias is None else _dot_nt(k, q) + bias for k, bias in zip(keys, biases_t)]


def _probs_t(scores):
    m = functools.reduce(jnp.maximum, [jnp.max(s, axis=0, keepdims=True) for s in scores])
    probs = [jnp.exp2(s - m) for s in scores]
    denom = functools.reduce(lambda a, b: a + b, [jnp.sum(p, axis=0, keepdims=True) for p in probs])
    return [p.astype(BF) for p in probs], denom


def _pv_t(probs, denom, values_t):
    out_t = functools.reduce(lambda a, b: a + b, [_dot(vt, p) for p, vt in zip(probs, values_t)])
    return (out_t / denom).T


def _softmax_pv_t(scores, values_t):
    return _pv_t(*_probs_t(scores), values_t)


def _attend_pipelined(items, emit):
    n = len(items)
    scores = {j: _scores_t(*items[j][:3]) for j in range(min(2, n))}
    probs = {0: _probs_t(scores.pop(0))}
    for i in range(n):
        if i + 2 < n:
            scores[i + 2] = _scores_t(*items[i + 2][:3])
        if i + 1 < n:
            probs[i + 1] = _probs_t(scores.pop(i + 1))
        emit(i, _pv_t(*probs.pop(i), items[i][3]))


def _attn_gqa_kernel(q_ref, *rest, n_seg, n_blk):
    k_refs, v_refs, o_ref = rest[:n_seg], rest[n_seg:2 * n_seg], rest[2 * n_seg]
    grp = A_HEADS // A_KV_HEADS
    half = _idiv(_lane_iota(), HEAD_DIM)
    values_t = [_transpose_values(r) for r in v_refs]
    items, place = [], []
    for blk in range(n_blk):
        head = (pl.program_id(1) if n_blk == 1 else blk) * 2
        kv_pos = (head // grp) % 2
        kv_blk = 0 if n_blk == 1 else blk * 2 // grp // 2
        keys = [r[:, kv_blk * LANES:(kv_blk + 1) * LANES] for r in k_refs]
        values = [vt[kv_blk * LANES:(kv_blk + 1) * LANES, :] for vt in values_t]
        for r0 in range(0, q_ref.shape[0], ATT_SUB):
            q = q_ref[r0:r0 + ATT_SUB, blk * LANES:(blk + 1) * LANES].astype(F32)
            q_sw = pltpu.roll(q, HEAD_DIM, axis=1)
            for e in range(2):
                q_e = jnp.where(kv_pos == e, q, q_sw)
                q_e = jnp.where(half == kv_pos, q_e, 0.0).astype(BF)
                items.append((q_e, keys, [None] * n_seg, values))
                place.append((blk, r0, e, kv_pos))
    outs = {}

    def emit(i, o):
        blk, r0, e, kv_pos = place[i]
        outs[e] = jnp.where(kv_pos == e, o, pltpu.roll(o, HEAD_DIM, axis=1))
        if e == 1:
            o_ref[r0:r0 + ATT_SUB, blk * LANES:(blk + 1) * LANES] = (
                jnp.where(half == 0, outs[0], outs[1]).astype(BF))

    _attend_pipelined(items, emit)


def _blocks_per_step(n_q_tok, n_blocks):
    return n_blocks if n_q_tok < ATT_SUB else 1


def _attn_gqa(q_src, kv_srcs):
    bsz, n_q_tok, _ = q_src.shape
    tq = _attn_tile(n_q_tok)
    n_qb = A_HEADS * HEAD_DIM // LANES
    n_kb = A_KV_HEADS * HEAD_DIM // LANES
    per_kb = n_qb // n_kb
    n_seg = len(kv_srcs)
    n_blk = _blocks_per_step(n_q_tok, n_qb)
    kv_w = LANES if n_blk == 1 else n_kb * LANES
    k0, v0 = n_qb * LANES // kv_w, (n_qb + n_kb) * LANES // kv_w
    k_specs = [pl.BlockSpec((None, s.shape[1], kv_w), lambda b, p, i: (b, 0, k0 + p * n_blk // per_kb))
               for s in kv_srcs]
    v_specs = [pl.BlockSpec((None, s.shape[1], kv_w), lambda b, p, i: (b, 0, v0 + p * n_blk // per_kb))
               for s in kv_srcs]
    return pl.pallas_call(
        functools.partial(_attn_gqa_kernel, n_seg=n_seg, n_blk=n_blk),
        out_shape=jax.ShapeDtypeStruct((bsz, n_q_tok, n_qb * LANES), BF),
        grid=(bsz, n_qb // n_blk, n_q_tok // tq),
        in_specs=[pl.BlockSpec((None, tq, n_blk * LANES), lambda b, p, i: (b, i, p))] + k_specs + v_specs,
        out_specs=pl.BlockSpec((None, tq, n_blk * LANES), lambda b, p, i: (b, i, p)),
        compiler_params=_cparams(3),
        name="gqa_attention",
    )(q_src, *kv_srcs, *kv_srcs)


def _attn_diff_kernel(q_ref, lam_ref, subln_ref, *rest, n_seg, lam_init):
    k_refs, v_refs, o_ref = rest[:n_seg], rest[n_seg:2 * n_seg], rest[2 * n_seg]
    lam_v = lam_ref[...]
    lam = (jnp.exp(jnp.sum(lam_v[0:1] * lam_v[1:2], axis=-1, keepdims=True))
           - jnp.exp(jnp.sum(lam_v[2:3] * lam_v[3:4], axis=-1, keepdims=True)) + lam_init)
    half = _idiv(_lane_iota(), HEAD_DIM)
    values_t = [_transpose_values(r) for r in v_refs]
    n_blk = q_ref.shape[1] // LANES
    items, place = [], []
    for blk in range(n_blk):
        lanes = slice(blk * LANES, (blk + 1) * LANES)
        keys = [r[:, lanes] for r in k_refs]
        values = [vt[lanes, :] for vt in values_t]
        for r0 in range(0, q_ref.shape[0], ATT_SUB):
            q = q_ref[r0:r0 + ATT_SUB, lanes]
            for e in range(2):
                items.append((jnp.where(half == e, q, jnp.zeros_like(q)), keys, [None] * n_seg, values))
                place.append((lanes, r0, e))
    outs = {}

    def emit(i, o):
        lanes, r0, e = place[i]
        outs[e] = o
        if e == 1:
            o = outs[0] - lam * outs[1]
            ms = jnp.mean(o * o, axis=-1, keepdims=True)
            o = o * lax.rsqrt(ms + NORM_EPS) * subln_ref[...] * (1.0 - lam_init)
            o_ref[r0:r0 + ATT_SUB, lanes] = o.astype(BF)

    _attend_pipelined(items, emit)


def _attn_diff(q_src, kv_srcs, lam_vecs, subln, lam_init):
    bsz, n_q_tok, _ = q_src.shape
    tq = _attn_tile(n_q_tok)
    nh = B_HEADS
    n_seg = len(kv_srcs)
    n_blk = _blocks_per_step(n_q_tok, nh)
    w = n_blk * LANES
    steps = nh // n_blk
    k_specs = [pl.BlockSpec((None, s.shape[1], w), lambda b, h, i: (b, 0, steps + h)) for s in kv_srcs]
    v_specs = [pl.BlockSpec((None, s.shape[1], w), lambda b, h, i: (b, 0, 2 * steps + h)) for s in kv_srcs]
    return pl.pallas_call(
        functools.partial(_attn_diff_kernel, n_seg=n_seg, lam_init=lam_init),
        out_shape=jax.ShapeDtypeStruct((bsz, n_q_tok, nh * LANES), BF),
        grid=(bsz, steps, n_q_tok // tq),
        in_specs=[pl.BlockSpec((None, tq, w), lambda b, h, i: (b, i, h)),
                  pl.BlockSpec(lam_vecs.shape, lambda b, h, i: (0, 0)),
                  pl.BlockSpec((1, LANES), lambda b, h, i: (0, 0))] + k_specs + v_specs,
        out_specs=pl.BlockSpec((None, tq, w), lambda b, h, i: (b, i, h)),
        compiler_params=_cparams(3),
        name="differential_attention",
    )(q_src, lam_vecs, subln.reshape(1, LANES), *kv_srcs, *kv_srcs)


NB_QROWS = 4
NB_BAND = 12


def _nb_band_start(g, rows):
    return int(np.clip(NB_QROWS * g - NB_QROWS, 0, rows - NB_BAND))


def _nb_config(g, rows):
    n_g = rows // NB_QROWS
    return 0 if g == 0 else (2 if g == n_g - 1 else 1)


def _nb_bias_table(rpb, rows):
    nh = rpb.shape[0]
    wr = min(WIN_R, rows)
    cq = np.arange(GRID_W)
    c0 = np.clip(cq - WIN_C // 2, 0, GRID_W - WIN_C)
    col_in = (cq[None, :] >= c0[:, None]) & (cq[None, :] < c0[:, None] + WIN_C)
    dc_idx = np.clip(cq[None, :] - cq[:, None], -(WIN_C - 1), WIN_C - 1) + WIN_C - 1
    n_dr = 2 * WIN_R - 1
    tiles = rpb[:, :, dc_idx]
    tiles = jnp.where(col_in[None, None], tiles * LOG2E, MASK_VALUE)
    tiles = jnp.concatenate([tiles, jnp.full((nh, 1, GRID_W, GRID_W), MASK_VALUE, F32)], axis=1)
    n_g = rows // NB_QROWS
    sel = np.full((3, NB_QROWS, NB_BAND), n_dr, dtype=np.int32)
    for cfg, g in ((0, 0), (1, 1), (2, n_g - 1)):
        start = _nb_band_start(g, rows)
        for a in range(NB_QROWS):
            r = NB_QROWS * g + a
            r0 = int(np.clip(r - wr // 2, 0, rows - wr))
            for jb in range(NB_BAND):
                kr = start + jb
                if r0 <= kr < r0 + wr:
                    sel[cfg, a, jb] = kr - r + (WIN_R - 1)
    big = tiles[:, sel]
    big = big.transpose(0, 1, 3, 5, 2, 4)
    return big.reshape(nh, 3, NB_BAND * GRID_W, NB_QROWS * GRID_W)


def _attn_nb_kernel(q_ref, kl_ref, vl_ref, kc_ref, vc_ref, bias_ref, o_ref, *, rows):
    half = _idiv(_lane_iota(), HEAD_DIM)
    kc = kc_ref[...]
    vc = _transpose_values(vc_ref)
    vl = _transpose_values(vl_ref)
    tq = NB_QROWS * GRID_W
    tk = NB_BAND * GRID_W
    items = []
    for g in range(rows // NB_QROWS):
        ks = _nb_band_start(g, rows) * GRID_W
        cfg = _nb_config(g, rows)
        q = q_ref[g * tq:(g + 1) * tq, :]
        kb = kl_ref[ks:ks + tk, :]
        vb = vl[:, ks:ks + tk]
        for e in range(2):
            q_e = jnp.where(half == e, q, jnp.zeros_like(q))
            items.append((q_e, [kb, kc], [bias_ref[e, cfg], None], [vb, vc]))
    outs = {}

    def emit(i, o):
        g, e = i // 2, i % 2
        outs[e] = o
        if e == 1:
            o_ref[g * tq:(g + 1) * tq, :] = jnp.where(half == 0, outs[0], outs[1]).astype(BF)

    _attend_pipelined(items, emit)


def _attn_nb(qkv_lat, qkv_ctx, bias):
    bsz, n_tok, _ = qkv_lat.shape
    n_ctx = qkv_ctx.shape[1]
    rows = n_tok // GRID_W
    npair = D_HEADS * HEAD_DIM // LANES
    return pl.pallas_call(
        functools.partial(_attn_nb_kernel, rows=rows),
        out_shape=jax.ShapeDtypeStruct((bsz, n_tok, npair * LANES), BF),
        grid=(npair, bsz),
        in_specs=[
            pl.BlockSpec((None, n_tok, LANES), lambda p, b: (b, 0, p)),
            pl.BlockSpec((None, n_tok, LANES), lambda p, b: (b, 0, npair + p)),
            pl.BlockSpec((None, n_tok, LANES), lambda p, b: (b, 0, 2 * npair + p)),
            pl.BlockSpec((None, n_ctx, LANES), lambda p, b: (b, 0, npair + p)),
            pl.BlockSpec((None, n_ctx, LANES), lambda p, b: (b, 0, 2 * npair + p)),
            pl.BlockSpec((2,) + bias.shape[1:], lambda p, b: (p, 0, 0, 0)),
        ],
        out_specs=pl.BlockSpec((None, n_tok, LANES), lambda p, b: (b, 0, p)),
        compiler_params=_cparams(2),
        name="neighbourhood_attention",
    )(qkv_lat, qkv_lat, qkv_lat, qkv_ctx, qkv_ctx, bias)


def _attn_ctx_kernel(q_ref, k_ref, v_ref, o_ref):
    half = _idiv(_lane_iota(), HEAD_DIM)
    q = q_ref[...]
    zero = jnp.zeros_like(q)
    keys, values = [k_ref[...]], [_transpose_values(v_ref)]
    outs = [_softmax_pv_t(_scores_t(jnp.where(half == e, q, zero), keys, [None]), values) for e in range(2)]
    o_ref[...] = jnp.where(half == 0, outs[0], outs[1]).astype(BF)


def _attn_ctx(qkv_ctx, npair):
    bsz, n_ctx, _ = qkv_ctx.shape
    return pl.pallas_call(
        _attn_ctx_kernel,
        out_shape=jax.ShapeDtypeStruct((bsz, n_ctx, npair * LANES), BF),
        grid=(bsz, npair),
        in_specs=[
            pl.BlockSpec((None, n_ctx, LANES), lambda b, p: (b, 0, p)),
            pl.BlockSpec((None, n_ctx, LANES), lambda b, p: (b, 0, npair + p)),
            pl.BlockSpec((None, n_ctx, LANES), lambda b, p: (b, 0, 2 * npair + p)),
        ],
        out_specs=pl.BlockSpec((None, n_ctx, LANES), lambda b, p: (b, 0, p)),
        compiler_params=_cparams(2),
        name="context_attention",
    )(qkv_ctx, qkv_ctx, qkv_ctx)


def _gla_geometry(rev):
    row = lax.broadcasted_iota(jnp.int32, (C_CHUNK, 1), 0)
    col = lax.broadcasted_iota(jnp.int32, (1, C_CHUNK), 1)
    sub = _idiv(row, C_SUB)
    col_sub = _idiv(col, C_SUB)
    if rev:
        tri, cross, dist = col >= row, col_sub > sub, col - row
    else:
        tri, cross, dist = col <= row, col_sub < sub, row - col
    off_blk = jnp.where(cross, sub, -1)
    dmat = jnp.where(tri, jnp.where(col_sub == sub, dist, -1), -1)
    return tri.astype(BF), sub, off_blk, dmat


def _gla_decay(g, geom):
    return _split_dot_left(geom[0], g) * LOG2E


def _gla_factors(q, k, b, rev, geom):
    sub = geom[1]
    nsub = C_CHUNK // C_SUB
    b_tot = b[0:1] if rev else b[C_CHUNK - 1:C_CHUNK]
    blocks = range(nsub - 1) if rev else range(1, nsub)
    bounds = {}
    ref_rows = jnp.zeros_like(b)
    for a in blocks:
        r = (a + 1) * C_SUB if rev else a * C_SUB - 1
        bounds[a] = b[r:r + 1]
        ref_rows = jnp.where(sub == a, bounds[a], ref_rows)
    q_in = (q * jnp.exp2(b)).astype(BF)
    q_t = (q * jnp.exp2(b - ref_rows)).astype(BF)
    k_ts = {a: (k * jnp.exp2(jnp.minimum(bounds[a] - b, 0.0))).astype(BF) for a in blocks}
    k_end = (k * jnp.exp2(b_tot - b)).astype(BF)
    return q_in, q_t, k_ts, k_end, jnp.exp2(b_tot)


def _gla_cross(factors, v, geom):
    _, q_t, k_ts, k_end, _ = factors
    a_mat = jnp.zeros((C_CHUNK, C_CHUNK), F32)
    for a, k_t in k_ts.items():
        a_mat = jnp.where(geom[2] == a, _dot_nt(q_t, k_t), a_mat)
    return a_mat, _dot_tn(v, k_end)


def _gla_same(q, k, b, a_mat, rev, geom):
    nsub, half = C_CHUNK // C_SUB, C_SUB // 2

    def halves(x):
        x = x.reshape(nsub, C_SUB, x.shape[-1])
        return x[:, :half], x[:, half:]

    first, second = zip(*[halves(x) for x in (q, k, b, a_mat, geom[3])])
    near, far = (second, first) if rev else (first, second)
    (q_n, k_n, b_n, a_n, d_n), (q_f, k_f, b_f, a_f, d_f) = near, far
    r = lax.broadcasted_iota(jnp.int32, (1, half, 1), 1)

    def rot(x, d):
        shift = (half - d) % half if rev else d
        return x if shift == 0 else pltpu.roll(x, shift, axis=1)

    def diag(qq, kk, bb, b_partner):
        return jnp.sum(qq * kk * jnp.exp2(bb - b_partner), axis=-1, keepdims=True)

    for d in range(half):
        kn, bn, kf, bf = rot(k_n, d), rot(b_n, d), rot(k_f, d), rot(b_f, d)
        own = (r + d < half) if rev else (r >= d)
        a_n = jnp.where(d_n == d, diag(q_n, kn, b_n, bn), a_n)
        a_f = jnp.where(d_f == d, diag(q_f, jnp.where(own, kf, kn), b_f, jnp.where(own, bf, bn)), a_f)
    for d in range(half, C_SUB):
        a_f = jnp.where(d_f == d, diag(q_f, rot(k_n, d - half), b_f, rot(b_n, d - half)), a_f)
    lo, hi = (a_f, a_n) if rev else (a_n, a_f)
    return jnp.concatenate([lo, hi], axis=1).reshape(C_CHUNK, C_CHUNK)


def _gla_advance(factors, a_mat, v, st, increment):
    q_in, _, _, _, decay = factors
    o = _dot_nt(q_in, st.astype(BF)) + _dot(a_mat.astype(BF), v)
    return o, st * decay + increment


def _gla_kernel(ql_ref, kl_ref, vl_ref, ogl_ref, gfl_ref, gbl_ref,
                qc_ref, kc_ref, vc_ref, ogc_ref, gfc_ref, gbc_ref, ng_ref,
                ol_ref, oc_ref, fl_ref, bl_ref, fc_ref, bc_ref, *, with_ctx_out):
    dk = ql_ref.shape[-1]
    dv = vl_ref.shape[-1]
    n_lat = ql_ref.shape[0] // C_CHUNK
    n_ctx = qc_ref.shape[0] // C_CHUNK
    geom_f = _gla_geometry(False)
    geom_b = _gla_geometry(True)

    def load(q_ref, k_ref, v_ref, g_ref, c):
        sl = pl.ds(pl.multiple_of(c * C_CHUNK, C_CHUNK), C_CHUNK)
        return q_ref[sl, :].astype(F32), k_ref[sl, :].astype(F32), v_ref[sl, :], g_ref[sl, :]

    def chunk_at(c):
        return pl.ds(pl.multiple_of(c * C_CHUNK, C_CHUNK), C_CHUNK)

    def scan_both(refs_f, refs_b, f_ref, b_ref, n_chunks, states):
        group = min(GLA_GROUP, n_chunks)

        def body(step, carry):
            st_f, st_b = carry
            work = []
            for u in range(group):
                j = step * group + u
                work.append((False, geom_f, refs_f, j, f_ref))
                work.append((True, geom_b, refs_b, n_chunks - 1 - j, b_ref))
            data = [load(*refs, c) for _, _, refs, c, _ in work]
            decays = [_gla_decay(d[3], w[1]) for d, w in zip(data, work)]
            factors = [_gla_factors(d[0], d[1], b, w[0], w[1]) for d, b, w in zip(data, decays, work)]
            cross = [_gla_cross(f, d[2], w[1]) for f, d, w in zip(factors, data, work)]
            a_mats = [_gla_same(d[0], d[1], b, x[0], w[0], w[1])
                      for d, b, x, w in zip(data, decays, cross, work)]
            for idx, (rev, _, _, c, out_ref) in enumerate(work):
                st = st_b if rev else st_f
                o, st = _gla_advance(factors[idx], a_mats[idx], data[idx][2], st, cross[idx][1])
                out_ref[chunk_at(c), :] = o
                if rev:
                    st_b = st
                else:
                    st_f = st
            return st_f, st_b
        return lax.fori_loop(0, n_chunks // group, body, states)

    def finish(f_ref, b_ref, og_ref, out_ref):
        def body(i, carry):
            sl = pl.ds(pl.multiple_of(i * GLA_FINISH_ROWS, GLA_FINISH_ROWS), GLA_FINISH_ROWS)
            tot = f_ref[sl, :] + b_ref[sl, :]
            ms = jnp.mean(tot * tot, axis=-1, keepdims=True)
            y = tot * lax.rsqrt(ms + NORM_EPS) * ng_ref[...]
            og = og_ref[sl, :].astype(F32)
            out_ref[sl, :] = (y * (og * _sigmoid(og))).astype(BF)
            return carry
        lax.fori_loop(0, f_ref.shape[0] // GLA_FINISH_ROWS, body, 0)

    zero = jnp.zeros((dv, dk), F32)
    lat_f = (ql_ref, kl_ref, vl_ref, gfl_ref)
    lat_b = (ql_ref, kl_ref, vl_ref, gbl_ref)
    ctx_f = (qc_ref, kc_ref, vc_ref, gfc_ref)
    ctx_b = (qc_ref, kc_ref, vc_ref, gbc_ref)
    states = scan_both(ctx_f, ctx_b, fc_ref, bc_ref, n_ctx, (zero, zero))
    scan_both(lat_f, lat_b, fl_ref, bl_ref, n_lat, states)
    finish(fl_ref, bl_ref, ogl_ref, ol_ref)
    if with_ctx_out:
        finish(fc_ref, bc_ref, ogc_ref, oc_ref)
    else:
        oc_ref[...] = jnp.zeros_like(oc_ref)


def _gla(proj_lat, gates_lat, proj_ctx, gates_ctx, norm_g, with_ctx_out):
    bsz, n_lat, _ = proj_lat.shape
    n_ctx = proj_ctx.shape[1]
    nh = C_HEADS
    dk = 512 // nh
    dv = 1024 // nh
    kb, vb = 512 // dk, 1024 // dv

    def specs(n_tok):
        return [
            pl.BlockSpec((None, n_tok, dk), lambda b, h: (b, 0, h)),
            pl.BlockSpec((None, n_tok, dk), lambda b, h: (b, 0, kb + h)),
            pl.BlockSpec((None, n_tok, dv), lambda b, h: (b, 0, (2 * kb * dk) // dv + h)),
            pl.BlockSpec((None, n_tok, dv), lambda b, h: (b, 0, (2 * kb * dk) // dv + vb + h)),
            pl.BlockSpec((None, n_tok, dk), lambda b, h: (b, 0, h)),
            pl.BlockSpec((None, n_tok, dk), lambda b, h: (b, 0, kb + h)),
        ]

    out_l, out_c = pl.pallas_call(
        functools.partial(_gla_kernel, with_ctx_out=with_ctx_out),
        out_shape=(jax.ShapeDtypeStruct((bsz, n_lat, nh * dv), BF),
                   jax.ShapeDtypeStruct((bsz, n_ctx, nh * dv), BF)),
        grid=(bsz, nh),
        in_specs=specs(n_lat) + specs(n_ctx) + [pl.BlockSpec((1, dv), lambda b, h: (0, 0))],
        out_specs=(pl.BlockSpec((None, n_lat, dv), lambda b, h: (b, 0, h)),
                   pl.BlockSpec((None, n_ctx, dv), lambda b, h: (b, 0, h))),
        scratch_shapes=[pltpu.VMEM((n_lat, dv), F32), pltpu.VMEM((n_lat, dv), F32),
                        pltpu.VMEM((n_ctx, dv), F32), pltpu.VMEM((n_ctx, dv), F32)],
        compiler_params=_cparams(2),
        name="gated_linear_attention",
    )(proj_lat, proj_lat, proj_lat, proj_lat, gates_lat, gates_lat,
      proj_ctx, proj_ctx, proj_ctx, proj_ctx, gates_ctx, gates_ctx, norm_g.reshape(1, dv))
    return out_l, out_c


def kernel(x, c, ctx, c_ctx, mod_w, mod_b, ln1_g, ln1_b, ffn_w_up, ffn_conv_w, ffn_conv_b, ffn_w_down, ln2_g, ln2_b, a_w_qkv, a_q_norm, a_k_norm, a_w_o, b_w_qkv, b_lambda_q1, b_lambda_k1, b_lambda_q2, b_lambda_k2, b_subln, b_w_o, c_w_in, c_w_gate_fwd, c_b_gate_fwd, c_w_gate_bwd, c_b_gate_bwd, c_norm, c_w_o, d_w_qkv, d_rpb, d_w_o):
    bsz, n_lat, dm = x.shape
    n_ctx = ctx.shape[1]
    scale = HEAD_DIM ** -0.5 * LOG2E

    pad_rows = (-(bsz + 1)) % 8
    cond = jnp.concatenate([c, c_ctx[None, :], jnp.zeros((pad_rows, dm), F32)], axis=0)
    mod_all = _modulation(cond, mod_w, mod_b)

    x_lat, x_ctx = x, ctx
    for i in range(DEPTH):
        kind, j = i % N_MIXERS, i // N_MIXERS
        last = i == DEPTH - 1
        m_lat = mod_all[i, :bsz].reshape(bsz, N_MOD, dm)
        m_ctx = jnp.broadcast_to(mod_all[i, bsz].reshape(1, N_MOD, dm), (bsz, N_MOD, dm))

        if kind == 0:
            w = a_w_qkv[j].astype(BF)
            gains = (a_q_norm[j], a_k_norm[j])
            nq, nk = A_HEADS * HEAD_DIM // LANES, A_KV_HEADS * HEAD_DIM // LANES
            p_lat = _proj_qkv(x_lat, m_lat, w, n_q=nq, n_k=nk, n_v=nk, q_scale=scale, norm_gains=gains, rope=True)
            p_ctx = _proj_qkv(x_ctx, m_ctx, w, n_q=nq, n_k=nk, n_v=nk, q_scale=scale, norm_gains=gains)
            a_lat = _attn_gqa(p_lat, [p_ctx, p_lat])
            a_ctx = None if last else _attn_gqa(p_ctx, [p_ctx])
            w_o = a_w_o[j]
        elif kind == 1:
            w = b_w_qkv[j].astype(BF)
            nb = 2 * B_HEADS * HEAD_DIM // LANES
            p_lat = _proj_qkv(x_lat, m_lat, w, n_q=nb, n_k=nb, n_v=nb, q_scale=scale, rope=True)
            p_ctx = _proj_qkv(x_ctx, m_ctx, w, n_q=nb, n_k=nb, n_v=nb, q_scale=scale)
            lam_vecs = jnp.stack([b_lambda_q1[j], b_lambda_k1[j], b_lambda_q2[j], b_lambda_k2[j]])
            lam_init = 0.8 - 0.6 * math.exp(-0.3 * i)
            a_lat = _attn_diff(p_lat, [p_ctx, p_lat], lam_vecs, b_subln[j], lam_init)
            a_ctx = None if last else _attn_diff(p_ctx, [p_ctx], lam_vecs, b_subln[j], lam_init)
            w_o = b_w_o[j]
        elif kind == 2:
            w_in = c_w_in[j]
            n_main = w_in.shape[1] - 2 * C_GATE_RANK
            w_main = w_in[:, :n_main].astype(BF)
            w_z = jnp.pad(w_in[:, n_main:], ((0, 0), (0, LANES - 2 * C_GATE_RANK))).astype(BF)
            kd = c_w_gate_fwd.shape[-1]
            w_gate = jnp.zeros((LANES, 2 * kd), F32)
            w_gate = w_gate.at[:C_GATE_RANK, :kd].set(c_w_gate_fwd[j])
            w_gate = w_gate.at[C_GATE_RANK:2 * C_GATE_RANK, kd:].set(c_w_gate_bwd[j]).astype(BF)
            b_gate = jnp.concatenate([c_b_gate_fwd[j], c_b_gate_bwd[j]]).reshape(1, 2 * kd)
            q_scale = (kd // C_HEADS) ** -0.5
            nq = kd // LANES
            p_lat, g_lat = _proj_gla(x_lat, m_lat, w_main, w_z, w_gate, b_gate, q_scale=q_scale, n_q=nq)
            p_ctx, g_ctx = _proj_gla(x_ctx, m_ctx, w_main, w_z, w_gate, b_gate, q_scale=q_scale, n_q=nq)
            a_lat, a_ctx = _gla(p_lat, g_lat, p_ctx, g_ctx, c_norm[j], not last)
            w_o = c_w_o[j]
        else:
            w = d_w_qkv[j].astype(BF)
            nb = D_HEADS * HEAD_DIM // LANES
            p_lat = _proj_qkv(x_lat, m_lat, w, n_q=nb, n_k=nb, n_v=nb, q_scale=scale)
            p_ctx = _proj_qkv(x_ctx, m_ctx, w, n_q=nb, n_k=nb, n_v=nb, q_scale=scale)
            bias = _nb_bias_table(d_rpb[j], n_lat // GRID_W)
            a_lat = _attn_nb(p_lat, p_ctx, bias)
            a_ctx = None if last else _attn_ctx(p_ctx, nb)
            w_o = d_w_o[j]

        w_o = w_o.astype(BF)
        w_up = ffn_w_up[i].astype(BF)
        w_dn = ffn_w_down[i].astype(BF)
        ffn = (w_o, ln1_g[i], ln1_b[i], w_up, ffn_conv_w[i], ffn_conv_b[i], w_dn, ln2_g[i], ln2_b[i])
        x_lat = _mix_ffn(a_lat, x_lat, m_lat, *ffn)
        if not last:
            x_ctx = _mix_ffn(a_ctx, x_ctx, m_ctx, *ffn)
    return x_lat
```

```python
import functools
import math

import jax
import jax.numpy as jnp
import numpy as np
from jax import lax
from jax.experimental import pallas as pl
from jax.experimental.pallas import tpu as pltpu

F32 = jnp.float32
BF = jnp.bfloat16

DEPTH = 4
N_MIXERS = 4
N_MOD = 6
GRID_W = 64
ROPE_THETA = 10000.0
NORM_EPS = 1e-6
LN_EPS = 1e-5
HEAD_DIM = 64
A_HEADS, A_KV_HEADS = 16, 4
B_HEADS = 8
C_HEADS = 4
C_GATE_RANK = 16
C_GATE_NORMALIZER = 16.0
C_CHUNK = 64
C_SUB = 16
GLA_FINISH_ROWS = 256
GLA_GROUP = 8
D_HEADS = 16
WIN_R, WIN_C = 8, 16
FF_DIM = 2816
DEEPNORM_ALPHA = (2 * DEPTH) ** 0.25
MASK_VALUE = -1e30
LOG2E = math.log2(math.e)

LANES = 128
MXU_WIDTH = 256
VMEM_LIMIT_BYTES = 56 * 1024 * 1024


def _cparams(n_axes):
    return pltpu.CompilerParams(
        dimension_semantics=("arbitrary",) * n_axes,
        vmem_limit_bytes=VMEM_LIMIT_BYTES)


def _dot(a, b):
    return jnp.dot(a, b, preferred_element_type=F32)


def _dot_nt(a, b):
    return lax.dot_general(a, b, (((1,), (1,)), ((), ())), preferred_element_type=F32)


def _dot_tn(a, b):
    return lax.dot_general(a, b, (((0,), (0,)), ((), ())), preferred_element_type=F32)


def _split_dot(x, m_bf16):
    hi = x.astype(BF)
    lo = (x - hi.astype(F32)).astype(BF)
    return _dot(hi, m_bf16) + _dot(lo, m_bf16)


def _split_dot_left(m_bf16, x):
    hi = x.astype(BF)
    lo = (x - hi.astype(F32)).astype(BF)
    return _dot(m_bf16, hi) + _dot(m_bf16, lo)


def _sigmoid(x):
    return 1.0 / (1.0 + jnp.exp(-x))


def _modulate(x_ref, mod_ref, shift_row, scale_row):
    x = x_ref[...]
    return x * (1.0 + mod_ref[scale_row:scale_row + 1, :]) + mod_ref[shift_row:shift_row + 1, :]


def _layer_norm(z, g, b):
    mu = jnp.mean(z, axis=-1, keepdims=True)
    zc = z - mu
    var = jnp.mean(zc * zc, axis=-1, keepdims=True)
    return zc * lax.rsqrt(var + LN_EPS) * g + b


def _rope(y, cos_t, sin_t, lo_half):
    swap = jnp.where(lo_half, pltpu.roll(y, LANES - 32, axis=1), pltpu.roll(y, 32, axis=1))
    return y * cos_t + swap * sin_t


def _idiv(x, n):
    return lax.shift_right_logical(x, int(n).bit_length() - 1)


def _imod(x, n):
    return x & (n - 1)


def _lane_iota(shape=(1, LANES)):
    return lax.broadcasted_iota(jnp.int32, shape, len(shape) - 1)


def _mod_kernel(cond_ref, w_ref, b_ref, o_ref):
    cnd = cond_ref[...]
    act = cnd * _sigmoid(cnd)
    o_ref[...] = _dot(act.astype(BF), w_ref[...].astype(BF)) + b_ref[...]


def _modulation(cond, mod_w, mod_b):
    depth, dm, n = mod_w.shape
    rows = cond.shape[0]
    tn = 1536
    return pl.pallas_call(
        _mod_kernel,
        out_shape=jax.ShapeDtypeStruct((depth, rows, n), F32),
        grid=(depth, n // tn),
        in_specs=[
            pl.BlockSpec((rows, dm), lambda i, j: (0, 0)),
            pl.BlockSpec((None, dm, tn), lambda i, j: (i, 0, j)),
            pl.BlockSpec((None, 1, tn), lambda i, j: (i, 0, j)),
        ],
        out_specs=pl.BlockSpec((None, rows, tn), lambda i, j: (i, 0, j)),
        compiler_params=_cparams(2),
        name="adaln_modulation",
    )(cond, mod_w, mod_b.reshape(depth, 1, n))


def _row_tile(n_tok):
    return 512 if n_tok % 512 == 0 else n_tok


def _proj_call(kernel, x, mod, w, extras, extra_specs, out_shapes, out_specs, name):
    bsz, n_tok, dm = x.shape
    tm = _row_tile(n_tok)
    in_specs = [
        pl.BlockSpec((None, tm, dm), lambda b, i: (b, i, 0)),
        pl.BlockSpec((None, N_MOD, dm), lambda b, i: (b, 0, 0)),
        pl.BlockSpec(w.shape, lambda b, i: (0, 0)),
    ] + list(extra_specs(tm))
    return pl.pallas_call(
        kernel,
        out_shape=out_shapes,
        grid=(bsz, n_tok // tm),
        in_specs=in_specs,
        out_specs=out_specs(tm),
        compiler_params=_cparams(2),
        name=name,
    )(x, mod, w, *extras)


def _full_spec(arr):
    nd = arr.ndim
    return pl.BlockSpec(arr.shape, lambda b, i: (0,) * nd)


def _seg_matrix():
    idx = np.arange(LANES) // HEAD_DIM
    return jnp.asarray((idx[:, None] == idx[None, :]).astype(np.float32) / HEAD_DIM, dtype=BF)


def _rope_tables(n_tok):
    t = jnp.arange(n_tok)
    row = (t // GRID_W).astype(F32)
    col = (t % GRID_W).astype(F32)
    n_freq = HEAD_DIM // 4
    inv = 1.0 / (ROPE_THETA ** (jnp.arange(n_freq, dtype=F32) / n_freq))
    ang = jnp.concatenate([row[:, None] * inv, col[:, None] * inv], axis=-1)
    cos, sin = jnp.cos(ang), jnp.sin(ang)
    cos_t = jnp.concatenate([cos, cos, cos, cos], axis=-1)
    sin_t = jnp.concatenate([-sin, sin, -sin, sin], axis=-1)
    return cos_t, sin_t


def _proj_qkv_kernel(x_ref, mod_ref, w_ref, *rest, n_q, n_k, n_v, norm, rope, q_scale):
    rest = list(rest)
    seg_ref = gq_ref = gk_ref = cos_ref = sin_ref = None
    if norm:
        seg_ref, gq_ref, gk_ref = rest[:3]
        rest = rest[3:]
    if rope:
        cos_ref, sin_ref = rest[:2]
        rest = rest[2:]
    (o_ref,) = rest
    hb = _modulate(x_ref, mod_ref, 0, 1).astype(BF)
    lo_half = _imod(_lane_iota(), HEAD_DIM) < (HEAD_DIM // 2)
    n_blocks = n_q + n_k + n_v
    per_dot = MXU_WIDTH // LANES
    ys = []
    for c0 in range(0, n_blocks, per_dot):
        nb = min(per_dot, n_blocks - c0)
        y2 = _dot(hb, w_ref[:, c0 * LANES:(c0 + nb) * LANES])
        ys += [y2[:, j * LANES:(j + 1) * LANES] for j in range(nb)]
    mean_sq = [_split_dot(y * y, seg_ref[...]) if norm and c < n_q + n_k else None for c, y in enumerate(ys)]
    for c, y in enumerate(ys):
        if c < n_q + n_k:
            if norm:
                gain = gq_ref[...] if c < n_q else gk_ref[...]
                y = y * lax.rsqrt(mean_sq[c] + NORM_EPS) * gain
            if rope:
                y = _rope(y, cos_ref[...], sin_ref[...], lo_half)
            if c < n_q and q_scale != 1.0:
                y = y * q_scale
        o_ref[:, c * LANES:(c + 1) * LANES] = y.astype(BF)


def _proj_qkv(x, mod, w, *, n_q, n_k, n_v, q_scale, norm_gains=None, rope=False):
    bsz, n_tok, _ = x.shape
    n_out = w.shape[1]
    extras, spec_fns = [], []
    if norm_gains is not None:
        gq, gk = norm_gains
        extras += [_seg_matrix(), jnp.tile(gq, 2).reshape(1, LANES), jnp.tile(gk, 2).reshape(1, LANES)]
        spec_fns += [lambda tm, a=a: _full_spec(a) for a in extras]
    if rope:
        cos_t, sin_t = _rope_tables(n_tok)
        extras += [cos_t, sin_t]
        spec_fns += [lambda tm: pl.BlockSpec((tm, LANES), lambda b, i: (i, 0))] * 2
    kern = functools.partial(_proj_qkv_kernel, n_q=n_q, n_k=n_k, n_v=n_v,
                             norm=norm_gains is not None, rope=rope, q_scale=q_scale)
    return _proj_call(
        kern, x, mod, w, extras, lambda tm: [f(tm) for f in spec_fns],
        jax.ShapeDtypeStruct((bsz, n_tok, n_out), BF),
        lambda tm: pl.BlockSpec((None, tm, n_out), lambda b, i: (b, i, 0)),
        "mixer_qkv_projection")


def _log_sigmoid(x):
    return jnp.minimum(x, 0.0) - jnp.log(1.0 + jnp.exp(-jnp.abs(x)))


def _proj_gla_kernel(x_ref, mod_ref, w_ref, wz_ref, wg_ref, bg_ref, o_ref, g_ref, *, q_scale, n_q):
    hb = _modulate(x_ref, mod_ref, 0, 1).astype(BF)
    n_blocks = o_ref.shape[-1] // LANES
    per_dot = MXU_WIDTH // LANES
    for c0 in range(0, n_blocks, per_dot):
        y2 = _dot(hb, w_ref[:, c0 * LANES:(c0 + per_dot) * LANES])
        if c0 < n_q:
            y2 = y2 * q_scale
        o_ref[:, c0 * LANES:(c0 + per_dot) * LANES] = y2.astype(BF)
    z = _dot(hb, wz_ref[...]).astype(BF)
    n_gate = g_ref.shape[-1]
    for c0 in range(0, n_gate, MXU_WIDTH):
        pre = _dot(z, wg_ref[:, c0:c0 + MXU_WIDTH]) + bg_ref[:, c0:c0 + MXU_WIDTH]
        g_ref[:, c0:c0 + MXU_WIDTH] = _log_sigmoid(pre) * (1.0 / C_GATE_NORMALIZER)


def _proj_gla(x, mod, w_main, w_z, w_gate, b_gate, *, q_scale, n_q):
    bsz, n_tok, _ = x.shape
    n_out, n_gate = w_main.shape[1], w_gate.shape[1]
    extras = [w_z, w_gate, b_gate]
    kern = functools.partial(_proj_gla_kernel, q_scale=q_scale, n_q=n_q)
    return _proj_call(
        kern, x, mod, w_main, extras, lambda tm: [_full_spec(a) for a in extras],
        (jax.ShapeDtypeStruct((bsz, n_tok, n_out), BF), jax.ShapeDtypeStruct((bsz, n_tok, n_gate), F32)),
        lambda tm: (pl.BlockSpec((None, tm, n_out), lambda b, i: (b, i, 0)),
                    pl.BlockSpec((None, tm, n_gate), lambda b, i: (b, i, 0))),
        "gla_projection")


FFN_CHUNK = 256
FFN_TILE = 512
FFN_ROW_PARTS = 2
HALO = 16


def _mix_ffn_kernel(a_ref, ap_ref, an_ref, x_ref, xp_ref, xn_ref, mod_ref, wo_ref, g1_ref, b1_ref,
                    wu_ref, cw_ref, cb_ref, wd_ref, g2_ref, b2_ref, o_ref, act_ref):
    tm = x_ref.shape[0]
    i = pl.program_id(1)
    has_prev = (i > 0).astype(F32)
    has_next = (i < pl.num_programs(1) - 1).astype(F32)
    n_ext = tm + 2 * HALO

    a_ext = jnp.concatenate([a_ref[...], an_ref[...], ap_ref[...]], axis=0)
    x_ext = jnp.concatenate([x_ref[...], xn_ref[...], xp_ref[...]], axis=0)
    he = n_ext // FFN_ROW_PARTS
    ext_starts = list(range(0, n_ext, he))
    y_next = _dot(a_ext[0:he], wo_ref[...])
    x_mid = []
    for j, r in enumerate(ext_starts):
        y = y_next
        if j + 1 < len(ext_starts):
            y_next = _dot(a_ext[r + he:r + 2 * he], wo_ref[...])
        z1 = DEEPNORM_ALPHA * x_ext[r:r + he] + mod_ref[2:3, :] * y
        x_mid.append(_layer_norm(z1, g1_ref[...], b1_ref[...]))
    x_mid = jnp.concatenate(x_mid, axis=0)
    h = x_mid * (1.0 + mod_ref[4:5, :]) + mod_ref[3:4, :]
    hb = jnp.concatenate([h[:tm], h[tm:tm + HALO] * has_next, h[tm + HALO:] * has_prev], axis=0).astype(BF)

    def up(c):
        cg = c * FFN_CHUNK
        return _dot(hb, wu_ref[:, cg:cg + FFN_CHUNK]), _dot(hb, wu_ref[:, FF_DIM + cg:FF_DIM + cg + FFN_CHUNK])

    def conv(u_ext, c0):
        u_dn = pltpu.roll(u_ext, 1, axis=0)[:tm]
        u_up = pltpu.roll(u_ext, n_ext - 1, axis=0)[:tm]
        w0 = cw_ref[0:1, c0:c0 + FFN_CHUNK]
        w1 = cw_ref[1:2, c0:c0 + FFN_CHUNK]
        w2 = cw_ref[2:3, c0:c0 + FFN_CHUNK]
        return cb_ref[:, c0:c0 + FFN_CHUNK] + u_dn * w0 + u_ext[:tm] * w1 + u_up * w2

    for c in range(FF_DIM // FFN_CHUNK):
        cg = c * FFN_CHUNK
        u_gate, u_val = up(c)
        gate = conv(u_gate, cg)
        val = conv(u_val, FF_DIM + cg)
        act_ref[:, cg:cg + FFN_CHUNK] = (gate * _sigmoid(gate) * val).astype(BF)
    hm = tm // FFN_ROW_PARTS
    starts = list(range(0, tm, hm))
    y_next = _dot(act_ref[0:hm, :], wd_ref[...])
    for j, r in enumerate(starts):
        y = y_next
        if j + 1 < len(starts):
            y_next = _dot(act_ref[r + hm:r + 2 * hm, :], wd_ref[...])
        z2 = DEEPNORM_ALPHA * x_mid[r:r + hm] + mod_ref[5:6, :] * y
        o_ref[r:r + hm, :] = _layer_norm(z2, g2_ref[...], b2_ref[...])


def _mix_ffn(a, x, mod, w_o, ln1_g, ln1_b, w_up, conv_w, conv_b, w_down, ln2_g, ln2_b):
    bsz, n_tok, dm = x.shape
    k_in = a.shape[-1]
    tm = FFN_TILE if n_tok % FFN_TILE == 0 else n_tok
    n_halo = n_tok // HALO
    per = tm // HALO
    tile = lambda b, i: (b, i, 0)
    prev = lambda b, i: (b, jnp.maximum(i * per - 1, 0), 0)
    nxt = lambda b, i: (b, jnp.minimum((i + 1) * per, n_halo - 1), 0)
    const2 = lambda b, i: (0, 0)
    resident = functools.partial(pl.BlockSpec, index_map=const2, pipeline_mode=pl.Buffered(1))
    row = lambda v: v.reshape(1, -1)
    return pl.pallas_call(
        _mix_ffn_kernel,
        out_shape=jax.ShapeDtypeStruct((bsz, n_tok, dm), F32),
        grid=(bsz, n_tok // tm),
        in_specs=[
            pl.BlockSpec((None, tm, k_in), tile),
            pl.BlockSpec((None, HALO, k_in), prev),
            pl.BlockSpec((None, HALO, k_in), nxt),
            pl.BlockSpec((None, tm, dm), tile),
            pl.BlockSpec((None, HALO, dm), prev),
            pl.BlockSpec((None, HALO, dm), nxt),
            pl.BlockSpec((None, N_MOD, dm), lambda b, i: (b, 0, 0)),
            resident(w_o.shape),
            pl.BlockSpec((1, dm), const2),
            pl.BlockSpec((1, dm), const2),
            resident(w_up.shape),
            pl.BlockSpec(conv_w.shape, const2),
            pl.BlockSpec((1, 2 * FF_DIM), const2),
            resident(w_down.shape),
            pl.BlockSpec((1, dm), const2),
            pl.BlockSpec((1, dm), const2),
        ],
        out_specs=pl.BlockSpec((None, tm, dm), tile),
        scratch_shapes=[pltpu.VMEM((tm, FF_DIM), BF)],
        compiler_params=_cparams(2),
        name="outproj_convffn_layernorm",
    )(a, a, a, x, x, x, mod, w_o, row(ln1_g), row(ln1_b), w_up, conv_w, row(conv_b), w_down,
      row(ln2_g), row(ln2_b))


ATT_SUB = 512
ATT_TILE = 2048
ATT_BLOCKS = 1


def _attn_tile(n_tok):
    return ATT_TILE if n_tok % ATT_TILE == 0 else n_tok


def _transpose_values(v_ref):
    return v_ref[...].astype(F32).T.astype(BF)


def _scores_t(q, keys, biases_t):
    return [_dot_nt(k, q) if bias is None else _dot_nt(k, q) + bias for k, bias in zip(keys, biases_t)]


def _probs_t(scores):
    m = functools.reduce(jnp.maximum, [jnp.max(s, axis=0, keepdims=True) for s in scores])
    probs = [jnp.exp2(s - m) for s in scores]
    denom = functools.reduce(lambda a, b: a + b, [jnp.sum(p, axis=0, keepdims=True) for p in probs])
    return [p.astype(BF) for p in probs], denom


def _pv_t(probs, denom, values_t):
    out_t = functools.reduce(lambda a, b: a + b, [_dot(vt, p) for p, vt in zip(probs, values_t)])
    return (out_t / denom).T


def _softmax_pv_t(scores, values_t):
    return _pv_t(*_probs_t(scores), values_t)


def _attend_pipelined(items, emit):
    n = len(items)
    scores = {j: _scores_t(*items[j][:3]) for j in range(min(2, n))}
    probs = {0: _probs_t(scores.pop(0))}
    for i in range(n):
        if i + 2 < n:
            scores[i + 2] = _scores_t(*items[i + 2][:3])
        if i + 1 < n:
            probs[i + 1] = _probs_t(scores.pop(i + 1))
        emit(i, _pv_t(*probs.pop(i), items[i][3]))


def _attn_gqa_kernel(q_ref, *rest, n_seg, n_blk):
    k_refs, v_refs, o_ref = rest[:n_seg], rest[n_seg:2 * n_seg], rest[2 * n_seg]
    grp = A_HEADS // A_KV_HEADS
    half = _idiv(_lane_iota(), HEAD_DIM)
    values_t = [_transpose_values(r) for r in v_refs]
    items, place = [], []
    for blk in range(n_blk):
        all_blocks = n_blk * 2 == A_HEADS
        head = (blk if all_blocks else pl.program_id(1) * n_blk + blk) * 2
        kv_pos = (head // grp) % 2
        kv_blk = blk * 2 // grp // 2 if all_blocks else 0
        keys = [r[:, kv_blk * LANES:(kv_blk + 1) * LANES] for r in k_refs]
        values = [vt[kv_blk * LANES:(kv_blk + 1) * LANES, :] for vt in values_t]
        for r0 in range(0, q_ref.shape[0], ATT_SUB):
            q = q_ref[r0:r0 + ATT_SUB, blk * LANES:(blk + 1) * LANES].astype(F32)
            q_sw = pltpu.roll(q, HEAD_DIM, axis=1)
            for e in range(2):
                q_e = jnp.where(kv_pos == e, q, q_sw)
                q_e = jnp.where(half == kv_pos, q_e, 0.0).astype(BF)
                items.append((q_e, keys, [None] * n_seg, values))
                place.append((blk, r0, e, kv_pos))
    outs = {}

    def emit(i, o):
        blk, r0, e, kv_pos = place[i]
        outs[e] = jnp.where(kv_pos == e, o, pltpu.roll(o, HEAD_DIM, axis=1))
        if e == 1:
            o_ref[r0:r0 + ATT_SUB, blk * LANES:(blk + 1) * LANES] = (
                jnp.where(half == 0, outs[0], outs[1]).astype(BF))

    _attend_pipelined(items, emit)


def _blocks_per_step(n_q_tok, n_blocks):
    return n_blocks if n_q_tok < ATT_SUB else ATT_BLOCKS


def _attn_gqa(q_src, kv_srcs):
    bsz, n_q_tok, _ = q_src.shape
    tq = _attn_tile(n_q_tok)
    n_qb = A_HEADS * HEAD_DIM // LANES
    n_kb = A_KV_HEADS * HEAD_DIM // LANES
    per_kb = n_qb // n_kb
    n_seg = len(kv_srcs)
    n_blk = _blocks_per_step(n_q_tok, n_qb)
    assert n_blk == n_qb or per_kb % n_blk == 0
    kv_w = LANES if n_blk < n_qb else n_kb * LANES
    k0, v0 = n_qb * LANES // kv_w, (n_qb + n_kb) * LANES // kv_w
    k_specs = [pl.BlockSpec((None, s.shape[1], kv_w), lambda b, p, i: (b, 0, k0 + p * n_blk // per_kb))
               for s in kv_srcs]
    v_specs = [pl.BlockSpec((None, s.shape[1], kv_w), lambda b, p, i: (b, 0, v0 + p * n_blk // per_kb))
               for s in kv_srcs]
    return pl.pallas_call(
        functools.partial(_attn_gqa_kernel, n_seg=n_seg, n_blk=n_blk),
        out_shape=jax.ShapeDtypeStruct((bsz, n_q_tok, n_qb * LANES), BF),
        grid=(bsz, n_qb // n_blk, n_q_tok // tq),
        in_specs=[pl.BlockSpec((None, tq, n_blk * LANES), lambda b, p, i: (b, i, p))] + k_specs + v_specs,
        out_specs=pl.BlockSpec((None, tq, n_blk * LANES), lambda b, p, i: (b, i, p)),
        compiler_params=_cparams(3),
        name="gqa_attention",
    )(q_src, *kv_srcs, *kv_srcs)


def _attn_diff_kernel(q_ref, lam_ref, subln_ref, *rest, n_seg, lam_init):
    k_refs, v_refs, o_ref = rest[:n_seg], rest[n_seg:2 * n_seg], rest[2 * n_seg]
    lam_v = lam_ref[...]
    lam = (jnp.exp(jnp.sum(lam_v[0:1] * lam_v[1:2], axis=-1, keepdims=True))
           - jnp.exp(jnp.sum(lam_v[2:3] * lam_v[3:4], axis=-1, keepdims=True)) + lam_init)
    half = _idiv(_lane_iota(), HEAD_DIM)
    values_t = [_transpose_values(r) for r in v_refs]
    n_blk = q_ref.shape[1] // LANES
    items, place = [], []
    for blk in range(n_blk):
        lanes = slice(blk * LANES, (blk + 1) * LANES)
        keys = [r[:, lanes] for r in k_refs]
        values = [vt[lanes, :] for vt in values_t]
        for r0 in range(0, q_ref.shape[0], ATT_SUB):
            q = q_ref[r0:r0 + ATT_SUB, lanes]
            for e in range(2):
                items.append((jnp.where(half == e, q, jnp.zeros_like(q)), keys, [None] * n_seg, values))
                place.append((lanes, r0, e))
    outs = {}

    def emit(i, o):
        lanes, r0, e = place[i]
        outs[e] = o
        if e == 1:
            o = outs[0] - lam * outs[1]
            ms = jnp.mean(o * o, axis=-1, keepdims=True)
            o = o * lax.rsqrt(ms + NORM_EPS) * subln_ref[...] * (1.0 - lam_init)
            o_ref[r0:r0 + ATT_SUB, lanes] = o.astype(BF)

    _attend_pipelined(items, emit)


def _attn_diff(q_src, kv_srcs, lam_vecs, subln, lam_init):
    bsz, n_q_tok, _ = q_src.shape
    tq = _attn_tile(n_q_tok)
    nh = B_HEADS
    n_seg = len(kv_srcs)
    n_blk = _blocks_per_step(n_q_tok, nh)
    w = n_blk * LANES
    steps = nh // n_blk
    k_specs = [pl.BlockSpec((None, s.shape[1], w), lambda b, h, i: (b, 0, steps + h)) for s in kv_srcs]
    v_specs = [pl.BlockSpec((None, s.shape[1], w), lambda b, h, i: (b, 0, 2 * steps + h)) for s in kv_srcs]
    return pl.pallas_call(
        functools.partial(_attn_diff_kernel, n_seg=n_seg, lam_init=lam_init),
        out_shape=jax.ShapeDtypeStruct((bsz, n_q_tok, nh * LANES), BF),
        grid=(bsz, steps, n_q_tok // tq),
        in_specs=[pl.BlockSpec((None, tq, w), lambda b, h, i: (b, i, h)),
                  pl.BlockSpec(lam_vecs.shape, lambda b, h, i: (0, 0)),
                  pl.BlockSpec((1, LANES), lambda b, h, i: (0, 0))] + k_specs + v_specs,
        out_specs=pl.BlockSpec((None, tq, w), lambda b, h, i: (b, i, h)),
        compiler_params=_cparams(3),
        name="differential_attention",
    )(q_src, lam_vecs, subln.reshape(1, LANES), *kv_srcs, *kv_srcs)


NB_QROWS = 4
NB_BAND = 12


def _nb_band_start(g, rows):
    return int(np.clip(NB_QROWS * g - NB_QROWS, 0, rows - NB_BAND))


def _nb_config(g, rows):
    n_g = rows // NB_QROWS
    return 0 if g == 0 else (2 if g == n_g - 1 else 1)


def _nb_bias_table(rpb, rows):
    nh = rpb.shape[0]
    wr = min(WIN_R, rows)
    cq = np.arange(GRID_W)
    c0 = np.clip(cq - WIN_C // 2, 0, GRID_W - WIN_C)
    col_in = (cq[None, :] >= c0[:, None]) & (cq[None, :] < c0[:, None] + WIN_C)
    dc_idx = np.clip(cq[None, :] - cq[:, None], -(WIN_C - 1), WIN_C - 1) + WIN_C - 1
    n_dr = 2 * WIN_R - 1
    tiles = rpb[:, :, dc_idx]
    tiles = jnp.where(col_in[None, None], tiles * LOG2E, MASK_VALUE)
    tiles = jnp.concatenate([tiles, jnp.full((nh, 1, GRID_W, GRID_W), MASK_VALUE, F32)], axis=1)
    n_g = rows // NB_QROWS
    sel = np.full((3, NB_QROWS, NB_BAND), n_dr, dtype=np.int32)
    for cfg, g in ((0, 0), (1, 1), (2, n_g - 1)):
        start = _nb_band_start(g, rows)
        for a in range(NB_QROWS):
            r = NB_QROWS * g + a
            r0 = int(np.clip(r - wr // 2, 0, rows - wr))
            for jb in range(NB_BAND):
                kr = start + jb
                if r0 <= kr < r0 + wr:
                    sel[cfg, a, jb] = kr - r + (WIN_R - 1)
    big = tiles[:, sel]
    big = big.transpose(0, 1, 3, 5, 2, 4)
    return big.reshape(nh, 3, NB_BAND * GRID_W, NB_QROWS * GRID_W)


def _attn_nb_kernel(q_ref, kl_ref, vl_ref, kc_ref, vc_ref, bias_ref, o_ref, *, rows):
    half = _idiv(_lane_iota(), HEAD_DIM)
    kc = kc_ref[...]
    vc = _transpose_values(vc_ref)
    vl = _transpose_values(vl_ref)
    tq = NB_QROWS * GRID_W
    tk = NB_BAND * GRID_W
    items = []
    for g in range(rows // NB_QROWS):
        ks = _nb_band_start(g, rows) * GRID_W
        cfg = _nb_config(g, rows)
        q = q_ref[g * tq:(g + 1) * tq, :]
        kb = kl_ref[ks:ks + tk, :]
        vb = vl[:, ks:ks + tk]
        for e in range(2):
            q_e = jnp.where(half == e, q, jnp.zeros_like(q))
            items.append((q_e, [kb, kc], [bias_ref[e, cfg], None], [vb, vc]))
    outs = {}

    def emit(i, o):
        g, e = i // 2, i % 2
        outs[e] = o
        if e == 1:
            o_ref[g * tq:(g + 1) * tq, :] = jnp.where(half == 0, outs[0], outs[1]).astype(BF)

    _attend_pipelined(items, emit)


def _attn_nb(qkv_lat, qkv_ctx, bias):
    bsz, n_tok, _ = qkv_lat.shape
    n_ctx = qkv_ctx.shape[1]
    rows = n_tok // GRID_W
    npair = D_HEADS * HEAD_DIM // LANES
    return pl.pallas_call(
        functools.partial(_attn_nb_kernel, rows=rows),
        out_shape=jax.ShapeDtypeStruct((bsz, n_tok, npair * LANES), BF),
        grid=(npair, bsz),
        in_specs=[
            pl.BlockSpec((None, n_tok, LANES), lambda p, b: (b, 0, p)),
            pl.BlockSpec((None, n_tok, LANES), lambda p, b: (b, 0, npair + p)),
            pl.BlockSpec((None, n_tok, LANES), lambda p, b: (b, 0, 2 * npair + p)),
            pl.BlockSpec((None, n_ctx, LANES), lambda p, b: (b, 0, npair + p)),
            pl.BlockSpec((None, n_ctx, LANES), lambda p, b: (b, 0, 2 * npair + p)),
            pl.BlockSpec((2,) + bias.shape[1:], lambda p, b: (p, 0, 0, 0)),
        ],
        out_specs=pl.BlockSpec((None, n_tok, LANES), lambda p, b: (b, 0, p)),
        compiler_params=_cparams(2),
        name="neighbourhood_attention",
    )(qkv_lat, qkv_lat, qkv_lat, qkv_ctx, qkv_ctx, bias)


def _attn_ctx_kernel(q_ref, k_ref, v_ref, o_ref):
    half = _idiv(_lane_iota(), HEAD_DIM)
    q = q_ref[...]
    zero = jnp.zeros_like(q)
    keys, values = [k_ref[...]], [_transpose_values(v_ref)]
    outs = [_softmax_pv_t(_scores_t(jnp.where(half == e, q, zero), keys, [None]), values) for e in range(2)]
    o_ref[...] = jnp.where(half == 0, outs[0], outs[1]).astype(BF)


def _attn_ctx(qkv_ctx, npair):
    bsz, n_ctx, _ = qkv_ctx.shape
    return pl.pallas_call(
        _attn_ctx_kernel,
        out_shape=jax.ShapeDtypeStruct((bsz, n_ctx, npair * LANES), BF),
        grid=(bsz, npair),
        in_specs=[
            pl.BlockSpec((None, n_ctx, LANES), lambda b, p: (b, 0, p)),
            pl.BlockSpec((None, n_ctx, LANES), lambda b, p: (b, 0, npair + p)),
            pl.BlockSpec((None, n_ctx, LANES), lambda b, p: (b, 0, 2 * npair + p)),
        ],
        out_specs=pl.BlockSpec((None, n_ctx, LANES), lambda b, p: (b, 0, p)),
        compiler_params=_cparams(2),
        name="context_attention",
    )(qkv_ctx, qkv_ctx, qkv_ctx)


def _gla_geometry(rev):
    row = lax.broadcasted_iota(jnp.int32, (C_CHUNK, 1), 0)
    col = lax.broadcasted_iota(jnp.int32, (1, C_CHUNK), 1)
    sub = _idiv(row, C_SUB)
    col_sub = _idiv(col, C_SUB)
    if rev:
        tri, cross, dist = col >= row, col_sub > sub, col - row
    else:
        tri, cross, dist = col <= row, col_sub < sub, row - col
    off_blk = jnp.where(cross, sub, -1)
    dmat = jnp.where(tri, jnp.where(col_sub == sub, dist, -1), -1)
    return tri.astype(BF), sub, off_blk, dmat


def _gla_decay(g, geom):
    return _split_dot_left(geom[0], g) * LOG2E


def _gla_factors(q, k, b, rev, geom):
    sub = geom[1]
    nsub = C_CHUNK // C_SUB
    b_tot = b[0:1] if rev else b[C_CHUNK - 1:C_CHUNK]
    blocks = range(nsub - 1) if rev else range(1, nsub)
    bounds = {}
    ref_rows = jnp.zeros_like(b)
    for a in blocks:
        r = (a + 1) * C_SUB if rev else a * C_SUB - 1
        bounds[a] = b[r:r + 1]
        ref_rows = jnp.where(sub == a, bounds[a], ref_rows)
    q_in = (q * jnp.exp2(b)).astype(BF)
    q_t = (q * jnp.exp2(b - ref_rows)).astype(BF)
    k_ts = {a: (k * jnp.exp2(jnp.minimum(bounds[a] - b, 0.0))).astype(BF) for a in blocks}
    k_end = (k * jnp.exp2(b_tot - b)).astype(BF)
    return q_in, q_t, k_ts, k_end, jnp.exp2(b_tot)


def _gla_cross(factors, v, geom):
    _, q_t, k_ts, k_end, _ = factors
    a_mat = jnp.zeros((C_CHUNK, C_CHUNK), F32)
    for a, k_t in k_ts.items():
        a_mat = jnp.where(geom[2] == a, _dot_nt(q_t, k_t), a_mat)
    return a_mat, _dot_tn(v, k_end)


def _gla_same(q, k, b, a_mat, rev, geom):
    nsub, half = C_CHUNK // C_SUB, C_SUB // 2

    def halves(x):
        x = x.reshape(nsub, C_SUB, x.shape[-1])
        return x[:, :half], x[:, half:]

    first, second = zip(*[halves(x) for x in (q, k, b, a_mat, geom[3])])
    near, far = (second, first) if rev else (first, second)
    (q_n, k_n, b_n, a_n, d_n), (q_f, k_f, b_f, a_f, d_f) = near, far
    r = lax.broadcasted_iota(jnp.int32, (1, half, 1), 1)

    def rot(x, d):
        shift = (half - d) % half if rev else d
        return x if shift == 0 else pltpu.roll(x, shift, axis=1)

    def diag(qq, kk, bb, b_partner):
        return jnp.sum(qq * kk * jnp.exp2(bb - b_partner), axis=-1, keepdims=True)

    a_n = jnp.where(d_n == 0, jnp.sum(q_n * k_n, axis=-1, keepdims=True), a_n)
    a_f = jnp.where(d_f == 0, jnp.sum(q_f * k_f, axis=-1, keepdims=True), a_f)
    for d in range(1, half):
        kn, bn, kf, bf = rot(k_n, d), rot(b_n, d), rot(k_f, d), rot(b_f, d)
        own = (r + d < half) if rev else (r >= d)
        a_n = jnp.where(d_n == d, diag(q_n, kn, b_n, bn), a_n)
        a_f = jnp.where(d_f == d, diag(q_f, jnp.where(own, kf, kn), b_f, jnp.where(own, bf, bn)), a_f)
    for d in range(half, C_SUB):
        a_f = jnp.where(d_f == d, diag(q_f, rot(k_n, d - half), b_f, rot(b_n, d - half)), a_f)
    lo, hi = (a_f, a_n) if rev else (a_n, a_f)
    return jnp.concatenate([lo, hi], axis=1).reshape(C_CHUNK, C_CHUNK)


def _gla_advance(factors, a_mat, v, st, increment):
    q_in, _, _, _, decay = factors
    o = _dot_nt(q_in, st.astype(BF)) + _dot(a_mat.astype(BF), v)
    return o, st * decay + increment


def _gla_kernel(ql_ref, kl_ref, vl_ref, ogl_ref, gfl_ref, gbl_ref,
                qc_ref, kc_ref, vc_ref, ogc_ref, gfc_ref, gbc_ref, ng_ref,
                ol_ref, oc_ref, fl_ref, bl_ref, fc_ref, bc_ref, *, with_ctx_out):
    dk = ql_ref.shape[-1]
    dv = vl_ref.shape[-1]
    n_lat = ql_ref.shape[0] // C_CHUNK
    n_ctx = qc_ref.shape[0] // C_CHUNK
    geom_f = _gla_geometry(False)
    geom_b = _gla_geometry(True)

    def load(q_ref, k_ref, v_ref, g_ref, c):
        sl = pl.ds(pl.multiple_of(c * C_CHUNK, C_CHUNK), C_CHUNK)
        return q_ref[sl, :].astype(F32), k_ref[sl, :].astype(F32), v_ref[sl, :], g_ref[sl, :]

    def chunk_at(c):
        return pl.ds(pl.multiple_of(c * C_CHUNK, C_CHUNK), C_CHUNK)

    def scan_both(refs_f, refs_b, f_ref, b_ref, n_chunks, states):
        group = min(GLA_GROUP, n_chunks)

        def body(step, carry):
            st_f, st_b = carry
            work = []
            for u in range(group):
                j = step * group + u
                work.append((False, geom_f, refs_f, j, f_ref))
                work.append((True, geom_b, refs_b, n_chunks - 1 - j, b_ref))
            data = [load(*refs, c) for _, _, refs, c, _ in work]
            decays = [_gla_decay(d[3], w[1]) for d, w in zip(data, work)]
            factors = [_gla_factors(d[0], d[1], b, w[0], w[1]) for d, b, w in zip(data, decays, work)]
            cross = [_gla_cross(f, d[2], w[1]) for f, d, w in zip(factors, data, work)]
            a_mats = [_gla_same(d[0], d[1], b, x[0], w[0], w[1])
                      for d, b, x, w in zip(data, decays, cross, work)]
            for idx, (rev, _, _, c, out_ref) in enumerate(work):
                st = st_b if rev else st_f
                o, st = _gla_advance(factors[idx], a_mats[idx], data[idx][2], st, cross[idx][1])
                out_ref[chunk_at(c), :] = o
                if rev:
                    st_b = st
                else:
                    st_f = st
            return st_f, st_b
        return lax.fori_loop(0, n_chunks // group, body, states)

    def finish(f_ref, b_ref, og_ref, out_ref):
        def body(i, carry):
            sl = pl.ds(pl.multiple_of(i * GLA_FINISH_ROWS, GLA_FINISH_ROWS), GLA_FINISH_ROWS)
            tot = f_ref[sl, :] + b_ref[sl, :]
            ms = jnp.mean(tot * tot, axis=-1, keepdims=True)
            y = tot * lax.rsqrt(ms + NORM_EPS) * ng_ref[...]
            og = og_ref[sl, :].astype(F32)
            out_ref[sl, :] = (y * (og * _sigmoid(og))).astype(BF)
            return carry
        lax.fori_loop(0, f_ref.shape[0] // GLA_FINISH_ROWS, body, 0)

    zero = jnp.zeros((dv, dk), F32)
    lat_f = (ql_ref, kl_ref, vl_ref, gfl_ref)
    lat_b = (ql_ref, kl_ref, vl_ref, gbl_ref)
    ctx_f = (qc_ref, kc_ref, vc_ref, gfc_ref)
    ctx_b = (qc_ref, kc_ref, vc_ref, gbc_ref)
    states = scan_both(ctx_f, ctx_b, fc_ref, bc_ref, n_ctx, (zero, zero))
    scan_both(lat_f, lat_b, fl_ref, bl_ref, n_lat, states)
    finish(fl_ref, bl_ref, ogl_ref, ol_ref)
    if with_ctx_out:
        finish(fc_ref, bc_ref, ogc_ref, oc_ref)
    else:
        oc_ref[...] = jnp.zeros_like(oc_ref)


def _gla(proj_lat, gates_lat, proj_ctx, gates_ctx, norm_g, with_ctx_out):
    bsz, n_lat, _ = proj_lat.shape
    n_ctx = proj_ctx.shape[1]
    nh = C_HEADS
    dk = 512 // nh
    dv = 1024 // nh
    kb, vb = 512 // dk, 1024 // dv

    def specs(n_tok):
        return [
            pl.BlockSpec((None, n_tok, dk), lambda b, h: (b, 0, h)),
            pl.BlockSpec((None, n_tok, dk), lambda b, h: (b, 0, kb + h)),
            pl.BlockSpec((None, n_tok, dv), lambda b, h: (b, 0, (2 * kb * dk) // dv + h)),
            pl.BlockSpec((None, n_tok, dv), lambda b, h: (b, 0, (2 * kb * dk) // dv + vb + h)),
            pl.BlockSpec((None, n_tok, dk), lambda b, h: (b, 0, h)),
            pl.BlockSpec((None, n_tok, dk), lambda b, h: (b, 0, kb + h)),
        ]

    out_l, out_c = pl.pallas_call(
        functools.partial(_gla_kernel, with_ctx_out=with_ctx_out),
        out_shape=(jax.ShapeDtypeStruct((bsz, n_lat, nh * dv), BF),
                   jax.ShapeDtypeStruct((bsz, n_ctx, nh * dv), BF)),
        grid=(bsz, nh),
        in_specs=specs(n_lat) + specs(n_ctx) + [pl.BlockSpec((1, dv), lambda b, h: (0, 0))],
        out_specs=(pl.BlockSpec((None, n_lat, dv), lambda b, h: (b, 0, h)),
                   pl.BlockSpec((None, n_ctx, dv), lambda b, h: (b, 0, h))),
        scratch_shapes=[pltpu.VMEM((n_lat, dv), F32), pltpu.VMEM((n_lat, dv), F32),
                        pltpu.VMEM((n_ctx, dv), F32), pltpu.VMEM((n_ctx, dv), F32)],
        compiler_params=_cparams(2),
        name="gated_linear_attention",
    )(proj_lat, proj_lat, proj_lat, proj_lat, gates_lat, gates_lat,
      proj_ctx, proj_ctx, proj_ctx, proj_ctx, gates_ctx, gates_ctx, norm_g.reshape(1, dv))
    return out_l, out_c


def kernel(x, c, ctx, c_ctx, mod_w, mod_b, ln1_g, ln1_b, ffn_w_up, ffn_conv_w, ffn_conv_b, ffn_w_down, ln2_g, ln2_b, a_w_qkv, a_q_norm, a_k_norm, a_w_o, b_w_qkv, b_lambda_q1, b_lambda_k1, b_lambda_q2, b_lambda_k2, b_subln, b_w_o, c_w_in, c_w_gate_fwd, c_b_gate_fwd, c_w_gate_bwd, c_b_gate_bwd, c_norm, c_w_o, d_w_qkv, d_rpb, d_w_o):
    bsz, n_lat, dm = x.shape
    n_ctx = ctx.shape[1]
    scale = HEAD_DIM ** -0.5 * LOG2E

    pad_rows = (-(bsz + 1)) % 8
    cond = jnp.concatenate([c, c_ctx[None, :], jnp.zeros((pad_rows, dm), F32)], axis=0)
    mod_all = _modulation(cond, mod_w, mod_b)

    x_lat, x_ctx = x, ctx
    for i in range(DEPTH):
        kind, j = i % N_MIXERS, i // N_MIXERS
        last = i == DEPTH - 1
        m_lat = mod_all[i, :bsz].reshape(bsz, N_MOD, dm)
        m_ctx = jnp.broadcast_to(mod_all[i, bsz].reshape(1, N_MOD, dm), (bsz, N_MOD, dm))

        if kind == 0:
            w = a_w_qkv[j].astype(BF)
            gains = (a_q_norm[j], a_k_norm[j])
            nq, nk = A_HEADS * HEAD_DIM // LANES, A_KV_HEADS * HEAD_DIM // LANES
            p_lat = _proj_qkv(x_lat, m_lat, w, n_q=nq, n_k=nk, n_v=nk, q_scale=scale, norm_gains=gains, rope=True)
            p_ctx = _proj_qkv(x_ctx, m_ctx, w, n_q=nq, n_k=nk, n_v=nk, q_scale=scale, norm_gains=gains)
            a_lat = _attn_gqa(p_lat, [p_ctx, p_lat])
            a_ctx = None if last else _attn_gqa(p_ctx, [p_ctx])
            w_o = a_w_o[j]
        elif kind == 1:
            w = b_w_qkv[j].astype(BF)
            nb = 2 * B_HEADS * HEAD_DIM // LANES
            p_lat = _proj_qkv(x_lat, m_lat, w, n_q=nb, n_k=nb, n_v=nb, q_scale=scale, rope=True)
            p_ctx = _proj_qkv(x_ctx, m_ctx, w, n_q=nb, n_k=nb, n_v=nb, q_scale=scale)
            lam_vecs = jnp.stack([b_lambda_q1[j], b_lambda_k1[j], b_lambda_q2[j], b_lambda_k2[j]])
            lam_init = 0.8 - 0.6 * math.exp(-0.3 * i)
            a_lat = _attn_diff(p_lat, [p_ctx, p_lat], lam_vecs, b_subln[j], lam_init)
            a_ctx = None if last else _attn_diff(p_ctx, [p_ctx], lam_vecs, b_subln[j], lam_init)
            w_o = b_w_o[j]
        elif kind == 2:
            w_in = c_w_in[j]
            n_main = w_in.shape[1] - 2 * C_GATE_RANK
            w_main = w_in[:, :n_main].astype(BF)
            w_z = jnp.pad(w_in[:, n_main:], ((0, 0), (0, LANES - 2 * C_GATE_RANK))).astype(BF)
            kd = c_w_gate_fwd.shape[-1]
            w_gate = jnp.zeros((LANES, 2 * kd), F32)
            w_gate = w_gate.at[:C_GATE_RANK, :kd].set(c_w_gate_fwd[j])
            w_gate = w_gate.at[C_GATE_RANK:2 * C_GATE_RANK, kd:].set(c_w_gate_bwd[j]).astype(BF)
            b_gate = jnp.concatenate([c_b_gate_fwd[j], c_b_gate_bwd[j]]).reshape(1, 2 * kd)
            q_scale = (kd // C_HEADS) ** -0.5
            nq = kd // LANES
            p_lat, g_lat = _proj_gla(x_lat, m_lat, w_main, w_z, w_gate, b_gate, q_scale=q_scale, n_q=nq)
            p_ctx, g_ctx = _proj_gla(x_ctx, m_ctx, w_main, w_z, w_gate, b_gate, q_scale=q_scale, n_q=nq)
            a_lat, a_ctx = _gla(p_lat, g_lat, p_ctx, g_ctx, c_norm[j], not last)
            w_o = c_w_o[j]
        else:
            w = d_w_qkv[j].astype(BF)
            nb = D_HEADS * HEAD_DIM // LANES
            p_lat = _proj_qkv(x_lat, m_lat, w, n_q=nb, n_k=nb, n_v=nb, q_scale=scale)
            p_ctx = _proj_qkv(x_ctx, m_ctx, w, n_q=nb, n_k=nb, n_v=nb, q_scale=scale)
            bias = _nb_bias_table(d_rpb[j], n_lat // GRID_W)
            a_lat = _attn_nb(p_lat, p_ctx, bias)
            a_ctx = None if last else _attn_ctx(p_ctx, nb)
            w_o = d_w_o[j]

        w_o = w_o.astype(BF)
        w_up = ffn_w_up[i].astype(BF)
        w_dn = ffn_w_down[i].astype(BF)
        ffn = (w_o, ln1_g[i], ln1_b[i], w_up, ffn_conv_w[i], ffn_conv_b[i], w_dn, ln2_g[i], ln2_b[i])
        x_lat = _mix_ffn(a_lat, x_lat, m_lat, *ffn)
        if not last:
            x_ctx = _mix_ffn(a_ctx, x_ctx, m_ctx, *ffn)
    return x_lat
```

```python
import functools
import math

import jax
import jax.numpy as jnp
import numpy as np
from jax import lax
from jax.experimental import pallas as pl
from jax.experimental.pallas import tpu as pltpu

F32 = jnp.float32
BF = jnp.bfloat16

DEPTH = 4
N_MIXERS = 4
N_MOD = 6
GRID_W = 64
ROPE_THETA = 10000.0
NORM_EPS = 1e-6
LN_EPS = 1e-5
HEAD_DIM = 64
A_HEADS, A_KV_HEADS = 16, 4
B_HEADS = 8
C_HEADS = 4
C_GATE_RANK = 16
C_GATE_NORMALIZER = 16.0
C_CHUNK = 64
C_SUB = 16
GLA_FINISH_ROWS = 256
GLA_GROUP = 8
D_HEADS = 16
WIN_R, WIN_C = 8, 16
FF_DIM = 2816
DEEPNORM_ALPHA = (2 * DEPTH) ** 0.25
MASK_VALUE = -1e30
LOG2E = math.log2(math.e)

LANES = 128
MXU_WIDTH = 256
VMEM_LIMIT_BYTES = 56 * 1024 * 1024


def _cparams(n_axes):
    return pltpu.CompilerParams(
        dimension_semantics=("arbitrary",) * n_axes,
        vmem_limit_bytes=VMEM_LIMIT_BYTES)


def _dot(a, b):
    return jnp.dot(a, b, preferred_element_type=F32)


def _dot_nt(a, b):
    return lax.dot_general(a, b, (((1,), (1,)), ((), ())), preferred_element_type=F32)


def _dot_tn(a, b):
    return lax.dot_general(a, b, (((0,), (0,)), ((), ())), preferred_element_type=F32)


def _split_dot(x, m_bf16):
    hi = x.astype(BF)
    lo = (x - hi.astype(F32)).astype(BF)
    return _dot(hi, m_bf16) + _dot(lo, m_bf16)


def _split_dot_left(m_bf16, x):
    hi = x.astype(BF)
    lo = (x - hi.astype(F32)).astype(BF)
    return _dot(m_bf16, hi) + _dot(m_bf16, lo)


def _sigmoid(x):
    return 1.0 / (1.0 + jnp.exp(-x))


def _modulate(x_ref, mod_ref, shift_row, scale_row):
    x = x_ref[...]
    return x * (1.0 + mod_ref[scale_row:scale_row + 1, :]) + mod_ref[shift_row:shift_row + 1, :]


def _layer_norm(z, g, b):
    mu = jnp.mean(z, axis=-1, keepdims=True)
    zc = z - mu
    var = jnp.mean(zc * zc, axis=-1, keepdims=True)
    return zc * lax.rsqrt(var + LN_EPS) * g + b


def _rope(y, cos_t, sin_t, lo_half):
    swap = jnp.where(lo_half, pltpu.roll(y, LANES - 32, axis=1), pltpu.roll(y, 32, axis=1))
    return y * cos_t + swap * sin_t


def _idiv(x, n):
    return lax.shift_right_logical(x, int(n).bit_length() - 1)


def _imod(x, n):
    return x & (n - 1)


def _lane_iota(shape=(1, LANES)):
    return lax.broadcasted_iota(jnp.int32, shape, len(shape) - 1)


def _mod_kernel(cond_ref, w_ref, b_ref, o_ref):
    cnd = cond_ref[...]
    act = cnd * _sigmoid(cnd)
    o_ref[...] = _dot(act.astype(BF), w_ref[...].astype(BF)) + b_ref[...]


def _modulation(cond, mod_w, mod_b):
    depth, dm, n = mod_w.shape
    rows = cond.shape[0]
    tn = 1536
    return pl.pallas_call(
        _mod_kernel,
        out_shape=jax.ShapeDtypeStruct((depth, rows, n), F32),
        grid=(depth, n // tn),
        in_specs=[
            pl.BlockSpec((rows, dm), lambda i, j: (0, 0)),
            pl.BlockSpec((None, dm, tn), lambda i, j: (i, 0, j)),
            pl.BlockSpec((None, 1, tn), lambda i, j: (i, 0, j)),
        ],
        out_specs=pl.BlockSpec((None, rows, tn), lambda i, j: (i, 0, j)),
        compiler_params=_cparams(2),
        name="adaln_modulation",
    )(cond, mod_w, mod_b.reshape(depth, 1, n))


def _row_tile(n_tok):
    return 512 if n_tok % 512 == 0 else n_tok


def _proj_call(kernel, x, mod, w, extras, extra_specs, out_shapes, out_specs, name):
    bsz, n_tok, dm = x.shape
    tm = _row_tile(n_tok)
    in_specs = [
        pl.BlockSpec((None, tm, dm), lambda b, i: (b, i, 0)),
        pl.BlockSpec((None, N_MOD, dm), lambda b, i: (b, 0, 0)),
        pl.BlockSpec(w.shape, lambda b, i: (0, 0)),
    ] + list(extra_specs(tm))
    return pl.pallas_call(
        kernel,
        out_shape=out_shapes,
        grid=(bsz, n_tok // tm),
        in_specs=in_specs,
        out_specs=out_specs(tm),
        compiler_params=_cparams(2),
        name=name,
    )(x, mod, w, *extras)


def _full_spec(arr):
    nd = arr.ndim
    return pl.BlockSpec(arr.shape, lambda b, i: (0,) * nd)


def _seg_matrix():
    idx = np.arange(LANES) // HEAD_DIM
    return jnp.asarray((idx[:, None] == idx[None, :]).astype(np.float32) / HEAD_DIM, dtype=BF)


def _rope_tables(n_tok):
    t = jnp.arange(n_tok)
    row = (t // GRID_W).astype(F32)
    col = (t % GRID_W).astype(F32)
    n_freq = HEAD_DIM // 4
    inv = 1.0 / (ROPE_THETA ** (jnp.arange(n_freq, dtype=F32) / n_freq))
    ang = jnp.concatenate([row[:, None] * inv, col[:, None] * inv], axis=-1)
    cos, sin = jnp.cos(ang), jnp.sin(ang)
    cos_t = jnp.concatenate([cos, cos, cos, cos], axis=-1)
    sin_t = jnp.concatenate([-sin, sin, -sin, sin], axis=-1)
    return cos_t, sin_t


def _proj_qkv_kernel(x_ref, mod_ref, w_ref, *rest, n_q, n_k, n_v, norm, rope, q_scale):
    rest = list(rest)
    seg_ref = gq_ref = gk_ref = cos_ref = sin_ref = None
    if norm:
        seg_ref, gq_ref, gk_ref = rest[:3]
        rest = rest[3:]
    if rope:
        cos_ref, sin_ref = rest[:2]
        rest = rest[2:]
    (o_ref,) = rest
    hb = _modulate(x_ref, mod_ref, 0, 1).astype(BF)
    lo_half = _imod(_lane_iota(), HEAD_DIM) < (HEAD_DIM // 2)
    n_blocks = n_q + n_k + n_v
    per_dot = MXU_WIDTH // LANES
    ys = []
    for c0 in range(0, n_blocks, per_dot):
        nb = min(per_dot, n_blocks - c0)
        y2 = _dot(hb, w_ref[:, c0 * LANES:(c0 + nb) * LANES])
        ys += [y2[:, j * LANES:(j + 1) * LANES] for j in range(nb)]
    mean_sq = [_split_dot(y * y, seg_ref[...]) if norm and c < n_q + n_k else None for c, y in enumerate(ys)]
    for c, y in enumerate(ys):
        if c < n_q + n_k:
            if norm:
                gain = gq_ref[...] if c < n_q else gk_ref[...]
                y = y * lax.rsqrt(mean_sq[c] + NORM_EPS) * gain
            if rope:
                y = _rope(y, cos_ref[...], sin_ref[...], lo_half)
            if c < n_q and q_scale != 1.0:
                y = y * q_scale
        o_ref[:, c * LANES:(c + 1) * LANES] = y.astype(BF)


def _proj_qkv(x, mod, w, *, n_q, n_k, n_v, q_scale, norm_gains=None, rope=False):
    bsz, n_tok, _ = x.shape
    n_out = w.shape[1]
    extras, spec_fns = [], []
    if norm_gains is not None:
        gq, gk = norm_gains
        extras += [_seg_matrix(), jnp.tile(gq, 2).reshape(1, LANES), jnp.tile(gk, 2).reshape(1, LANES)]
        spec_fns += [lambda tm, a=a: _full_spec(a) for a in extras]
    if rope:
        cos_t, sin_t = _rope_tables(n_tok)
        extras += [cos_t, sin_t]
        spec_fns += [lambda tm: pl.BlockSpec((tm, LANES), lambda b, i: (i, 0))] * 2
    kern = functools.partial(_proj_qkv_kernel, n_q=n_q, n_k=n_k, n_v=n_v,
                             norm=norm_gains is not None, rope=rope, q_scale=q_scale)
    return _proj_call(
        kern, x, mod, w, extras, lambda tm: [f(tm) for f in spec_fns],
        jax.ShapeDtypeStruct((bsz, n_tok, n_out), BF),
        lambda tm: pl.BlockSpec((None, tm, n_out), lambda b, i: (b, i, 0)),
        "mixer_qkv_projection")


def _log_sigmoid(x):
    return jnp.minimum(x, 0.0) - jnp.log(1.0 + jnp.exp(-jnp.abs(x)))


def _proj_gla_kernel(x_ref, mod_ref, w_ref, wz_ref, wg_ref, bg_ref, o_ref, g_ref, *, q_scale, n_q):
    hb = _modulate(x_ref, mod_ref, 0, 1).astype(BF)
    n_blocks = o_ref.shape[-1] // LANES
    per_dot = MXU_WIDTH // LANES
    for c0 in range(0, n_blocks, per_dot):
        y2 = _dot(hb, w_ref[:, c0 * LANES:(c0 + per_dot) * LANES])
        if c0 < n_q:
            y2 = y2 * q_scale
        o_ref[:, c0 * LANES:(c0 + per_dot) * LANES] = y2.astype(BF)
    z = _dot(hb, wz_ref[...]).astype(BF)
    n_gate = g_ref.shape[-1]
    for c0 in range(0, n_gate, MXU_WIDTH):
        pre = _dot(z, wg_ref[:, c0:c0 + MXU_WIDTH]) + bg_ref[:, c0:c0 + MXU_WIDTH]
        g_ref[:, c0:c0 + MXU_WIDTH] = _log_sigmoid(pre) * (1.0 / C_GATE_NORMALIZER)


def _proj_gla(x, mod, w_main, w_z, w_gate, b_gate, *, q_scale, n_q):
    bsz, n_tok, _ = x.shape
    n_out, n_gate = w_main.shape[1], w_gate.shape[1]
    extras = [w_z, w_gate, b_gate]
    kern = functools.partial(_proj_gla_kernel, q_scale=q_scale, n_q=n_q)
    return _proj_call(
        kern, x, mod, w_main, extras, lambda tm: [_full_spec(a) for a in extras],
        (jax.ShapeDtypeStruct((bsz, n_tok, n_out), BF), jax.ShapeDtypeStruct((bsz, n_tok, n_gate), F32)),
        lambda tm: (pl.BlockSpec((None, tm, n_out), lambda b, i: (b, i, 0)),
                    pl.BlockSpec((None, tm, n_gate), lambda b, i: (b, i, 0))),
        "gla_projection")


FFN_CHUNK = 256
FFN_TILE = 512
FFN_ROW_PARTS = 2
HALO = 16


def _mix_ffn_kernel(a_ref, ap_ref, an_ref, x_ref, xp_ref, xn_ref, mod_ref, wo_ref, g1_ref, b1_ref,
                    wu_ref, cw_ref, cb_ref, wd_ref, g2_ref, b2_ref, o_ref, act_ref):
    tm = x_ref.shape[0]
    i = pl.program_id(1)
    has_prev = (i > 0).astype(F32)
    has_next = (i < pl.num_programs(1) - 1).astype(F32)
    n_ext = tm + 2 * HALO

    a_ext = jnp.concatenate([a_ref[...], an_ref[...], ap_ref[...]], axis=0)
    x_ext = jnp.concatenate([x_ref[...], xn_ref[...], xp_ref[...]], axis=0)
    he = n_ext // FFN_ROW_PARTS
    ext_starts = list(range(0, n_ext, he))
    y_next = _dot(a_ext[0:he], wo_ref[...])
    x_mid = []
    for j, r in enumerate(ext_starts):
        y = y_next
        if j + 1 < len(ext_starts):
            y_next = _dot(a_ext[r + he:r + 2 * he], wo_ref[...])
        z1 = DEEPNORM_ALPHA * x_ext[r:r + he] + mod_ref[2:3, :] * y
        x_mid.append(_layer_norm(z1, g1_ref[...], b1_ref[...]))
    x_mid = jnp.concatenate(x_mid, axis=0)
    h = x_mid * (1.0 + mod_ref[4:5, :]) + mod_ref[3:4, :]
    hb = jnp.concatenate([h[:tm], h[tm:tm + HALO] * has_next, h[tm + HALO:] * has_prev], axis=0).astype(BF)

    def up(c):
        cg = c * FFN_CHUNK
        return _dot(hb, wu_ref[:, cg:cg + FFN_CHUNK]), _dot(hb, wu_ref[:, FF_DIM + cg:FF_DIM + cg + FFN_CHUNK])

    def conv(u_ext, c0):
        u_dn = pltpu.roll(u_ext, 1, axis=0)[:tm]
        u_up = pltpu.roll(u_ext, n_ext - 1, axis=0)[:tm]
        w0 = cw_ref[0:1, c0:c0 + FFN_CHUNK]
        w1 = cw_ref[1:2, c0:c0 + FFN_CHUNK]
        w2 = cw_ref[2:3, c0:c0 + FFN_CHUNK]
        return cb_ref[:, c0:c0 + FFN_CHUNK] + u_dn * w0 + u_ext[:tm] * w1 + u_up * w2

    for c in range(FF_DIM // FFN_CHUNK):
        cg = c * FFN_CHUNK
        u_gate, u_val = up(c)
        gate = conv(u_gate, cg)
        val = conv(u_val, FF_DIM + cg)
        act_ref[:, cg:cg + FFN_CHUNK] = (gate * _sigmoid(gate) * val).astype(BF)
    hm = tm // FFN_ROW_PARTS
    starts = list(range(0, tm, hm))
    y_next = _dot(act_ref[0:hm, :], wd_ref[...])
    for j, r in enumerate(starts):
        y = y_next
        if j + 1 < len(starts):
            y_next = _dot(act_ref[r + hm:r + 2 * hm, :], wd_ref[...])
        z2 = DEEPNORM_ALPHA * x_mid[r:r + hm] + mod_ref[5:6, :] * y
        o_ref[r:r + hm, :] = _layer_norm(z2, g2_ref[...], b2_ref[...])


def _mix_ffn(a, x, mod, w_o, ln1_g, ln1_b, w_up, conv_w, conv_b, w_down, ln2_g, ln2_b):
    bsz, n_tok, dm = x.shape
    k_in = a.shape[-1]
    tm = FFN_TILE if n_tok % FFN_TILE == 0 else n_tok
    n_halo = n_tok // HALO
    per = tm // HALO
    tile = lambda b, i: (b, i, 0)
    prev = lambda b, i: (b, jnp.maximum(i * per - 1, 0), 0)
    nxt = lambda b, i: (b, jnp.minimum((i + 1) * per, n_halo - 1), 0)
    const2 = lambda b, i: (0, 0)
    resident = functools.partial(pl.BlockSpec, index_map=const2, pipeline_mode=pl.Buffered(1))
    row = lambda v: v.reshape(1, -1)
    return pl.pallas_call(
        _mix_ffn_kernel,
        out_shape=jax.ShapeDtypeStruct((bsz, n_tok, dm), F32),
        grid=(bsz, n_tok // tm),
        in_specs=[
            pl.BlockSpec((None, tm, k_in), tile),
            pl.BlockSpec((None, HALO, k_in), prev),
            pl.BlockSpec((None, HALO, k_in), nxt),
            pl.BlockSpec((None, tm, dm), tile),
            pl.BlockSpec((None, HALO, dm), prev),
            pl.BlockSpec((None, HALO, dm), nxt),
            pl.BlockSpec((None, N_MOD, dm), lambda b, i: (b, 0, 0)),
            resident(w_o.shape),
            pl.BlockSpec((1, dm), const2),
            pl.BlockSpec((1, dm), const2),
            resident(w_up.shape),
            pl.BlockSpec(conv_w.shape, const2),
            pl.BlockSpec((1, 2 * FF_DIM), const2),
            resident(w_down.shape),
            pl.BlockSpec((1, dm), const2),
            pl.BlockSpec((1, dm), const2),
        ],
        out_specs=pl.BlockSpec((None, tm, dm), tile),
        scratch_shapes=[pltpu.VMEM((tm, FF_DIM), BF)],
        compiler_params=_cparams(2),
        name="outproj_convffn_layernorm",
    )(a, a, a, x, x, x, mod, w_o, row(ln1_g), row(ln1_b), w_up, conv_w, row(conv_b), w_down,
      row(ln2_g), row(ln2_b))


ATT_SUB = 512
ATT_TILE = 2048
ATT_BLOCKS = 1


def _attn_tile(n_tok):
    return ATT_TILE if n_tok % ATT_TILE == 0 else n_tok


ATT_KEYS = 1024


def _key_blocks(n_keys):
    return [slice(r, min(r + ATT_KEYS, n_keys)) for r in range(0, n_keys, ATT_KEYS)]


def _transpose_values(v_ref):
    return v_ref[...].astype(F32).T.astype(BF)


def _scores_t(q, keys, biases_t):
    return [_dot_nt(k, q) if bias is None else _dot_nt(k, q) + bias for k, bias in zip(keys, biases_t)]


def _probs_t(scores):
    m = functools.reduce(jnp.maximum, [jnp.max(s, axis=0, keepdims=True) for s in scores])
    probs = [jnp.exp2(s - m) for s in scores]
    denom = functools.reduce(lambda a, b: a + b, [jnp.sum(p, axis=0, keepdims=True) for p in probs])
    return [p.astype(BF) for p in probs], denom


def _pv_t(probs, denom, values_t):
    out_t = functools.reduce(lambda a, b: a + b, [_dot(vt, p) for p, vt in zip(probs, values_t)])
    return (out_t / denom).T


def _softmax_pv_t(scores, values_t):
    return _pv_t(*_probs_t(scores), values_t)


def _attend_pipelined(items, emit):
    n = len(items)

    def n_seg(j):
        return len(items[j][1]) if j < n else 0

    def score_seg(j, s):
        q, keys, biases_t, _ = items[j]
        sc = _dot_nt(keys[s], q)
        return sc if biases_t[s] is None else sc + biases_t[s]

    def col_max(j):
        return functools.reduce(jnp.maximum, [jnp.max(s, axis=0, keepdims=True) for s in scores[j]])

    scores = {j: [score_seg(j, s) for s in range(n_seg(j))] for j in range(min(2, n))}
    probs = {0: _probs_t(scores.pop(0))}
    for i in range(n):
        m_next = col_max(i + 1) if i + 1 < n else None
        new_scores, new_probs, new_denom, out_t = [], [], None, None
        for s in range(max(n_seg(i), n_seg(i + 1), n_seg(i + 2))):
            if s < n_seg(i + 2):
                new_scores.append(score_seg(i + 2, s))
            if s < n_seg(i + 1):
                p = jnp.exp2(scores[i + 1][s] - m_next)
                part = jnp.sum(p, axis=0, keepdims=True)
                new_denom = part if new_denom is None else new_denom + part
                new_probs.append(p.astype(BF))
            if s < n_seg(i):
                part = _dot(items[i][3][s], probs[i][0][s])
                out_t = part if out_t is None else out_t + part
        emit(i, (out_t / probs.pop(i)[1]).T)
        if i + 1 < n:
            scores.pop(i + 1)
            probs[i + 1] = (new_probs, new_denom)
        if i + 2 < n:
            scores[i + 2] = new_scores


def _attn_gqa_kernel(q_ref, *rest, n_seg, n_blk):
    k_refs, v_refs, o_ref = rest[:n_seg], rest[n_seg:2 * n_seg], rest[2 * n_seg]
    grp = A_HEADS // A_KV_HEADS
    half = _idiv(_lane_iota(), HEAD_DIM)
    values_t = [_transpose_values(r) for r in v_refs]
    items, place = [], []
    for blk in range(n_blk):
        all_blocks = n_blk * 2 == A_HEADS
        head = (blk if all_blocks else pl.program_id(1) * n_blk + blk) * 2
        kv_pos = (head // grp) % 2
        kv_blk = blk * 2 // grp // 2 if all_blocks else 0
        lanes = slice(kv_blk * LANES, (kv_blk + 1) * LANES)
        keys = [r[rows, lanes] for r in k_refs for rows in _key_blocks(r.shape[0])]
        values = [vt[lanes, rows] for vt in values_t for rows in _key_blocks(vt.shape[1])]
        for r0 in range(0, q_ref.shape[0], ATT_SUB):
            q = q_ref[r0:r0 + ATT_SUB, blk * LANES:(blk + 1) * LANES].astype(F32)
            q_sw = pltpu.roll(q, HEAD_DIM, axis=1)
            for e in range(2):
                q_e = jnp.where(kv_pos == e, q, q_sw)
                q_e = jnp.where(half == kv_pos, q_e, 0.0).astype(BF)
                items.append((q_e, keys, [None] * len(keys), values))
                place.append((blk, r0, e, kv_pos))
    outs = {}

    def emit(i, o):
        blk, r0, e, kv_pos = place[i]
        outs[e] = jnp.where(kv_pos == e, o, pltpu.roll(o, HEAD_DIM, axis=1))
        if e == 1:
            o_ref[r0:r0 + ATT_SUB, blk * LANES:(blk + 1) * LANES] = (
                jnp.where(half == 0, outs[0], outs[1]).astype(BF))

    _attend_pipelined(items, emit)


def _blocks_per_step(n_q_tok, n_blocks):
    return n_blocks if n_q_tok < ATT_SUB else ATT_BLOCKS


def _attn_gqa(q_src, kv_srcs):
    bsz, n_q_tok, _ = q_src.shape
    tq = _attn_tile(n_q_tok)
    n_qb = A_HEADS * HEAD_DIM // LANES
    n_kb = A_KV_HEADS * HEAD_DIM // LANES
    per_kb = n_qb // n_kb
    n_seg = len(kv_srcs)
    n_blk = _blocks_per_step(n_q_tok, n_qb)
    assert n_blk == n_qb or per_kb % n_blk == 0
    kv_w = LANES if n_blk < n_qb else n_kb * LANES
    k0, v0 = n_qb * LANES // kv_w, (n_qb + n_kb) * LANES // kv_w
    k_specs = [pl.BlockSpec((None, s.shape[1], kv_w), lambda b, p, i: (b, 0, k0 + p * n_blk // per_kb))
               for s in kv_srcs]
    v_specs = [pl.BlockSpec((None, s.shape[1], kv_w), lambda b, p, i: (b, 0, v0 + p * n_blk // per_kb))
               for s in kv_srcs]
    return pl.pallas_call(
        functools.partial(_attn_gqa_kernel, n_seg=n_seg, n_blk=n_blk),
        out_shape=jax.ShapeDtypeStruct((bsz, n_q_tok, n_qb * LANES), BF),
        grid=(bsz, n_qb // n_blk, n_q_tok // tq),
        in_specs=[pl.BlockSpec((None, tq, n_blk * LANES), lambda b, p, i: (b, i, p))] + k_specs + v_specs,
        out_specs=pl.BlockSpec((None, tq, n_blk * LANES), lambda b, p, i: (b, i, p)),
        compiler_params=_cparams(3),
        name="gqa_attention",
    )(q_src, *kv_srcs, *kv_srcs)


def _attn_diff_kernel(q_ref, lam_ref, subln_ref, *rest, n_seg, lam_init):
    k_refs, v_refs, o_ref = rest[:n_seg], rest[n_seg:2 * n_seg], rest[2 * n_seg]
    lam_v = lam_ref[...]
    lam = (jnp.exp(jnp.sum(lam_v[0:1] * lam_v[1:2], axis=-1, keepdims=True))
           - jnp.exp(jnp.sum(lam_v[2:3] * lam_v[3:4], axis=-1, keepdims=True)) + lam_init)
    half = _idiv(_lane_iota(), HEAD_DIM)
    values_t = [_transpose_values(r) for r in v_refs]
    n_blk = q_ref.shape[1] // LANES
    items, place = [], []
    for blk in range(n_blk):
        lanes = slice(blk * LANES, (blk + 1) * LANES)
        keys = [r[rows, lanes] for r in k_refs for rows in _key_blocks(r.shape[0])]
        values = [vt[lanes, rows] for vt in values_t for rows in _key_blocks(vt.shape[1])]
        for r0 in range(0, q_ref.shape[0], ATT_SUB):
            q = q_ref[r0:r0 + ATT_SUB, lanes]
            for e in range(2):
                items.append((jnp.where(half == e, q, jnp.zeros_like(q)), keys, [None] * len(keys), values))
                place.append((lanes, r0, e))
    outs = {}

    def emit(i, o):
        lanes, r0, e = place[i]
        outs[e] = o
        if e == 1:
            o = outs[0] - lam * outs[1]
            ms = jnp.mean(o * o, axis=-1, keepdims=True)
            o = o * lax.rsqrt(ms + NORM_EPS) * subln_ref[...] * (1.0 - lam_init)
            o_ref[r0:r0 + ATT_SUB, lanes] = o.astype(BF)

    _attend_pipelined(items, emit)


def _attn_diff(q_src, kv_srcs, lam_vecs, subln, lam_init):
    bsz, n_q_tok, _ = q_src.shape
    tq = _attn_tile(n_q_tok)
    nh = B_HEADS
    n_seg = len(kv_srcs)
    n_blk = _blocks_per_step(n_q_tok, nh)
    w = n_blk * LANES
    steps = nh // n_blk
    k_specs = [pl.BlockSpec((None, s.shape[1], w), lambda b, h, i: (b, 0, steps + h)) for s in kv_srcs]
    v_specs = [pl.BlockSpec((None, s.shape[1], w), lambda b, h, i: (b, 0, 2 * steps + h)) for s in kv_srcs]
    return pl.pallas_call(
        functools.partial(_attn_diff_kernel, n_seg=n_seg, lam_init=lam_init),
        out_shape=jax.ShapeDtypeStruct((bsz, n_q_tok, nh * LANES), BF),
        grid=(bsz, steps, n_q_tok // tq),
        in_specs=[pl.BlockSpec((None, tq, w), lambda b, h, i: (b, i, h)),
                  pl.BlockSpec(lam_vecs.shape, lambda b, h, i: (0, 0)),
                  pl.BlockSpec((1, LANES), lambda b, h, i: (0, 0))] + k_specs + v_specs,
        out_specs=pl.BlockSpec((None, tq, w), lambda b, h, i: (b, i, h)),
        compiler_params=_cparams(3),
        name="differential_attention",
    )(q_src, lam_vecs, subln.reshape(1, LANES), *kv_srcs, *kv_srcs)


NB_QROWS = 4
NB_BAND = 12


def _nb_band_start(g, rows):
    return int(np.clip(NB_QROWS * g - NB_QROWS, 0, rows - NB_BAND))


def _nb_config(g, rows):
    n_g = rows // NB_QROWS
    return 0 if g == 0 else (2 if g == n_g - 1 else 1)


def _nb_bias_table(rpb, rows):
    nh = rpb.shape[0]
    wr = min(WIN_R, rows)
    cq = np.arange(GRID_W)
    c0 = np.clip(cq - WIN_C // 2, 0, GRID_W - WIN_C)
    col_in = (cq[None, :] >= c0[:, None]) & (cq[None, :] < c0[:, None] + WIN_C)
    dc_idx = np.clip(cq[None, :] - cq[:, None], -(WIN_C - 1), WIN_C - 1) + WIN_C - 1
    n_dr = 2 * WIN_R - 1
    tiles = rpb[:, :, dc_idx]
    tiles = jnp.where(col_in[None, None], tiles * LOG2E, MASK_VALUE)
    tiles = jnp.concatenate([tiles, jnp.full((nh, 1, GRID_W, GRID_W), MASK_VALUE, F32)], axis=1)
    n_g = rows // NB_QROWS
    sel = np.full((3, NB_QROWS, NB_BAND), n_dr, dtype=np.int32)
    for cfg, g in ((0, 0), (1, 1), (2, n_g - 1)):
        start = _nb_band_start(g, rows)
        for a in range(NB_QROWS):
            r = NB_QROWS * g + a
            r0 = int(np.clip(r - wr // 2, 0, rows - wr))
            for jb in range(NB_BAND):
                kr = start + jb
                if r0 <= kr < r0 + wr:
                    sel[cfg, a, jb] = kr - r + (WIN_R - 1)
    big = tiles[:, sel]
    big = big.transpose(0, 1, 3, 5, 2, 4)
    return big.reshape(nh, 3, NB_BAND * GRID_W, NB_QROWS * GRID_W)


def _attn_nb_kernel(q_ref, kl_ref, vl_ref, kc_ref, vc_ref, bias_ref, o_ref, *, rows):
    half = _idiv(_lane_iota(), HEAD_DIM)
    kc = kc_ref[...]
    vc = _transpose_values(vc_ref)
    vl = _transpose_values(vl_ref)
    tq = NB_QROWS * GRID_W
    tk = NB_BAND * GRID_W
    items = []
    for g in range(rows // NB_QROWS):
        ks = _nb_band_start(g, rows) * GRID_W
        cfg = _nb_config(g, rows)
        q = q_ref[g * tq:(g + 1) * tq, :]
        kb = kl_ref[ks:ks + tk, :]
        vb = vl[:, ks:ks + tk]
        for e in range(2):
            q_e = jnp.where(half == e, q, jnp.zeros_like(q))
            items.append((q_e, [kb, kc], [bias_ref[e, cfg], None], [vb, vc]))
    outs = {}

    def emit(i, o):
        g, e = i // 2, i % 2
        outs[e] = o
        if e == 1:
            o_ref[g * tq:(g + 1) * tq, :] = jnp.where(half == 0, outs[0], outs[1]).astype(BF)

    _attend_pipelined(items, emit)


def _attn_nb(qkv_lat, qkv_ctx, bias):
    bsz, n_tok, _ = qkv_lat.shape
    n_ctx = qkv_ctx.shape[1]
    rows = n_tok // GRID_W
    npair = D_HEADS * HEAD_DIM // LANES
    return pl.pallas_call(
        functools.partial(_attn_nb_kernel, rows=rows),
        out_shape=jax.ShapeDtypeStruct((bsz, n_tok, npair * LANES), BF),
        grid=(npair, bsz),
        in_specs=[
            pl.BlockSpec((None, n_tok, LANES), lambda p, b: (b, 0, p)),
            pl.BlockSpec((None, n_tok, LANES), lambda p, b: (b, 0, npair + p)),
            pl.BlockSpec((None, n_tok, LANES), lambda p, b: (b, 0, 2 * npair + p)),
            pl.BlockSpec((None, n_ctx, LANES), lambda p, b: (b, 0, npair + p)),
            pl.BlockSpec((None, n_ctx, LANES), lambda p, b: (b, 0, 2 * npair + p)),
            pl.BlockSpec((2,) + bias.shape[1:], lambda p, b: (p, 0, 0, 0)),
        ],
        out_specs=pl.BlockSpec((None, n_tok, LANES), lambda p, b: (b, 0, p)),
        compiler_params=_cparams(2),
        name="neighbourhood_attention",
    )(qkv_lat, qkv_lat, qkv_lat, qkv_ctx, qkv_ctx, bias)


def _attn_ctx_kernel(q_ref, k_ref, v_ref, o_ref):
    half = _idiv(_lane_iota(), HEAD_DIM)
    q = q_ref[...]
    zero = jnp.zeros_like(q)
    keys, values = [k_ref[...]], [_transpose_values(v_ref)]
    outs = [_softmax_pv_t(_scores_t(jnp.where(half == e, q, zero), keys, [None]), values) for e in range(2)]
    o_ref[...] = jnp.where(half == 0, outs[0], outs[1]).astype(BF)


def _attn_ctx(qkv_ctx, npair):
    bsz, n_ctx, _ = qkv_ctx.shape
    return pl.pallas_call(
        _attn_ctx_kernel,
        out_shape=jax.ShapeDtypeStruct((bsz, n_ctx, npair * LANES), BF),
        grid=(bsz, npair),
        in_specs=[
            pl.BlockSpec((None, n_ctx, LANES), lambda b, p: (b, 0, p)),
            pl.BlockSpec((None, n_ctx, LANES), lambda b, p: (b, 0, npair + p)),
            pl.BlockSpec((None, n_ctx, LANES), lambda b, p: (b, 0, 2 * npair + p)),
        ],
        out_specs=pl.BlockSpec((None, n_ctx, LANES), lambda b, p: (b, 0, p)),
        compiler_params=_cparams(2),
        name="context_attention",
    )(qkv_ctx, qkv_ctx, qkv_ctx)


def _gla_geometry(rev):
    row = lax.broadcasted_iota(jnp.int32, (C_CHUNK, 1), 0)
    col = lax.broadcasted_iota(jnp.int32, (1, C_CHUNK), 1)
    sub = _idiv(row, C_SUB)
    col_sub = _idiv(col, C_SUB)
    if rev:
        tri, cross, dist = col >= row, col_sub > sub, col - row
    else:
        tri, cross, dist = col <= row, col_sub < sub, row - col
    off_blk = jnp.where(cross, sub, -1)
    dmat = jnp.where(tri, jnp.where(col_sub == sub, dist, -1), -1)
    return tri.astype(BF), sub, off_blk, dmat


def _gla_decay(g, geom):
    return _split_dot_left(geom[0], g) * LOG2E


def _gla_factors(q, k, b, rev, geom):
    sub = geom[1]
    nsub = C_CHUNK // C_SUB
    b_tot = b[0:1] if rev else b[C_CHUNK - 1:C_CHUNK]
    blocks = range(nsub - 1) if rev else range(1, nsub)
    bounds = {}
    ref_rows = jnp.zeros_like(b)
    for a in blocks:
        r = (a + 1) * C_SUB if rev else a * C_SUB - 1
        bounds[a] = b[r:r + 1]
        ref_rows = jnp.where(sub == a, bounds[a], ref_rows)
    q_in = (q * jnp.exp2(b)).astype(BF)
    q_t = (q * jnp.exp2(b - ref_rows)).astype(BF)
    k_ts = {a: (k * jnp.exp2(jnp.minimum(bounds[a] - b, 0.0))).astype(BF) for a in blocks}
    k_end = (k * jnp.exp2(b_tot - b)).astype(BF)
    return q_in, q_t, k_ts, k_end, jnp.exp2(b_tot)


def _gla_cross(factors, v, geom):
    _, q_t, k_ts, k_end, _ = factors
    a_mat = jnp.zeros((C_CHUNK, C_CHUNK), F32)
    for a, k_t in k_ts.items():
        a_mat = jnp.where(geom[2] == a, _dot_nt(q_t, k_t), a_mat)
    return a_mat, _dot_tn(v, k_end)


def _gla_same(q, k, b, a_mat, rev, geom):
    nsub, half = C_CHUNK // C_SUB, C_SUB // 2

    def halves(x):
        x = x.reshape(nsub, C_SUB, x.shape[-1])
        return x[:, :half], x[:, half:]

    first, second = zip(*[halves(x) for x in (q, k, b, a_mat, geom[3])])
    near, far = (second, first) if rev else (first, second)
    (q_n, k_n, b_n, a_n, d_n), (q_f, k_f, b_f, a_f, d_f) = near, far
    r = lax.broadcasted_iota(jnp.int32, (1, half, 1), 1)

    def rot(x, d):
        shift = (half - d) % half if rev else d
        return x if shift == 0 else pltpu.roll(x, shift, axis=1)

    def diag(qq, kk, bb, b_partner):
        return jnp.sum(qq * kk * jnp.exp2(bb - b_partner), axis=-1, keepdims=True)

    a_n = jnp.where(d_n == 0, jnp.sum(q_n * k_n, axis=-1, keepdims=True), a_n)
    a_f = jnp.where(d_f == 0, jnp.sum(q_f * k_f, axis=-1, keepdims=True), a_f)
    for d in range(1, half):
        kn, bn, kf, bf = rot(k_n, d), rot(b_n, d), rot(k_f, d), rot(b_f, d)
        own = (r + d < half) if rev else (r >= d)
        a_n = jnp.where(d_n == d, diag(q_n, kn, b_n, bn), a_n)
        a_f = jnp.where(d_f == d, diag(q_f, jnp.where(own, kf, kn), b_f, jnp.where(own, bf, bn)), a_f)
    for d in range(half, C_SUB):
        a_f = jnp.where(d_f == d, diag(q_f, rot(k_n, d - half), b_f, rot(b_n, d - half)), a_f)
    lo, hi = (a_f, a_n) if rev else (a_n, a_f)
    return jnp.concatenate([lo, hi], axis=1).reshape(C_CHUNK, C_CHUNK)


def _gla_advance(factors, a_mat, v, st, increment):
    q_in, _, _, _, decay = factors
    o = _dot_nt(q_in, st.astype(BF)) + _dot(a_mat.astype(BF), v)
    return o, st * decay + increment


def _gla_kernel(ql_ref, kl_ref, vl_ref, ogl_ref, gfl_ref, gbl_ref,
                qc_ref, kc_ref, vc_ref, ogc_ref, gfc_ref, gbc_ref, ng_ref,
                ol_ref, oc_ref, fl_ref, bl_ref, fc_ref, bc_ref, *, with_ctx_out):
    dk = ql_ref.shape[-1]
    dv = vl_ref.shape[-1]
    n_lat = ql_ref.shape[0] // C_CHUNK
    n_ctx = qc_ref.shape[0] // C_CHUNK
    geom_f = _gla_geometry(False)
    geom_b = _gla_geometry(True)

    def load(q_ref, k_ref, v_ref, g_ref, c):
        sl = pl.ds(pl.multiple_of(c * C_CHUNK, C_CHUNK), C_CHUNK)
        return q_ref[sl, :].astype(F32), k_ref[sl, :].astype(F32), v_ref[sl, :], g_ref[sl, :]

    def chunk_at(c):
        return pl.ds(pl.multiple_of(c * C_CHUNK, C_CHUNK), C_CHUNK)

    def scan_both(refs_f, refs_b, f_ref, b_ref, n_chunks, states):
        group = min(GLA_GROUP, n_chunks)

        def body(step, carry):
            st_f, st_b = carry
            work = []
            for u in range(group):
                j = step * group + u
                work.append((False, geom_f, refs_f, j, f_ref))
                work.append((True, geom_b, refs_b, n_chunks - 1 - j, b_ref))
            data = [load(*refs, c) for _, _, refs, c, _ in work]
            decays = [_gla_decay(d[3], w[1]) for d, w in zip(data, work)]
            factors = [_gla_factors(d[0], d[1], b, w[0], w[1]) for d, b, w in zip(data, decays, work)]
            cross = [_gla_cross(f, d[2], w[1]) for f, d, w in zip(factors, data, work)]
            a_mats = [_gla_same(d[0], d[1], b, x[0], w[0], w[1])
                      for d, b, x, w in zip(data, decays, cross, work)]
            for idx, (rev, _, _, c, out_ref) in enumerate(work):
                st = st_b if rev else st_f
                o, st = _gla_advance(factors[idx], a_mats[idx], data[idx][2], st, cross[idx][1])
                out_ref[chunk_at(c), :] = o
                if rev:
                    st_b = st
                else:
                    st_f = st
            return st_f, st_b
        return lax.fori_loop(0, n_chunks // group, body, states)

    def finish(f_ref, b_ref, og_ref, out_ref):
        def body(i, carry):
            sl = pl.ds(pl.multiple_of(i * GLA_FINISH_ROWS, GLA_FINISH_ROWS), GLA_FINISH_ROWS)
            tot = f_ref[sl, :] + b_ref[sl, :]
            ms = jnp.mean(tot * tot, axis=-1, keepdims=True)
            y = tot * lax.rsqrt(ms + NORM_EPS) * ng_ref[...]
            og = og_ref[sl, :].astype(F32)
            out_ref[sl, :] = (y * (og * _sigmoid(og))).astype(BF)
            return carry
        lax.fori_loop(0, f_ref.shape[0] // GLA_FINISH_ROWS, body, 0)

    zero = jnp.zeros((dv, dk), F32)
    lat_f = (ql_ref, kl_ref, vl_ref, gfl_ref)
    lat_b = (ql_ref, kl_ref, vl_ref, gbl_ref)
    ctx_f = (qc_ref, kc_ref, vc_ref, gfc_ref)
    ctx_b = (qc_ref, kc_ref, vc_ref, gbc_ref)
    states = scan_both(ctx_f, ctx_b, fc_ref, bc_ref, n_ctx, (zero, zero))
    scan_both(lat_f, lat_b, fl_ref, bl_ref, n_lat, states)
    finish(fl_ref, bl_ref, ogl_ref, ol_ref)
    if with_ctx_out:
        finish(fc_ref, bc_ref, ogc_ref, oc_ref)
    else:
        oc_ref[...] = jnp.zeros_like(oc_ref)


def _gla(proj_lat, gates_lat, proj_ctx, gates_ctx, norm_g, with_ctx_out):
    bsz, n_lat, _ = proj_lat.shape
    n_ctx = proj_ctx.shape[1]
    nh = C_HEADS
    dk = 512 // nh
    dv = 1024 // nh
    kb, vb = 512 // dk, 1024 // dv

    def specs(n_tok):
        return [
            pl.BlockSpec((None, n_tok, dk), lambda b, h: (b, 0, h)),
            pl.BlockSpec((None, n_tok, dk), lambda b, h: (b, 0, kb + h)),
            pl.BlockSpec((None, n_tok, dv), lambda b, h: (b, 0, (2 * kb * dk) // dv + h)),
            pl.BlockSpec((None, n_tok, dv), lambda b, h: (b, 0, (2 * kb * dk) // dv + vb + h)),
            pl.BlockSpec((None, n_tok, dk), lambda b, h: (b, 0, h)),
            pl.BlockSpec((None, n_tok, dk), lambda b, h: (b, 0, kb + h)),
        ]

    out_l, out_c = pl.pallas_call(
        functools.partial(_gla_kernel, with_ctx_out=with_ctx_out),
        out_shape=(jax.ShapeDtypeStruct((bsz, n_lat, nh * dv), BF),
                   jax.ShapeDtypeStruct((bsz, n_ctx, nh * dv), BF)),
        grid=(bsz, nh),
        in_specs=specs(n_lat) + specs(n_ctx) + [pl.BlockSpec((1, dv), lambda b, h: (0, 0))],
        out_specs=(pl.BlockSpec((None, n_lat, dv), lambda b, h: (b, 0, h)),
                   pl.BlockSpec((None, n_ctx, dv), lambda b, h: (b, 0, h))),
        scratch_shapes=[pltpu.VMEM((n_lat, dv), F32), pltpu.VMEM((n_lat, dv), F32),
                        pltpu.VMEM((n_ctx, dv), F32), pltpu.VMEM((n_ctx, dv), F32)],
        compiler_params=_cparams(2),
        name="gated_linear_attention",
    )(proj_lat, proj_lat, proj_lat, proj_lat, gates_lat, gates_lat,
      proj_ctx, proj_ctx, proj_ctx, proj_ctx, gates_ctx, gates_ctx, norm_g.reshape(1, dv))
    return out_l, out_c


def kernel(x, c, ctx, c_ctx, mod_w, mod_b, ln1_g, ln1_b, ffn_w_up, ffn_conv_w, ffn_conv_b, ffn_w_down, ln2_g, ln2_b, a_w_qkv, a_q_norm, a_k_norm, a_w_o, b_w_qkv, b_lambda_q1, b_lambda_k1, b_lambda_q2, b_lambda_k2, b_subln, b_w_o, c_w_in, c_w_gate_fwd, c_b_gate_fwd, c_w_gate_bwd, c_b_gate_bwd, c_norm, c_w_o, d_w_qkv, d_rpb, d_w_o):
    bsz, n_lat, dm = x.shape
    n_ctx = ctx.shape[1]
    scale = HEAD_DIM ** -0.5 * LOG2E

    pad_rows = (-(bsz + 1)) % 8
    cond = jnp.concatenate([c, c_ctx[None, :], jnp.zeros((pad_rows, dm), F32)], axis=0)
    mod_all = _modulation(cond, mod_w, mod_b)

    x_lat, x_ctx = x, ctx
    for i in range(DEPTH):
        kind, j = i % N_MIXERS, i // N_MIXERS
        last = i == DEPTH - 1
        m_lat = mod_all[i, :bsz].reshape(bsz, N_MOD, dm)
        m_ctx = jnp.broadcast_to(mod_all[i, bsz].reshape(1, N_MOD, dm), (bsz, N_MOD, dm))

        if kind == 0:
            w = a_w_qkv[j].astype(BF)
            gains = (a_q_norm[j], a_k_norm[j])
            nq, nk = A_HEADS * HEAD_DIM // LANES, A_KV_HEADS * HEAD_DIM // LANES
            p_lat = _proj_qkv(x_lat, m_lat, w, n_q=nq, n_k=nk, n_v=nk, q_scale=scale, norm_gains=gains, rope=True)
            p_ctx = _proj_qkv(x_ctx, m_ctx, w, n_q=nq, n_k=nk, n_v=nk, q_scale=scale, norm_gains=gains)
            a_lat = _attn_gqa(p_lat, [p_ctx, p_lat])
            a_ctx = None if last else _attn_gqa(p_ctx, [p_ctx])
            w_o = a_w_o[j]
        elif kind == 1:
            w = b_w_qkv[j].astype(BF)
            nb = 2 * B_HEADS * HEAD_DIM // LANES
            p_lat = _proj_qkv(x_lat, m_lat, w, n_q=nb, n_k=nb, n_v=nb, q_scale=scale, rope=True)
            p_ctx = _proj_qkv(x_ctx, m_ctx, w, n_q=nb, n_k=nb, n_v=nb, q_scale=scale)
            lam_vecs = jnp.stack([b_lambda_q1[j], b_lambda_k1[j], b_lambda_q2[j], b_lambda_k2[j]])
            lam_init = 0.8 - 0.6 * math.exp(-0.3 * i)
            a_lat = _attn_diff(p_lat, [p_ctx, p_lat], lam_vecs, b_subln[j], lam_init)
            a_ctx = None if last else _attn_diff(p_ctx, [p_ctx], lam_vecs, b_subln[j], lam_init)
            w_o = b_w_o[j]
        elif kind == 2:
            w_in = c_w_in[j]
            n_main = w_in.shape[1] - 2 * C_GATE_RANK
            w_main = w_in[:, :n_main].astype(BF)
            w_z = jnp.pad(w_in[:, n_main:], ((0, 0), (0, LANES - 2 * C_GATE_RANK))).astype(BF)
            kd = c_w_gate_fwd.shape[-1]
            w_gate = jnp.zeros((LANES, 2 * kd), F32)
            w_gate = w_gate.at[:C_GATE_RANK, :kd].set(c_w_gate_fwd[j])
            w_gate = w_gate.at[C_GATE_RANK:2 * C_GATE_RANK, kd:].set(c_w_gate_bwd[j]).astype(BF)
            b_gate = jnp.concatenate([c_b_gate_fwd[j], c_b_gate_bwd[j]]).reshape(1, 2 * kd)
            q_scale = (kd // C_HEADS) ** -0.5
            nq = kd // LANES
            p_lat, g_lat = _proj_gla(x_lat, m_lat, w_main, w_z, w_gate, b_gate, q_scale=q_scale, n_q=nq)
            p_ctx, g_ctx = _proj_gla(x_ctx, m_ctx, w_main, w_z, w_gate, b_gate, q_scale=q_scale, n_q=nq)
            a_lat, a_ctx = _gla(p_lat, g_lat, p_ctx, g_ctx, c_norm[j], not last)
            w_o = c_w_o[j]
        else:
            w = d_w_qkv[j].astype(BF)
            nb = D_HEADS * HEAD_DIM // LANES
            p_lat = _proj_qkv(x_lat, m_lat, w, n_q=nb, n_k=nb, n_v=nb, q_scale=scale)
            p_ctx = _proj_qkv(x_ctx, m_ctx, w, n_q=nb, n_k=nb, n_v=nb, q_scale=scale)
            bias = _nb_bias_table(d_rpb[j], n_lat // GRID_W)
            a_lat = _attn_nb(p_lat, p_ctx, bias)
            a_ctx = None if last else _attn_ctx(p_ctx, nb)
            w_o = d_w_o[j]

        w_o = w_o.astype(BF)
        w_up = ffn_w_up[i].astype(BF)
        w_dn = ffn_w_down[i].astype(BF)
        ffn = (w_o, ln1_g[i], ln1_b[i], w_up, ffn_conv_w[i], ffn_conv_b[i], w_dn, ln2_g[i], ln2_b[i])
        x_lat = _mix_ffn(a_lat, x_lat, m_lat, *ffn)
        if not last:
            x_ctx = _mix_ffn(a_ctx, x_ctx, m_ctx, *ffn)
    return x_lat
```

```python
import functools
import math

import jax
import jax.numpy as jnp
import numpy as np
from jax import lax
from jax.experimental import pallas as pl
from jax.experimental.pallas import tpu as pltpu

F32 = jnp.float32
BF = jnp.bfloat16

DEPTH = 4
N_MIXERS = 4
N_MOD = 6
GRID_W = 64
ROPE_THETA = 10000.0
NORM_EPS = 1e-6
LN_EPS = 1e-5
HEAD_DIM = 64
A_HEADS, A_KV_HEADS = 16, 4
B_HEADS = 8
C_HEADS = 4
C_GATE_RANK = 16
C_GATE_NORMALIZER = 16.0
C_CHUNK = 64
C_SUB = 16
GLA_FINISH_ROWS = 256
GLA_GROUP = 8
D_HEADS = 16
WIN_R, WIN_C = 8, 16
FF_DIM = 2816
DEEPNORM_ALPHA = (2 * DEPTH) ** 0.25
MASK_VALUE = -1e30
LOG2E = math.log2(math.e)

LANES = 128
MXU_WIDTH = 256
VMEM_LIMIT_BYTES = 56 * 1024 * 1024


def _cparams(n_axes):
    return pltpu.CompilerParams(
        dimension_semantics=("arbitrary",) * n_axes,
        vmem_limit_bytes=VMEM_LIMIT_BYTES)


def _dot(a, b):
    return jnp.dot(a, b, preferred_element_type=F32)


def _dot_nt(a, b):
    return lax.dot_general(a, b, (((1,), (1,)), ((), ())), preferred_element_type=F32)


def _dot_tn(a, b):
    return lax.dot_general(a, b, (((0,), (0,)), ((), ())), preferred_element_type=F32)


def _split_dot(x, m_bf16):
    hi = x.astype(BF)
    lo = (x - hi.astype(F32)).astype(BF)
    return _dot(hi, m_bf16) + _dot(lo, m_bf16)


def _split_dot_left(m_bf16, x):
    hi = x.astype(BF)
    lo = (x - hi.astype(F32)).astype(BF)
    return _dot(m_bf16, hi) + _dot(m_bf16, lo)


def _sigmoid(x):
    return 1.0 / (1.0 + jnp.exp(-x))


def _modulate(x_ref, mod_ref, shift_row, scale_row):
    x = x_ref[...]
    return x * (1.0 + mod_ref[scale_row:scale_row + 1, :]) + mod_ref[shift_row:shift_row + 1, :]


def _layer_norm(z, g, b):
    mu = jnp.mean(z, axis=-1, keepdims=True)
    zc = z - mu
    var = jnp.mean(zc * zc, axis=-1, keepdims=True)
    return zc * lax.rsqrt(var + LN_EPS) * g + b


def _rope(y, cos_t, sin_t, lo_half):
    swap = jnp.where(lo_half, pltpu.roll(y, LANES - 32, axis=1), pltpu.roll(y, 32, axis=1))
    return y * cos_t + swap * sin_t


def _idiv(x, n):
    return lax.shift_right_logical(x, int(n).bit_length() - 1)


def _imod(x, n):
    return x & (n - 1)


def _lane_iota(shape=(1, LANES)):
    return lax.broadcasted_iota(jnp.int32, shape, len(shape) - 1)


MOD_COL_BLOCKS = 4


def _mod_kernel(cond_ref, w_ref, b_ref, o_ref):
    cnd = cond_ref[...]
    act = cnd * _sigmoid(cnd)
    o_ref[...] = _dot(act.astype(BF), w_ref[...].astype(BF)) + b_ref[...]


def _modulation(cond, mod_w, mod_b):
    depth, dm, n = mod_w.shape
    rows = cond.shape[0]
    tn = n // MOD_COL_BLOCKS
    return pl.pallas_call(
        _mod_kernel,
        out_shape=jax.ShapeDtypeStruct((depth, rows, n), F32),
        grid=(depth, n // tn),
        in_specs=[
            pl.BlockSpec((rows, dm), lambda i, j: (0, 0)),
            pl.BlockSpec((None, dm, tn), lambda i, j: (i, 0, j)),
            pl.BlockSpec((None, 1, tn), lambda i, j: (i, 0, j)),
        ],
        out_specs=pl.BlockSpec((None, rows, tn), lambda i, j: (i, 0, j)),
        compiler_params=_cparams(2),
        name="adaln_modulation",
    )(cond, mod_w, mod_b.reshape(depth, 1, n))


PROJ_TILE = 512


def _row_tile(n_tok):
    return PROJ_TILE if n_tok % PROJ_TILE == 0 else n_tok


def _proj_call(kernel, x, mod, w, extras, extra_specs, out_shapes, out_specs, name):
    bsz, n_tok, dm = x.shape
    tm = _row_tile(n_tok)
    in_specs = [
        pl.BlockSpec((None, tm, dm), lambda b, i: (b, i, 0)),
        pl.BlockSpec((None, N_MOD, dm), lambda b, i: (b, 0, 0)),
        pl.BlockSpec(w.shape, lambda b, i: (0, 0)),
    ] + list(extra_specs(tm))
    return pl.pallas_call(
        kernel,
        out_shape=out_shapes,
        grid=(bsz, n_tok // tm),
        in_specs=in_specs,
        out_specs=out_specs(tm),
        compiler_params=_cparams(2),
        name=name,
    )(x, mod, w, *extras)


def _full_spec(arr):
    nd = arr.ndim
    return pl.BlockSpec(arr.shape, lambda b, i: (0,) * nd)


def _seg_matrix():
    idx = np.arange(LANES) // HEAD_DIM
    return jnp.asarray((idx[:, None] == idx[None, :]).astype(np.float32) / HEAD_DIM, dtype=BF)


def _rope_tables(n_tok):
    t = jnp.arange(n_tok)
    row = (t // GRID_W).astype(F32)
    col = (t % GRID_W).astype(F32)
    n_freq = HEAD_DIM // 4
    inv = 1.0 / (ROPE_THETA ** (jnp.arange(n_freq, dtype=F32) / n_freq))
    ang = jnp.concatenate([row[:, None] * inv, col[:, None] * inv], axis=-1)
    cos, sin = jnp.cos(ang), jnp.sin(ang)
    cos_t = jnp.concatenate([cos, cos, cos, cos], axis=-1)
    sin_t = jnp.concatenate([-sin, sin, -sin, sin], axis=-1)
    return cos_t, sin_t


def _proj_qkv_kernel(x_ref, mod_ref, w_ref, *rest, n_q, n_k, n_v, norm, rope, q_scale):
    rest = list(rest)
    seg_ref = gq_ref = gk_ref = cos_ref = sin_ref = None
    if norm:
        seg_ref, gq_ref, gk_ref = rest[:3]
        rest = rest[3:]
    if rope:
        cos_ref, sin_ref = rest[:2]
        rest = rest[2:]
    (o_ref,) = rest
    hb = _modulate(x_ref, mod_ref, 0, 1).astype(BF)
    lo_half = _imod(_lane_iota(), HEAD_DIM) < (HEAD_DIM // 2)
    n_blocks = n_q + n_k + n_v
    per_dot = MXU_WIDTH // LANES
    ys = []
    for c0 in range(0, n_blocks, per_dot):
        nb = min(per_dot, n_blocks - c0)
        y2 = _dot(hb, w_ref[:, c0 * LANES:(c0 + nb) * LANES])
        ys += [y2[:, j * LANES:(j + 1) * LANES] for j in range(nb)]
    mean_sq = [_split_dot(y * y, seg_ref[...]) if norm and c < n_q + n_k else None for c, y in enumerate(ys)]
    for c, y in enumerate(ys):
        if c < n_q + n_k:
            if norm:
                gain = gq_ref[...] if c < n_q else gk_ref[...]
                y = y * lax.rsqrt(mean_sq[c] + NORM_EPS) * gain
            if rope:
                y = _rope(y, cos_ref[...], sin_ref[...], lo_half)
            if c < n_q and q_scale != 1.0:
                y = y * q_scale
        o_ref[:, c * LANES:(c + 1) * LANES] = y.astype(BF)


def _proj_qkv(x, mod, w, *, n_q, n_k, n_v, q_scale, norm_gains=None, rope=False):
    bsz, n_tok, _ = x.shape
    n_out = w.shape[1]
    extras, spec_fns = [], []
    if norm_gains is not None:
        gq, gk = norm_gains
        extras += [_seg_matrix(), jnp.tile(gq, 2).reshape(1, LANES), jnp.tile(gk, 2).reshape(1, LANES)]
        spec_fns += [lambda tm, a=a: _full_spec(a) for a in extras]
    if rope:
        cos_t, sin_t = _rope_tables(n_tok)
        extras += [cos_t, sin_t]
        spec_fns += [lambda tm: pl.BlockSpec((tm, LANES), lambda b, i: (i, 0))] * 2
    kern = functools.partial(_proj_qkv_kernel, n_q=n_q, n_k=n_k, n_v=n_v,
                             norm=norm_gains is not None, rope=rope, q_scale=q_scale)
    return _proj_call(
        kern, x, mod, w, extras, lambda tm: [f(tm) for f in spec_fns],
        jax.ShapeDtypeStruct((bsz, n_tok, n_out), BF),
        lambda tm: pl.BlockSpec((None, tm, n_out), lambda b, i: (b, i, 0)),
        "mixer_qkv_projection")


def _log_sigmoid(x):
    return jnp.minimum(x, 0.0) - jnp.log(1.0 + jnp.exp(-jnp.abs(x)))


def _proj_gla_kernel(x_ref, mod_ref, w_ref, wz_ref, wg_ref, bg_ref, o_ref, g_ref, *, q_scale, n_q):
    hb = _modulate(x_ref, mod_ref, 0, 1).astype(BF)
    n_blocks = o_ref.shape[-1] // LANES
    per_dot = MXU_WIDTH // LANES
    for c0 in range(0, n_blocks, per_dot):
        y2 = _dot(hb, w_ref[:, c0 * LANES:(c0 + per_dot) * LANES])
        if c0 < n_q:
            y2 = y2 * q_scale
        o_ref[:, c0 * LANES:(c0 + per_dot) * LANES] = y2.astype(BF)
    z = _dot(hb, wz_ref[...]).astype(BF)
    n_gate = g_ref.shape[-1]
    for c0 in range(0, n_gate, MXU_WIDTH):
        pre = _dot(z, wg_ref[:, c0:c0 + MXU_WIDTH]) + bg_ref[:, c0:c0 + MXU_WIDTH]
        g_ref[:, c0:c0 + MXU_WIDTH] = _log_sigmoid(pre) * (1.0 / C_GATE_NORMALIZER)


def _proj_gla(x, mod, w_main, w_z, w_gate, b_gate, *, q_scale, n_q):
    bsz, n_tok, _ = x.shape
    n_out, n_gate = w_main.shape[1], w_gate.shape[1]
    extras = [w_z, w_gate, b_gate]
    kern = functools.partial(_proj_gla_kernel, q_scale=q_scale, n_q=n_q)
    return _proj_call(
        kern, x, mod, w_main, extras, lambda tm: [_full_spec(a) for a in extras],
        (jax.ShapeDtypeStruct((bsz, n_tok, n_out), BF), jax.ShapeDtypeStruct((bsz, n_tok, n_gate), F32)),
        lambda tm: (pl.BlockSpec((None, tm, n_out), lambda b, i: (b, i, 0)),
                    pl.BlockSpec((None, tm, n_gate), lambda b, i: (b, i, 0))),
        "gla_projection")


FFN_CHUNK = 256
FFN_TILE = 512
FFN_ROW_PARTS = 2
HALO = 16


def _mix_ffn_kernel(a_ref, ap_ref, an_ref, x_ref, xp_ref, xn_ref, mod_ref, wo_ref, g1_ref, b1_ref,
                    wu_ref, cw_ref, cb_ref, wd_ref, g2_ref, b2_ref, o_ref, act_ref):
    tm = x_ref.shape[0]
    i = pl.program_id(1)
    has_prev = (i > 0).astype(F32)
    has_next = (i < pl.num_programs(1) - 1).astype(F32)
    n_ext = tm + 2 * HALO

    a_ext = jnp.concatenate([a_ref[...], an_ref[...], ap_ref[...]], axis=0)
    x_ext = jnp.concatenate([x_ref[...], xn_ref[...], xp_ref[...]], axis=0)
    he = n_ext // FFN_ROW_PARTS
    ext_starts = list(range(0, n_ext, he))
    y_next = _dot(a_ext[0:he], wo_ref[...])
    x_mid = []
    for j, r in enumerate(ext_starts):
        y = y_next
        if j + 1 < len(ext_starts):
            y_next = _dot(a_ext[r + he:r + 2 * he], wo_ref[...])
        z1 = DEEPNORM_ALPHA * x_ext[r:r + he] + mod_ref[2:3, :] * y
        x_mid.append(_layer_norm(z1, g1_ref[...], b1_ref[...]))
    x_mid = jnp.concatenate(x_mid, axis=0)
    h = x_mid * (1.0 + mod_ref[4:5, :]) + mod_ref[3:4, :]
    hb = jnp.concatenate([h[:tm], h[tm:tm + HALO] * has_next, h[tm + HALO:] * has_prev], axis=0).astype(BF)

    def up(c):
        cg = c * FFN_CHUNK
        return _dot(hb, wu_ref[:, cg:cg + FFN_CHUNK]), _dot(hb, wu_ref[:, FF_DIM + cg:FF_DIM + cg + FFN_CHUNK])

    def conv(u_ext, c0):
        u_dn = pltpu.roll(u_ext, 1, axis=0)[:tm]
        u_up = pltpu.roll(u_ext, n_ext - 1, axis=0)[:tm]
        w0 = cw_ref[0:1, c0:c0 + FFN_CHUNK]
        w1 = cw_ref[1:2, c0:c0 + FFN_CHUNK]
        w2 = cw_ref[2:3, c0:c0 + FFN_CHUNK]
        return cb_ref[:, c0:c0 + FFN_CHUNK] + u_dn * w0 + u_ext[:tm] * w1 + u_up * w2

    for c in range(FF_DIM // FFN_CHUNK):
        cg = c * FFN_CHUNK
        u_gate, u_val = up(c)
        gate = conv(u_gate, cg)
        val = conv(u_val, FF_DIM + cg)
        act_ref[:, cg:cg + FFN_CHUNK] = (gate * _sigmoid(gate) * val).astype(BF)
    hm = tm // FFN_ROW_PARTS
    starts = list(range(0, tm, hm))
    y_next = _dot(act_ref[0:hm, :], wd_ref[...])
    for j, r in enumerate(starts):
        y = y_next
        if j + 1 < len(starts):
            y_next = _dot(act_ref[r + hm:r + 2 * hm, :], wd_ref[...])
        z2 = DEEPNORM_ALPHA * x_mid[r:r + hm] + mod_ref[5:6, :] * y
        o_ref[r:r + hm, :] = _layer_norm(z2, g2_ref[...], b2_ref[...])


def _mix_ffn(a, x, mod, w_o, ln1_g, ln1_b, w_up, conv_w, conv_b, w_down, ln2_g, ln2_b):
    bsz, n_tok, dm = x.shape
    k_in = a.shape[-1]
    tm = FFN_TILE if n_tok % FFN_TILE == 0 else n_tok
    n_halo = n_tok // HALO
    per = tm // HALO
    tile = lambda b, i: (b, i, 0)
    prev = lambda b, i: (b, jnp.maximum(i * per - 1, 0), 0)
    nxt = lambda b, i: (b, jnp.minimum((i + 1) * per, n_halo - 1), 0)
    const2 = lambda b, i: (0, 0)
    resident = functools.partial(pl.BlockSpec, index_map=const2, pipeline_mode=pl.Buffered(1))
    row = lambda v: v.reshape(1, -1)
    return pl.pallas_call(
        _mix_ffn_kernel,
        out_shape=jax.ShapeDtypeStruct((bsz, n_tok, dm), F32),
        grid=(bsz, n_tok // tm),
        in_specs=[
            pl.BlockSpec((None, tm, k_in), tile),
            pl.BlockSpec((None, HALO, k_in), prev),
            pl.BlockSpec((None, HALO, k_in), nxt),
            pl.BlockSpec((None, tm, dm), tile),
            pl.BlockSpec((None, HALO, dm), prev),
            pl.BlockSpec((None, HALO, dm), nxt),
            pl.BlockSpec((None, N_MOD, dm), lambda b, i: (b, 0, 0)),
            resident(w_o.shape),
            pl.BlockSpec((1, dm), const2),
            pl.BlockSpec((1, dm), const2),
            resident(w_up.shape),
            pl.BlockSpec(conv_w.shape, const2),
            pl.BlockSpec((1, 2 * FF_DIM), const2),
            resident(w_down.shape),
            pl.BlockSpec((1, dm), const2),
            pl.BlockSpec((1, dm), const2),
        ],
        out_specs=pl.BlockSpec((None, tm, dm), tile),
        scratch_shapes=[pltpu.VMEM((tm, FF_DIM), BF)],
        compiler_params=_cparams(2),
        name="outproj_convffn_layernorm",
    )(a, a, a, x, x, x, mod, w_o, row(ln1_g), row(ln1_b), w_up, conv_w, row(conv_b), w_down,
      row(ln2_g), row(ln2_b))


ATT_SUB = 512
ATT_TILE = 2048
ATT_BLOCKS = 1


def _attn_tile(n_tok):
    return ATT_TILE if n_tok % ATT_TILE == 0 else n_tok


ATT_KEYS = 1024


def _key_blocks(n_keys):
    return [slice(r, min(r + ATT_KEYS, n_keys)) for r in range(0, n_keys, ATT_KEYS)]


def _transpose_values(v_ref):
    return v_ref[...].astype(F32).T.astype(BF)


def _scores_t(q, keys, biases_t):
    return [_dot_nt(k, q) if bias is None else _dot_nt(k, q) + bias for k, bias in zip(keys, biases_t)]


def _probs_t(scores):
    m = functools.reduce(jnp.maximum, [jnp.max(s, axis=0, keepdims=True) for s in scores])
    probs = [jnp.exp2(s - m) for s in scores]
    denom = functools.reduce(lambda a, b: a + b, [jnp.sum(p, axis=0, keepdims=True) for p in probs])
    return [p.astype(BF) for p in probs], denom


def _pv_t(probs, denom, values_t):
    out_t = functools.reduce(lambda a, b: a + b, [_dot(vt, p) for p, vt in zip(probs, values_t)])
    return (out_t / denom).T


def _softmax_pv_t(scores, values_t):
    return _pv_t(*_probs_t(scores), values_t)


def _attend_pipelined(items, emit):
    n = len(items)

    def n_seg(j):
        return len(items[j][1]) if j < n else 0

    def score_seg(j, s):
        q, keys, biases_t, _ = items[j]
        sc = _dot_nt(keys[s], q)
        return sc if biases_t[s] is None else sc + biases_t[s]

    def col_max(j):
        return functools.reduce(jnp.maximum, [jnp.max(s, axis=0, keepdims=True) for s in scores[j]])

    scores = {j: [score_seg(j, s) for s in range(n_seg(j))] for j in range(min(2, n))}
    probs = {0: _probs_t(scores.pop(0))}
    for i in range(n):
        m_next = col_max(i + 1) if i + 1 < n else None
        new_scores, new_probs, new_denom, out_t = [], [], None, None
        for s in range(max(n_seg(i), n_seg(i + 1), n_seg(i + 2))):
            if s < n_seg(i + 2):
                new_scores.append(score_seg(i + 2, s))
            if s < n_seg(i + 1):
                p = jnp.exp2(scores[i + 1][s] - m_next)
                part = jnp.sum(p, axis=0, keepdims=True)
                new_denom = part if new_denom is None else new_denom + part
                new_probs.append(p.astype(BF))
            if s < n_seg(i):
                part = _dot(items[i][3][s], probs[i][0][s])
                out_t = part if out_t is None else out_t + part
        emit(i, (out_t / probs.pop(i)[1]).T)
        if i + 1 < n:
            scores.pop(i + 1)
            probs[i + 1] = (new_probs, new_denom)
        if i + 2 < n:
            scores[i + 2] = new_scores


def _attn_gqa_kernel(q_ref, *rest, n_seg, n_blk):
    k_refs, v_refs, o_ref = rest[:n_seg], rest[n_seg:2 * n_seg], rest[2 * n_seg]
    grp = A_HEADS // A_KV_HEADS
    half = _idiv(_lane_iota(), HEAD_DIM)
    values_t = [_transpose_values(r) for r in v_refs]
    items, place = [], []
    for blk in range(n_blk):
        all_blocks = n_blk * 2 == A_HEADS
        head = (blk if all_blocks else pl.program_id(1) * n_blk + blk) * 2
        kv_pos = (head // grp) % 2
        kv_blk = blk * 2 // grp // 2 if all_blocks else 0
        lanes = slice(kv_blk * LANES, (kv_blk + 1) * LANES)
        keys = [r[rows, lanes] for r in k_refs for rows in _key_blocks(r.shape[0])]
        values = [vt[lanes, rows] for vt in values_t for rows in _key_blocks(vt.shape[1])]
        for r0 in range(0, q_ref.shape[0], ATT_SUB):
            q = q_ref[r0:r0 + ATT_SUB, blk * LANES:(blk + 1) * LANES].astype(F32)
            q_sw = pltpu.roll(q, HEAD_DIM, axis=1)
            for e in range(2):
                q_e = jnp.where(kv_pos == e, q, q_sw)
                q_e = jnp.where(half == kv_pos, q_e, 0.0).astype(BF)
                items.append((q_e, keys, [None] * len(keys), values))
                place.append((blk, r0, e, kv_pos))
    outs = {}

    def emit(i, o):
        blk, r0, e, kv_pos = place[i]
        outs[e] = jnp.where(kv_pos == e, o, pltpu.roll(o, HEAD_DIM, axis=1))
        if e == 1:
            o_ref[r0:r0 + ATT_SUB, blk * LANES:(blk + 1) * LANES] = (
                jnp.where(half == 0, outs[0], outs[1]).astype(BF))

    _attend_pipelined(items, emit)


def _blocks_per_step(n_q_tok, n_blocks):
    return n_blocks if n_q_tok < ATT_SUB else ATT_BLOCKS


def _attn_gqa(q_src, kv_srcs):
    bsz, n_q_tok, _ = q_src.shape
    tq = _attn_tile(n_q_tok)
    n_qb = A_HEADS * HEAD_DIM // LANES
    n_kb = A_KV_HEADS * HEAD_DIM // LANES
    per_kb = n_qb // n_kb
    n_seg = len(kv_srcs)
    n_blk = _blocks_per_step(n_q_tok, n_qb)
    assert n_blk == n_qb or per_kb % n_blk == 0
    kv_w = LANES if n_blk < n_qb else n_kb * LANES
    k0, v0 = n_qb * LANES // kv_w, (n_qb + n_kb) * LANES // kv_w
    k_specs = [pl.BlockSpec((None, s.shape[1], kv_w), lambda b, p, i: (b, 0, k0 + p * n_blk // per_kb))
               for s in kv_srcs]
    v_specs = [pl.BlockSpec((None, s.shape[1], kv_w), lambda b, p, i: (b, 0, v0 + p * n_blk // per_kb))
               for s in kv_srcs]
    return pl.pallas_call(
        functools.partial(_attn_gqa_kernel, n_seg=n_seg, n_blk=n_blk),
        out_shape=jax.ShapeDtypeStruct((bsz, n_q_tok, n_qb * LANES), BF),
        grid=(bsz, n_qb // n_blk, n_q_tok // tq),
        in_specs=[pl.BlockSpec((None, tq, n_blk * LANES), lambda b, p, i: (b, i, p))] + k_specs + v_specs,
        out_specs=pl.BlockSpec((None, tq, n_blk * LANES), lambda b, p, i: (b, i, p)),
        compiler_params=_cparams(3),
        name="gqa_attention",
    )(q_src, *kv_srcs, *kv_srcs)


def _attn_diff_kernel(q_ref, lam_ref, subln_ref, *rest, n_seg, lam_init):
    k_refs, v_refs, o_ref = rest[:n_seg], rest[n_seg:2 * n_seg], rest[2 * n_seg]
    lam_v = lam_ref[...]
    lam = (jnp.exp(jnp.sum(lam_v[0:1] * lam_v[1:2], axis=-1, keepdims=True))
           - jnp.exp(jnp.sum(lam_v[2:3] * lam_v[3:4], axis=-1, keepdims=True)) + lam_init)
    half = _idiv(_lane_iota(), HEAD_DIM)
    values_t = [_transpose_values(r) for r in v_refs]
    n_blk = q_ref.shape[1] // LANES
    items, place = [], []
    for blk in range(n_blk):
        lanes = slice(blk * LANES, (blk + 1) * LANES)
        keys = [r[rows, lanes] for r in k_refs for rows in _key_blocks(r.shape[0])]
        values = [vt[lanes, rows] for vt in values_t for rows in _key_blocks(vt.shape[1])]
        for r0 in range(0, q_ref.shape[0], ATT_SUB):
            q = q_ref[r0:r0 + ATT_SUB, lanes]
            for e in range(2):
                items.append((jnp.where(half == e, q, jnp.zeros_like(q)), keys, [None] * len(keys), values))
                place.append((lanes, r0, e))
    outs = {}

    def emit(i, o):
        lanes, r0, e = place[i]
        outs[e] = o
        if e == 1:
            o = outs[0] - lam * outs[1]
            ms = jnp.mean(o * o, axis=-1, keepdims=True)
            o = o * lax.rsqrt(ms + NORM_EPS) * subln_ref[...] * (1.0 - lam_init)
            o_ref[r0:r0 + ATT_SUB, lanes] = o.astype(BF)

    _attend_pipelined(items, emit)


def _attn_diff(q_src, kv_srcs, lam_vecs, subln, lam_init):
    bsz, n_q_tok, _ = q_src.shape
    tq = _attn_tile(n_q_tok)
    nh = B_HEADS
    n_seg = len(kv_srcs)
    n_blk = _blocks_per_step(n_q_tok, nh)
    w = n_blk * LANES
    steps = nh // n_blk
    k_specs = [pl.BlockSpec((None, s.shape[1], w), lambda b, h, i: (b, 0, steps + h)) for s in kv_srcs]
    v_specs = [pl.BlockSpec((None, s.shape[1], w), lambda b, h, i: (b, 0, 2 * steps + h)) for s in kv_srcs]
    return pl.pallas_call(
        functools.partial(_attn_diff_kernel, n_seg=n_seg, lam_init=lam_init),
        out_shape=jax.ShapeDtypeStruct((bsz, n_q_tok, nh * LANES), BF),
        grid=(bsz, steps, n_q_tok // tq),
        in_specs=[pl.BlockSpec((None, tq, w), lambda b, h, i: (b, i, h)),
                  pl.BlockSpec(lam_vecs.shape, lambda b, h, i: (0, 0)),
                  pl.BlockSpec((1, LANES), lambda b, h, i: (0, 0))] + k_specs + v_specs,
        out_specs=pl.BlockSpec((None, tq, w), lambda b, h, i: (b, i, h)),
        compiler_params=_cparams(3),
        name="differential_attention",
    )(q_src, lam_vecs, subln.reshape(1, LANES), *kv_srcs, *kv_srcs)


NB_QROWS = 4
NB_BAND = 12


def _nb_band_start(g, rows):
    return int(np.clip(NB_QROWS * g - NB_QROWS, 0, rows - NB_BAND))


def _nb_config(g, rows):
    n_g = rows // NB_QROWS
    return 0 if g == 0 else (2 if g == n_g - 1 else 1)


def _nb_bias_table(rpb, rows):
    nh = rpb.shape[0]
    wr = min(WIN_R, rows)
    cq = np.arange(GRID_W)
    c0 = np.clip(cq - WIN_C // 2, 0, GRID_W - WIN_C)
    col_in = (cq[None, :] >= c0[:, None]) & (cq[None, :] < c0[:, None] + WIN_C)
    dc_idx = np.clip(cq[None, :] - cq[:, None], -(WIN_C - 1), WIN_C - 1) + WIN_C - 1
    n_dr = 2 * WIN_R - 1
    tiles = rpb[:, :, dc_idx]
    tiles = jnp.where(col_in[None, None], tiles * LOG2E, MASK_VALUE)
    tiles = jnp.concatenate([tiles, jnp.full((nh, 1, GRID_W, GRID_W), MASK_VALUE, F32)], axis=1)
    n_g = rows // NB_QROWS
    sel = np.full((3, NB_QROWS, NB_BAND), n_dr, dtype=np.int32)
    for cfg, g in ((0, 0), (1, 1), (2, n_g - 1)):
        start = _nb_band_start(g, rows)
        for a in range(NB_QROWS):
            r = NB_QROWS * g + a
            r0 = int(np.clip(r - wr // 2, 0, rows - wr))
            for jb in range(NB_BAND):
                kr = start + jb
                if r0 <= kr < r0 + wr:
                    sel[cfg, a, jb] = kr - r + (WIN_R - 1)
    big = tiles[:, sel]
    big = big.transpose(0, 1, 3, 5, 2, 4)
    return big.reshape(nh, 3, NB_BAND * GRID_W, NB_QROWS * GRID_W)


def _attn_nb_kernel(q_ref, kl_ref, vl_ref, kc_ref, vc_ref, bias_ref, o_ref, *, rows):
    half = _idiv(_lane_iota(), HEAD_DIM)
    kc = kc_ref[...]
    vc = _transpose_values(vc_ref)
    vl = _transpose_values(vl_ref)
    tq = NB_QROWS * GRID_W
    tk = NB_BAND * GRID_W
    items = []
    for g in range(rows // NB_QROWS):
        ks = _nb_band_start(g, rows) * GRID_W
        cfg = _nb_config(g, rows)
        q = q_ref[g * tq:(g + 1) * tq, :]
        kb = kl_ref[ks:ks + tk, :]
        vb = vl[:, ks:ks + tk]
        for e in range(2):
            q_e = jnp.where(half == e, q, jnp.zeros_like(q))
            items.append((q_e, [kb, kc], [bias_ref[e, cfg], None], [vb, vc]))
    outs = {}

    def emit(i, o):
        g, e = i // 2, i % 2
        outs[e] = o
        if e == 1:
            o_ref[g * tq:(g + 1) * tq, :] = jnp.where(half == 0, outs[0], outs[1]).astype(BF)

    _attend_pipelined(items, emit)


def _attn_nb(qkv_lat, qkv_ctx, bias):
    bsz, n_tok, _ = qkv_lat.shape
    n_ctx = qkv_ctx.shape[1]
    rows = n_tok // GRID_W
    npair = D_HEADS * HEAD_DIM // LANES
    return pl.pallas_call(
        functools.partial(_attn_nb_kernel, rows=rows),
        out_shape=jax.ShapeDtypeStruct((bsz, n_tok, npair * LANES), BF),
        grid=(npair, bsz),
        in_specs=[
            pl.BlockSpec((None, n_tok, LANES), lambda p, b: (b, 0, p)),
            pl.BlockSpec((None, n_tok, LANES), lambda p, b: (b, 0, npair + p)),
            pl.BlockSpec((None, n_tok, LANES), lambda p, b: (b, 0, 2 * npair + p)),
            pl.BlockSpec((None, n_ctx, LANES), lambda p, b: (b, 0, npair + p)),
            pl.BlockSpec((None, n_ctx, LANES), lambda p, b: (b, 0, 2 * npair + p)),
            pl.BlockSpec((2,) + bias.shape[1:], lambda p, b: (p, 0, 0, 0)),
        ],
        out_specs=pl.BlockSpec((None, n_tok, LANES), lambda p, b: (b, 0, p)),
        compiler_params=_cparams(2),
        name="neighbourhood_attention",
    )(qkv_lat, qkv_lat, qkv_lat, qkv_ctx, qkv_ctx, bias)


def _attn_ctx_kernel(q_ref, k_ref, v_ref, o_ref):
    half = _idiv(_lane_iota(), HEAD_DIM)
    q = q_ref[...]
    zero = jnp.zeros_like(q)
    keys, values = [k_ref[...]], [_transpose_values(v_ref)]
    outs = [_softmax_pv_t(_scores_t(jnp.where(half == e, q, zero), keys, [None]), values) for e in range(2)]
    o_ref[...] = jnp.where(half == 0, outs[0], outs[1]).astype(BF)


def _attn_ctx(qkv_ctx, npair):
    bsz, n_ctx, _ = qkv_ctx.shape
    return pl.pallas_call(
        _attn_ctx_kernel,
        out_shape=jax.ShapeDtypeStruct((bsz, n_ctx, npair * LANES), BF),
        grid=(bsz, npair),
        in_specs=[
            pl.BlockSpec((None, n_ctx, LANES), lambda b, p: (b, 0, p)),
            pl.BlockSpec((None, n_ctx, LANES), lambda b, p: (b, 0, npair + p)),
            pl.BlockSpec((None, n_ctx, LANES), lambda b, p: (b, 0, 2 * npair + p)),
        ],
        out_specs=pl.BlockSpec((None, n_ctx, LANES), lambda b, p: (b, 0, p)),
        compiler_params=_cparams(2),
        name="context_attention",
    )(qkv_ctx, qkv_ctx, qkv_ctx)


def _gla_geometry(rev):
    row = lax.broadcasted_iota(jnp.int32, (C_CHUNK, 1), 0)
    col = lax.broadcasted_iota(jnp.int32, (1, C_CHUNK), 1)
    sub = _idiv(row, C_SUB)
    col_sub = _idiv(col, C_SUB)
    if rev:
        tri, cross, dist = col >= row, col_sub > sub, col - row
    else:
        tri, cross, dist = col <= row, col_sub < sub, row - col
    off_blk = jnp.where(cross, sub, -1)
    dmat = jnp.where(tri, jnp.where(col_sub == sub, dist, -1), -1)
    return tri.astype(BF), sub, off_blk, dmat


def _gla_decay(g, geom):
    return _split_dot_left(geom[0], g) * LOG2E


def _gla_factors(q, k, b, rev, geom):
    sub = geom[1]
    nsub = C_CHUNK // C_SUB
    b_tot = b[0:1] if rev else b[C_CHUNK - 1:C_CHUNK]
    blocks = range(nsub - 1) if rev else range(1, nsub)
    bounds = {}
    ref_rows = jnp.zeros_like(b)
    for a in blocks:
        r = (a + 1) * C_SUB if rev else a * C_SUB - 1
        bounds[a] = b[r:r + 1]
        ref_rows = jnp.where(sub == a, bounds[a], ref_rows)
    q_in = (q * jnp.exp2(b)).astype(BF)
    q_t = (q * jnp.exp2(b - ref_rows)).astype(BF)
    k_ts = {a: (k * jnp.exp2(jnp.minimum(bounds[a] - b, 0.0))).astype(BF) for a in blocks}
    k_end = (k * jnp.exp2(b_tot - b)).astype(BF)
    return q_in, q_t, k_ts, k_end, jnp.exp2(b_tot)


def _gla_cross(factors, v, geom):
    _, q_t, k_ts, k_end, _ = factors
    a_mat = jnp.zeros((C_CHUNK, C_CHUNK), F32)
    for a, k_t in k_ts.items():
        a_mat = jnp.where(geom[2] == a, _dot_nt(q_t, k_t), a_mat)
    return a_mat, _dot_tn(v, k_end)


def _gla_same(q, k, b, a_mat, rev, geom):
    nsub, half = C_CHUNK // C_SUB, C_SUB // 2

    def halves(x):
        x = x.reshape(nsub, C_SUB, x.shape[-1])
        return x[:, :half], x[:, half:]

    first, second = zip(*[halves(x) for x in (q, k, b, a_mat, geom[3])])
    near, far = (second, first) if rev else (first, second)
    (q_n, k_n, b_n, a_n, d_n), (q_f, k_f, b_f, a_f, d_f) = near, far
    r = lax.broadcasted_iota(jnp.int32, (1, half, 1), 1)

    def rot(x, d):
        shift = (half - d) % half if rev else d
        return x if shift == 0 else pltpu.roll(x, shift, axis=1)

    def diag(qq, kk, bb, b_partner):
        return jnp.sum(qq * kk * jnp.exp2(bb - b_partner), axis=-1, keepdims=True)

    a_n = jnp.where(d_n == 0, jnp.sum(q_n * k_n, axis=-1, keepdims=True), a_n)
    a_f = jnp.where(d_f == 0, jnp.sum(q_f * k_f, axis=-1, keepdims=True), a_f)
    for d in range(1, half):
        kn, bn, kf, bf = rot(k_n, d), rot(b_n, d), rot(k_f, d), rot(b_f, d)
        own = (r + d < half) if rev else (r >= d)
        a_n = jnp.where(d_n == d, diag(q_n, kn, b_n, bn), a_n)
        a_f = jnp.where(d_f == d, diag(q_f, jnp.where(own, kf, kn), b_f, jnp.where(own, bf, bn)), a_f)
    for d in range(half, C_SUB):
        a_f = jnp.where(d_f == d, diag(q_f, rot(k_n, d - half), b_f, rot(b_n, d - half)), a_f)
    lo, hi = (a_f, a_n) if rev else (a_n, a_f)
    return jnp.concatenate([lo, hi], axis=1).reshape(C_CHUNK, C_CHUNK)


def _gla_advance(factors, a_mat, v, st, increment):
    q_in, _, _, _, decay = factors
    o = _dot_nt(q_in, st.astype(BF)) + _dot(a_mat.astype(BF), v)
    return o, st * decay + increment


def _gla_kernel(ql_ref, kl_ref, vl_ref, ogl_ref, gfl_ref, gbl_ref,
                qc_ref, kc_ref, vc_ref, ogc_ref, gfc_ref, gbc_ref, ng_ref,
                ol_ref, oc_ref, fl_ref, bl_ref, fc_ref, bc_ref, *, with_ctx_out):
    dk = ql_ref.shape[-1]
    dv = vl_ref.shape[-1]
    n_lat = ql_ref.shape[0] // C_CHUNK
    n_ctx = qc_ref.shape[0] // C_CHUNK
    geom_f = _gla_geometry(False)
    geom_b = _gla_geometry(True)

    def load(q_ref, k_ref, v_ref, g_ref, c):
        sl = pl.ds(pl.multiple_of(c * C_CHUNK, C_CHUNK), C_CHUNK)
        return q_ref[sl, :].astype(F32), k_ref[sl, :].astype(F32), v_ref[sl, :], g_ref[sl, :]

    def chunk_at(c):
        return pl.ds(pl.multiple_of(c * C_CHUNK, C_CHUNK), C_CHUNK)

    def scan_both(refs_f, refs_b, f_ref, b_ref, n_chunks, states):
        group = min(GLA_GROUP, n_chunks)

        def body(step, carry):
            st_f, st_b = carry
            work = []
            for u in range(group):
                j = step * group + u
                work.append((False, geom_f, refs_f, j, f_ref))
                work.append((True, geom_b, refs_b, n_chunks - 1 - j, b_ref))
            data = [load(*refs, c) for _, _, refs, c, _ in work]
            decays = [_gla_decay(d[3], w[1]) for d, w in zip(data, work)]
            factors = [_gla_factors(d[0], d[1], b, w[0], w[1]) for d, b, w in zip(data, decays, work)]
            cross = [_gla_cross(f, d[2], w[1]) for f, d, w in zip(factors, data, work)]
            a_mats = [_gla_same(d[0], d[1], b, x[0], w[0], w[1])
                      for d, b, x, w in zip(data, decays, cross, work)]
            for idx, (rev, _, _, c, out_ref) in enumerate(work):
                st = st_b if rev else st_f
                o, st = _gla_advance(factors[idx], a_mats[idx], data[idx][2], st, cross[idx][1])
                out_ref[chunk_at(c), :] = o
                if rev:
                    st_b = st
                else:
                    st_f = st
            return st_f, st_b
        return lax.fori_loop(0, n_chunks // group, body, states)

    def finish(f_ref, b_ref, og_ref, out_ref):
        def body(i, carry):
            sl = pl.ds(pl.multiple_of(i * GLA_FINISH_ROWS, GLA_FINISH_ROWS), GLA_FINISH_ROWS)
            tot = f_ref[sl, :] + b_ref[sl, :]
            ms = jnp.mean(tot * tot, axis=-1, keepdims=True)
            y = tot * lax.rsqrt(ms + NORM_EPS) * ng_ref[...]
            og = og_ref[sl, :].astype(F32)
            out_ref[sl, :] = (y * (og * _sigmoid(og))).astype(BF)
            return carry
        lax.fori_loop(0, f_ref.shape[0] // GLA_FINISH_ROWS, body, 0)

    zero = jnp.zeros((dv, dk), F32)
    lat_f = (ql_ref, kl_ref, vl_ref, gfl_ref)
    lat_b = (ql_ref, kl_ref, vl_ref, gbl_ref)
    ctx_f = (qc_ref, kc_ref, vc_ref, gfc_ref)
    ctx_b = (qc_ref, kc_ref, vc_ref, gbc_ref)
    states = scan_both(ctx_f, ctx_b, fc_ref, bc_ref, n_ctx, (zero, zero))
    scan_both(lat_f, lat_b, fl_ref, bl_ref, n_lat, states)
    finish(fl_ref, bl_ref, ogl_ref, ol_ref)
    if with_ctx_out:
        finish(fc_ref, bc_ref, ogc_ref, oc_ref)
    else:
        oc_ref[...] = jnp.zeros_like(oc_ref)


def _gla(proj_lat, gates_lat, proj_ctx, gates_ctx, norm_g, with_ctx_out):
    bsz, n_lat, _ = proj_lat.shape
    n_ctx = proj_ctx.shape[1]
    nh = C_HEADS
    key_dim = gates_lat.shape[-1] // 2
    val_dim = (proj_lat.shape[-1] - 2 * key_dim) // 2
    dk = key_dim // nh
    dv = val_dim // nh
    kb, vb = key_dim // dk, val_dim // dv

    def specs(n_tok):
        return [
            pl.BlockSpec((None, n_tok, dk), lambda b, h: (b, 0, h)),
            pl.BlockSpec((None, n_tok, dk), lambda b, h: (b, 0, kb + h)),
            pl.BlockSpec((None, n_tok, dv), lambda b, h: (b, 0, (2 * kb * dk) // dv + h)),
            pl.BlockSpec((None, n_tok, dv), lambda b, h: (b, 0, (2 * kb * dk) // dv + vb + h)),
            pl.BlockSpec((None, n_tok, dk), lambda b, h: (b, 0, h)),
            pl.BlockSpec((None, n_tok, dk), lambda b, h: (b, 0, kb + h)),
        ]

    out_l, out_c = pl.pallas_call(
        functools.partial(_gla_kernel, with_ctx_out=with_ctx_out),
        out_shape=(jax.ShapeDtypeStruct((bsz, n_lat, nh * dv), BF),
                   jax.ShapeDtypeStruct((bsz, n_ctx, nh * dv), BF)),
        grid=(bsz, nh),
        in_specs=specs(n_lat) + specs(n_ctx) + [pl.BlockSpec((1, dv), lambda b, h: (0, 0))],
        out_specs=(pl.BlockSpec((None, n_lat, dv), lambda b, h: (b, 0, h)),
                   pl.BlockSpec((None, n_ctx, dv), lambda b, h: (b, 0, h))),
        scratch_shapes=[pltpu.VMEM((n_lat, dv), F32), pltpu.VMEM((n_lat, dv), F32),
                        pltpu.VMEM((n_ctx, dv), F32), pltpu.VMEM((n_ctx, dv), F32)],
        compiler_params=_cparams(2),
        name="gated_linear_attention",
    )(proj_lat, proj_lat, proj_lat, proj_lat, gates_lat, gates_lat,
      proj_ctx, proj_ctx, proj_ctx, proj_ctx, gates_ctx, gates_ctx, norm_g.reshape(1, dv))
    return out_l, out_c


def kernel(x, c, ctx, c_ctx, mod_w, mod_b, ln1_g, ln1_b, ffn_w_up, ffn_conv_w, ffn_conv_b, ffn_w_down, ln2_g, ln2_b, a_w_qkv, a_q_norm, a_k_norm, a_w_o, b_w_qkv, b_lambda_q1, b_lambda_k1, b_lambda_q2, b_lambda_k2, b_subln, b_w_o, c_w_in, c_w_gate_fwd, c_b_gate_fwd, c_w_gate_bwd, c_b_gate_bwd, c_norm, c_w_o, d_w_qkv, d_rpb, d_w_o):
    bsz, n_lat, dm = x.shape
    n_ctx = ctx.shape[1]
    scale = HEAD_DIM ** -0.5 * LOG2E

    pad_rows = (-(bsz + 1)) % 8
    cond = jnp.concatenate([c, c_ctx[None, :], jnp.zeros((pad_rows, dm), F32)], axis=0)
    mod_all = _modulation(cond, mod_w, mod_b)

    x_lat, x_ctx = x, ctx
    for i in range(DEPTH):
        kind, j = i % N_MIXERS, i // N_MIXERS
        last = i == DEPTH - 1
        m_lat = mod_all[i, :bsz].reshape(bsz, N_MOD, dm)
        m_ctx = jnp.broadcast_to(mod_all[i, bsz].reshape(1, N_MOD, dm), (bsz, N_MOD, dm))

        if kind == 0:
            w = a_w_qkv[j].astype(BF)
            gains = (a_q_norm[j], a_k_norm[j])
            nq, nk = A_HEADS * HEAD_DIM // LANES, A_KV_HEADS * HEAD_DIM // LANES
            p_lat = _proj_qkv(x_lat, m_lat, w, n_q=nq, n_k=nk, n_v=nk, q_scale=scale, norm_gains=gains, rope=True)
            p_ctx = _proj_qkv(x_ctx, m_ctx, w, n_q=nq, n_k=nk, n_v=nk, q_scale=scale, norm_gains=gains)
            a_lat = _attn_gqa(p_lat, [p_ctx, p_lat])
            a_ctx = None if last else _attn_gqa(p_ctx, [p_ctx])
            w_o = a_w_o[j]
        elif kind == 1:
            w = b_w_qkv[j].astype(BF)
            nb = 2 * B_HEADS * HEAD_DIM // LANES
            p_lat = _proj_qkv(x_lat, m_lat, w, n_q=nb, n_k=nb, n_v=nb, q_scale=scale, rope=True)
            p_ctx = _proj_qkv(x_ctx, m_ctx, w, n_q=nb, n_k=nb, n_v=nb, q_scale=scale)
            lam_vecs = jnp.stack([b_lambda_q1[j], b_lambda_k1[j], b_lambda_q2[j], b_lambda_k2[j]])
            lam_init = 0.8 - 0.6 * math.exp(-0.3 * i)
            a_lat = _attn_diff(p_lat, [p_ctx, p_lat], lam_vecs, b_subln[j], lam_init)
            a_ctx = None if last else _attn_diff(p_ctx, [p_ctx], lam_vecs, b_subln[j], lam_init)
            w_o = b_w_o[j]
        elif kind == 2:
            w_in = c_w_in[j]
            n_main = w_in.shape[1] - 2 * C_GATE_RANK
            w_main = w_in[:, :n_main].astype(BF)
            w_z = jnp.pad(w_in[:, n_main:], ((0, 0), (0, LANES - 2 * C_GATE_RANK))).astype(BF)
            kd = c_w_gate_fwd.shape[-1]
            w_gate = jnp.zeros((LANES, 2 * kd), F32)
            w_gate = w_gate.at[:C_GATE_RANK, :kd].set(c_w_gate_fwd[j])
            w_gate = w_gate.at[C_GATE_RANK:2 * C_GATE_RANK, kd:].set(c_w_gate_bwd[j]).astype(BF)
            b_gate = jnp.concatenate([c_b_gate_fwd[j], c_b_gate_bwd[j]]).reshape(1, 2 * kd)
            q_scale = (kd // C_HEADS) ** -0.5
            nq = kd // LANES
            p_lat, g_lat = _proj_gla(x_lat, m_lat, w_main, w_z, w_gate, b_gate, q_scale=q_scale, n_q=nq)
            p_ctx, g_ctx = _proj_gla(x_ctx, m_ctx, w_main, w_z, w_gate, b_gate, q_scale=q_scale, n_q=nq)
            a_lat, a_ctx = _gla(p_lat, g_lat, p_ctx, g_ctx, c_norm[j], not last)
            w_o = c_w_o[j]
        else:
            w = d_w_qkv[j].astype(BF)
            nb = D_HEADS * HEAD_DIM // LANES
            p_lat = _proj_qkv(x_lat, m_lat, w, n_q=nb, n_k=nb, n_v=nb, q_scale=scale)
            p_ctx = _proj_qkv(x_ctx, m_ctx, w, n_q=nb, n_k=nb, n_v=nb, q_scale=scale)
            bias = _nb_bias_table(d_rpb[j], n_lat // GRID_W)
            a_lat = _attn_nb(p_lat, p_ctx, bias)
            a_ctx = None if last else _attn_ctx(p_ctx, nb)
            w_o = d_w_o[j]

        w_o = w_o.astype(BF)
        w_up = ffn_w_up[i].astype(BF)
        w_dn = ffn_w_down[i].astype(BF)
        ffn = (w_o, ln1_g[i], ln1_b[i], w_up, ffn_conv_w[i], ffn_conv_b[i], w_dn, ln2_g[i], ln2_b[i])
        x_lat = _mix_ffn(a_lat, x_lat, m_lat, *ffn)
        if not last:
            x_ctx = _mix_ffn(a_ctx, x_ctx, m_ctx, *ffn)
    return x_lat
```

```python
import functools
import math

import jax
import jax.numpy as jnp
import numpy as np
from jax import lax
from jax.experimental import pallas as pl
from jax.experimental.pallas import tpu as pltpu

F32 = jnp.float32
BF = jnp.bfloat16

DEPTH = 4
N_MIXERS = 4
N_MOD = 6
GRID_W = 64
ROPE_THETA = 10000.0
NORM_EPS = 1e-6
LN_EPS = 1e-5
HEAD_DIM = 64
A_HEADS, A_KV_HEADS = 16, 4
B_HEADS = 8
C_HEADS = 4
C_GATE_RANK = 16
C_GATE_NORMALIZER = 16.0
C_CHUNK = 64
C_SUB = 16
GLA_FINISH_ROWS = 256
GLA_GROUP = 8
D_HEADS = 16
WIN_R, WIN_C = 8, 16
FF_DIM = 2816
DEEPNORM_ALPHA = (2 * DEPTH) ** 0.25
MASK_VALUE = -1e30
LOG2E = math.log2(math.e)

LANES = 128
MXU_WIDTH = 256
VMEM_LIMIT_BYTES = 56 * 1024 * 1024


def _cparams(n_axes):
    return pltpu.CompilerParams(
        dimension_semantics=("arbitrary",) * n_axes,
        vmem_limit_bytes=VMEM_LIMIT_BYTES)


def _dot(a, b):
    return jnp.dot(a, b, preferred_element_type=F32)


def _dot_nt(a, b):
    return lax.dot_general(a, b, (((1,), (1,)), ((), ())), preferred_element_type=F32)


def _dot_tn(a, b):
    return lax.dot_general(a, b, (((0,), (0,)), ((), ())), preferred_element_type=F32)


def _split_dot(x, m_bf16):
    hi = x.astype(BF)
    lo = (x - hi.astype(F32)).astype(BF)
    return _dot(hi, m_bf16) + _dot(lo, m_bf16)


def _split_dot_left(m_bf16, x):
    hi = x.astype(BF)
    lo = (x - hi.astype(F32)).astype(BF)
    return _dot(m_bf16, hi) + _dot(m_bf16, lo)


def _sigmoid(x):
    return 1.0 / (1.0 + jnp.exp(-x))


def _modulate(x_ref, mod_ref, shift_row, scale_row):
    x = x_ref[...]
    return x * (1.0 + mod_ref[scale_row:scale_row + 1, :]) + mod_ref[shift_row:shift_row + 1, :]


def _layer_norm(z, g, b):
    mu = jnp.mean(z, axis=-1, keepdims=True)
    zc = z - mu
    var = jnp.mean(zc * zc, axis=-1, keepdims=True)
    return zc * lax.rsqrt(var + LN_EPS) * g + b


def _rope(y, cos_t, sin_t, lo_half):
    swap = jnp.where(lo_half, pltpu.roll(y, LANES - 32, axis=1), pltpu.roll(y, 32, axis=1))
    return y * cos_t + swap * sin_t


def _idiv(x, n):
    return lax.shift_right_logical(x, int(n).bit_length() - 1)


def _imod(x, n):
    return x & (n - 1)


def _lane_iota(shape=(1, LANES)):
    return lax.broadcasted_iota(jnp.int32, shape, len(shape) - 1)


MOD_COL_BLOCKS = 4


def _mod_kernel(cond_ref, w_ref, b_ref, o_ref):
    cnd = cond_ref[...]
    act = cnd * _sigmoid(cnd)
    o_ref[...] = _dot(act.astype(BF), w_ref[...].astype(BF)) + b_ref[...]


def _modulation(cond, mod_w, mod_b):
    depth, dm, n = mod_w.shape
    rows = cond.shape[0]
    tn = n // MOD_COL_BLOCKS
    return pl.pallas_call(
        _mod_kernel,
        out_shape=jax.ShapeDtypeStruct((depth, rows, n), F32),
        grid=(depth, n // tn),
        in_specs=[
            pl.BlockSpec((rows, dm), lambda i, j: (0, 0)),
            pl.BlockSpec((None, dm, tn), lambda i, j: (i, 0, j)),
            pl.BlockSpec((None, 1, tn), lambda i, j: (i, 0, j)),
        ],
        out_specs=pl.BlockSpec((None, rows, tn), lambda i, j: (i, 0, j)),
        compiler_params=_cparams(2),
        name="adaln_modulation",
    )(cond, mod_w, mod_b.reshape(depth, 1, n))


PROJ_TILE = 512


def _row_tile(n_tok):
    return PROJ_TILE if n_tok % PROJ_TILE == 0 else n_tok


def _proj_call(kernel, x, mod, w, extras, extra_specs, out_shapes, out_specs, name):
    bsz, n_tok, dm = x.shape
    tm = _row_tile(n_tok)
    in_specs = [
        pl.BlockSpec((None, tm, dm), lambda b, i: (b, i, 0)),
        pl.BlockSpec((None, N_MOD, dm), lambda b, i: (b, 0, 0)),
        pl.BlockSpec(w.shape, lambda b, i: (0, 0)),
    ] + list(extra_specs(tm))
    return pl.pallas_call(
        kernel,
        out_shape=out_shapes,
        grid=(bsz, n_tok // tm),
        in_specs=in_specs,
        out_specs=out_specs(tm),
        compiler_params=_cparams(2),
        name=name,
    )(x, mod, w, *extras)


def _full_spec(arr):
    nd = arr.ndim
    return pl.BlockSpec(arr.shape, lambda b, i: (0,) * nd)


def _seg_matrix():
    idx = np.arange(LANES) // HEAD_DIM
    return jnp.asarray((idx[:, None] == idx[None, :]).astype(np.float32) / HEAD_DIM, dtype=BF)


def _rope_tables(n_tok):
    t = jnp.arange(n_tok)
    row = (t // GRID_W).astype(F32)
    col = (t % GRID_W).astype(F32)
    n_freq = HEAD_DIM // 4
    inv = 1.0 / (ROPE_THETA ** (jnp.arange(n_freq, dtype=F32) / n_freq))
    ang = jnp.concatenate([row[:, None] * inv, col[:, None] * inv], axis=-1)
    cos, sin = jnp.cos(ang), jnp.sin(ang)
    cos_t = jnp.concatenate([cos, cos, cos, cos], axis=-1)
    sin_t = jnp.concatenate([-sin, sin, -sin, sin], axis=-1)
    return cos_t, sin_t


def _proj_qkv_kernel(x_ref, mod_ref, w_ref, *rest, n_q, n_k, n_v, norm, rope, q_scale):
    rest = list(rest)
    seg_ref = gq_ref = gk_ref = cos_ref = sin_ref = None
    if norm:
        seg_ref, gq_ref, gk_ref = rest[:3]
        rest = rest[3:]
    if rope:
        cos_ref, sin_ref = rest[:2]
        rest = rest[2:]
    (o_ref,) = rest
    hb = _modulate(x_ref, mod_ref, 0, 1).astype(BF)
    lo_half = _imod(_lane_iota(), HEAD_DIM) < (HEAD_DIM // 2)
    n_blocks = n_q + n_k + n_v
    per_dot = MXU_WIDTH // LANES
    ys = []
    for c0 in range(0, n_blocks, per_dot):
        nb = min(per_dot, n_blocks - c0)
        y2 = _dot(hb, w_ref[:, c0 * LANES:(c0 + nb) * LANES])
        ys += [y2[:, j * LANES:(j + 1) * LANES] for j in range(nb)]
    mean_sq = [_split_dot(y * y, seg_ref[...]) if norm and c < n_q + n_k else None for c, y in enumerate(ys)]
    for c, y in enumerate(ys):
        if c < n_q + n_k:
            if norm:
                gain = gq_ref[...] if c < n_q else gk_ref[...]
                y = y * lax.rsqrt(mean_sq[c] + NORM_EPS) * gain
            if rope:
                y = _rope(y, cos_ref[...], sin_ref[...], lo_half)
            if c < n_q and q_scale != 1.0:
                y = y * q_scale
        o_ref[:, c * LANES:(c + 1) * LANES] = y.astype(BF)


def _proj_qkv(x, mod, w, *, n_q, n_k, n_v, q_scale, norm_gains=None, rope=False):
    bsz, n_tok, _ = x.shape
    n_out = w.shape[1]
    extras, spec_fns = [], []
    if norm_gains is not None:
        gq, gk = norm_gains
        extras += [_seg_matrix(), jnp.tile(gq, 2).reshape(1, LANES), jnp.tile(gk, 2).reshape(1, LANES)]
        spec_fns += [lambda tm, a=a: _full_spec(a) for a in extras]
    if rope:
        cos_t, sin_t = _rope_tables(n_tok)
        extras += [cos_t, sin_t]
        spec_fns += [lambda tm: pl.BlockSpec((tm, LANES), lambda b, i: (i, 0))] * 2
    kern = functools.partial(_proj_qkv_kernel, n_q=n_q, n_k=n_k, n_v=n_v,
                             norm=norm_gains is not None, rope=rope, q_scale=q_scale)
    return _proj_call(
        kern, x, mod, w, extras, lambda tm: [f(tm) for f in spec_fns],
        jax.ShapeDtypeStruct((bsz, n_tok, n_out), BF),
        lambda tm: pl.BlockSpec((None, tm, n_out), lambda b, i: (b, i, 0)),
        "mixer_qkv_projection")


def _log_sigmoid(x):
    return jnp.minimum(x, 0.0) - jnp.log(1.0 + jnp.exp(-jnp.abs(x)))


def _proj_gla_kernel(x_ref, mod_ref, w_ref, wz_ref, wg_ref, bg_ref, o_ref, g_ref, *, q_scale, n_q):
    hb = _modulate(x_ref, mod_ref, 0, 1).astype(BF)
    n_blocks = o_ref.shape[-1] // LANES
    per_dot = MXU_WIDTH // LANES
    for c0 in range(0, n_blocks, per_dot):
        y2 = _dot(hb, w_ref[:, c0 * LANES:(c0 + per_dot) * LANES])
        if c0 < n_q:
            y2 = y2 * q_scale
        o_ref[:, c0 * LANES:(c0 + per_dot) * LANES] = y2.astype(BF)
    z = _dot(hb, wz_ref[...]).astype(BF)
    n_gate = g_ref.shape[-1]
    for c0 in range(0, n_gate, MXU_WIDTH):
        pre = _dot(z, wg_ref[:, c0:c0 + MXU_WIDTH]) + bg_ref[:, c0:c0 + MXU_WIDTH]
        g_ref[:, c0:c0 + MXU_WIDTH] = _log_sigmoid(pre) * (1.0 / C_GATE_NORMALIZER)


def _proj_gla(x, mod, w_main, w_z, w_gate, b_gate, *, q_scale, n_q):
    bsz, n_tok, _ = x.shape
    n_out, n_gate = w_main.shape[1], w_gate.shape[1]
    extras = [w_z, w_gate, b_gate]
    kern = functools.partial(_proj_gla_kernel, q_scale=q_scale, n_q=n_q)
    return _proj_call(
        kern, x, mod, w_main, extras, lambda tm: [_full_spec(a) for a in extras],
        (jax.ShapeDtypeStruct((bsz, n_tok, n_out), BF), jax.ShapeDtypeStruct((bsz, n_tok, n_gate), F32)),
        lambda tm: (pl.BlockSpec((None, tm, n_out), lambda b, i: (b, i, 0)),
                    pl.BlockSpec((None, tm, n_gate), lambda b, i: (b, i, 0))),
        "gla_projection")


FFN_CHUNK = 256
FFN_TILE = 512
FFN_ROW_PARTS = 2
HALO = 16


def _mix_ffn_kernel(a_ref, ap_ref, an_ref, x_ref, xp_ref, xn_ref, mod_ref, wo_ref, g1_ref, b1_ref,
                    wu_ref, cw_ref, cb_ref, wd_ref, g2_ref, b2_ref, o_ref, act_ref):
    tm = x_ref.shape[0]
    i = pl.program_id(1)
    has_prev = (i > 0).astype(F32)
    has_next = (i < pl.num_programs(1) - 1).astype(F32)
    n_ext = tm + 2 * HALO

    a_ext = jnp.concatenate([a_ref[...], an_ref[...], ap_ref[...]], axis=0)
    x_ext = jnp.concatenate([x_ref[...], xn_ref[...], xp_ref[...]], axis=0)
    he = n_ext // FFN_ROW_PARTS
    ext_starts = list(range(0, n_ext, he))
    y_next = _dot(a_ext[0:he], wo_ref[...])
    x_mid = []
    for j, r in enumerate(ext_starts):
        y = y_next
        if j + 1 < len(ext_starts):
            y_next = _dot(a_ext[r + he:r + 2 * he], wo_ref[...])
        z1 = DEEPNORM_ALPHA * x_ext[r:r + he] + mod_ref[2:3, :] * y
        x_mid.append(_layer_norm(z1, g1_ref[...], b1_ref[...]))
    x_mid = jnp.concatenate(x_mid, axis=0)
    h = x_mid * (1.0 + mod_ref[4:5, :]) + mod_ref[3:4, :]
    hb = jnp.concatenate([h[:tm], h[tm:tm + HALO] * has_next, h[tm + HALO:] * has_prev], axis=0).astype(BF)

    def up(c):
        cg = c * FFN_CHUNK
        return _dot(hb, wu_ref[:, cg:cg + FFN_CHUNK]), _dot(hb, wu_ref[:, FF_DIM + cg:FF_DIM + cg + FFN_CHUNK])

    def conv(u_ext, c0):
        u_dn = pltpu.roll(u_ext, 1, axis=0)[:tm]
        u_up = pltpu.roll(u_ext, n_ext - 1, axis=0)[:tm]
        w0 = cw_ref[0:1, c0:c0 + FFN_CHUNK]
        w1 = cw_ref[1:2, c0:c0 + FFN_CHUNK]
        w2 = cw_ref[2:3, c0:c0 + FFN_CHUNK]
        return cb_ref[:, c0:c0 + FFN_CHUNK] + u_dn * w0 + u_ext[:tm] * w1 + u_up * w2

    for c in range(FF_DIM // FFN_CHUNK):
        cg = c * FFN_CHUNK
        u_gate, u_val = up(c)
        gate = conv(u_gate, cg)
        val = conv(u_val, FF_DIM + cg)
        act_ref[:, cg:cg + FFN_CHUNK] = (gate * _sigmoid(gate) * val).astype(BF)
    hm = tm // FFN_ROW_PARTS
    starts = list(range(0, tm, hm))
    y_next = _dot(act_ref[0:hm, :], wd_ref[...])
    for j, r in enumerate(starts):
        y = y_next
        if j + 1 < len(starts):
            y_next = _dot(act_ref[r + hm:r + 2 * hm, :], wd_ref[...])
        z2 = DEEPNORM_ALPHA * x_mid[r:r + hm] + mod_ref[5:6, :] * y
        o_ref[r:r + hm, :] = _layer_norm(z2, g2_ref[...], b2_ref[...])


def _mix_ffn(a, x, mod, w_o, ln1_g, ln1_b, w_up, conv_w, conv_b, w_down, ln2_g, ln2_b):
    bsz, n_tok, dm = x.shape
    k_in = a.shape[-1]
    tm = FFN_TILE if n_tok % FFN_TILE == 0 else n_tok
    n_halo = n_tok // HALO
    per = tm // HALO
    tile = lambda b, i: (b, i, 0)
    prev = lambda b, i: (b, jnp.maximum(i * per - 1, 0), 0)
    nxt = lambda b, i: (b, jnp.minimum((i + 1) * per, n_halo - 1), 0)
    const2 = lambda b, i: (0, 0)
    resident = functools.partial(pl.BlockSpec, index_map=const2, pipeline_mode=pl.Buffered(1))
    row = lambda v: v.reshape(1, -1)
    return pl.pallas_call(
        _mix_ffn_kernel,
        out_shape=jax.ShapeDtypeStruct((bsz, n_tok, dm), F32),
        grid=(bsz, n_tok // tm),
        in_specs=[
            pl.BlockSpec((None, tm, k_in), tile),
            pl.BlockSpec((None, HALO, k_in), prev),
            pl.BlockSpec((None, HALO, k_in), nxt),
            pl.BlockSpec((None, tm, dm), tile),
            pl.BlockSpec((None, HALO, dm), prev),
            pl.BlockSpec((None, HALO, dm), nxt),
            pl.BlockSpec((None, N_MOD, dm), lambda b, i: (b, 0, 0)),
            resident(w_o.shape),
            pl.BlockSpec((1, dm), const2),
            pl.BlockSpec((1, dm), const2),
            resident(w_up.shape),
            pl.BlockSpec(conv_w.shape, const2),
            pl.BlockSpec((1, 2 * FF_DIM), const2),
            resident(w_down.shape),
            pl.BlockSpec((1, dm), const2),
            pl.BlockSpec((1, dm), const2),
        ],
        out_specs=pl.BlockSpec((None, tm, dm), tile),
        scratch_shapes=[pltpu.VMEM((tm, FF_DIM), BF)],
        compiler_params=_cparams(2),
        name="outproj_convffn_layernorm",
    )(a, a, a, x, x, x, mod, w_o, row(ln1_g), row(ln1_b), w_up, conv_w, row(conv_b), w_down,
      row(ln2_g), row(ln2_b))


ATT_SUB = 512
ATT_TILE = 2048
ATT_BLOCKS = 2


def _attn_tile(n_tok):
    return ATT_TILE if n_tok % ATT_TILE == 0 else n_tok


ATT_KEYS = 1024


def _key_blocks(n_keys):
    return [slice(r, min(r + ATT_KEYS, n_keys)) for r in range(0, n_keys, ATT_KEYS)]


def _transpose_values(v_ref):
    return v_ref[...].astype(F32).T.astype(BF)


def _scores_t(q, keys, biases_t):
    return [_dot_nt(k, q) if bias is None else _dot_nt(k, q) + bias for k, bias in zip(keys, biases_t)]


def _probs_t(scores):
    m = functools.reduce(jnp.maximum, [jnp.max(s, axis=0, keepdims=True) for s in scores])
    probs = [jnp.exp2(s - m) for s in scores]
    denom = functools.reduce(lambda a, b: a + b, [jnp.sum(p, axis=0, keepdims=True) for p in probs])
    return [p.astype(BF) for p in probs], denom


def _pv_t(probs, denom, values_t):
    out_t = functools.reduce(lambda a, b: a + b, [_dot(vt, p) for p, vt in zip(probs, values_t)])
    return (out_t / denom).T


def _softmax_pv_t(scores, values_t):
    return _pv_t(*_probs_t(scores), values_t)


def _attend_pipelined(items, emit):
    n = len(items)

    def n_seg(j):
        return len(items[j][1]) if j < n else 0

    def score_seg(j, s):
        q, keys, biases_t, _ = items[j]
        sc = _dot_nt(keys[s], q)
        return sc if biases_t[s] is None else sc + biases_t[s]

    def col_max(j):
        return functools.reduce(jnp.maximum, [jnp.max(s, axis=0, keepdims=True) for s in scores[j]])

    scores = {j: [score_seg(j, s) for s in range(n_seg(j))] for j in range(min(2, n))}
    probs = {0: _probs_t(scores.pop(0))}
    for i in range(n):
        m_next = col_max(i + 1) if i + 1 < n else None
        new_scores, new_probs, new_denom, out_t = [], [], None, None
        for s in range(max(n_seg(i), n_seg(i + 1), n_seg(i + 2))):
            if s < n_seg(i + 2):
                new_scores.append(score_seg(i + 2, s))
            if s < n_seg(i + 1):
                p = jnp.exp2(scores[i + 1][s] - m_next)
                part = jnp.sum(p, axis=0, keepdims=True)
                new_denom = part if new_denom is None else new_denom + part
                new_probs.append(p.astype(BF))
            if s < n_seg(i):
                part = _dot(items[i][3][s], probs[i][0][s])
                out_t = part if out_t is None else out_t + part
        emit(i, (out_t / probs.pop(i)[1]).T)
        if i + 1 < n:
            scores.pop(i + 1)
            probs[i + 1] = (new_probs, new_denom)
        if i + 2 < n:
            scores[i + 2] = new_scores


def _attn_gqa_kernel(q_ref, *rest, n_seg, n_blk):
    k_refs, v_refs, o_ref = rest[:n_seg], rest[n_seg:2 * n_seg], rest[2 * n_seg]
    grp = A_HEADS // A_KV_HEADS
    half = _idiv(_lane_iota(), HEAD_DIM)
    values_t = [_transpose_values(r) for r in v_refs]
    items, place = [], []
    for blk in range(n_blk):
        all_blocks = n_blk * 2 == A_HEADS
        head = (blk if all_blocks else pl.program_id(1) * n_blk + blk) * 2
        kv_pos = (head // grp) % 2
        kv_blk = blk * 2 // grp // 2 if all_blocks else 0
        lanes = slice(kv_blk * LANES, (kv_blk + 1) * LANES)
        keys = [r[rows, lanes] for r in k_refs for rows in _key_blocks(r.shape[0])]
        values = [vt[lanes, rows] for vt in values_t for rows in _key_blocks(vt.shape[1])]
        for r0 in range(0, q_ref.shape[0], ATT_SUB):
            q = q_ref[r0:r0 + ATT_SUB, blk * LANES:(blk + 1) * LANES].astype(F32)
            q_sw = pltpu.roll(q, HEAD_DIM, axis=1)
            for e in range(2):
                q_e = jnp.where(kv_pos == e, q, q_sw)
                q_e = jnp.where(half == kv_pos, q_e, 0.0).astype(BF)
                items.append((q_e, keys, [None] * len(keys), values))
                place.append((blk, r0, e, kv_pos))
    outs = {}

    def emit(i, o):
        blk, r0, e, kv_pos = place[i]
        outs[e] = jnp.where(kv_pos == e, o, pltpu.roll(o, HEAD_DIM, axis=1))
        if e == 1:
            o_ref[r0:r0 + ATT_SUB, blk * LANES:(blk + 1) * LANES] = (
                jnp.where(half == 0, outs[0], outs[1]).astype(BF))

    _attend_pipelined(items, emit)


def _blocks_per_step(n_q_tok, n_blocks):
    return n_blocks if n_q_tok < ATT_SUB else ATT_BLOCKS


def _attn_gqa(q_src, kv_srcs):
    bsz, n_q_tok, _ = q_src.shape
    tq = _attn_tile(n_q_tok)
    n_qb = A_HEADS * HEAD_DIM // LANES
    n_kb = A_KV_HEADS * HEAD_DIM // LANES
    per_kb = n_qb // n_kb
    n_seg = len(kv_srcs)
    n_blk = _blocks_per_step(n_q_tok, n_qb)
    assert n_blk == n_qb or per_kb % n_blk == 0
    kv_w = LANES if n_blk < n_qb else n_kb * LANES
    k0, v0 = n_qb * LANES // kv_w, (n_qb + n_kb) * LANES // kv_w
    k_specs = [pl.BlockSpec((None, s.shape[1], kv_w), lambda b, p, i: (b, 0, k0 + p * n_blk // per_kb))
               for s in kv_srcs]
    v_specs = [pl.BlockSpec((None, s.shape[1], kv_w), lambda b, p, i: (b, 0, v0 + p * n_blk // per_kb))
               for s in kv_srcs]
    return pl.pallas_call(
        functools.partial(_attn_gqa_kernel, n_seg=n_seg, n_blk=n_blk),
        out_shape=jax.ShapeDtypeStruct((bsz, n_q_tok, n_qb * LANES), BF),
        grid=(bsz, n_qb // n_blk, n_q_tok // tq),
        in_specs=[pl.BlockSpec((None, tq, n_blk * LANES), lambda b, p, i: (b, i, p))] + k_specs + v_specs,
        out_specs=pl.BlockSpec((None, tq, n_blk * LANES), lambda b, p, i: (b, i, p)),
        compiler_params=_cparams(3),
        name="gqa_attention",
    )(q_src, *kv_srcs, *kv_srcs)


def _attn_diff_kernel(q_ref, lam_ref, subln_ref, *rest, n_seg, lam_init):
    k_refs, v_refs, o_ref = rest[:n_seg], rest[n_seg:2 * n_seg], rest[2 * n_seg]
    lam_v = lam_ref[...]
    lam = (jnp.exp(jnp.sum(lam_v[0:1] * lam_v[1:2], axis=-1, keepdims=True))
           - jnp.exp(jnp.sum(lam_v[2:3] * lam_v[3:4], axis=-1, keepdims=True)) + lam_init)
    half = _idiv(_lane_iota(), HEAD_DIM)
    values_t = [_transpose_values(r) for r in v_refs]
    n_blk = q_ref.shape[1] // LANES
    items, place = [], []
    for blk in range(n_blk):
        lanes = slice(blk * LANES, (blk + 1) * LANES)
        keys = [r[rows, lanes] for r in k_refs for rows in _key_blocks(r.shape[0])]
        values = [vt[lanes, rows] for vt in values_t for rows in _key_blocks(vt.shape[1])]
        for r0 in range(0, q_ref.shape[0], ATT_SUB):
            q = q_ref[r0:r0 + ATT_SUB, lanes]
            for e in range(2):
                items.append((jnp.where(half == e, q, jnp.zeros_like(q)), keys, [None] * len(keys), values))
                place.append((lanes, r0, e))
    outs = {}

    def emit(i, o):
        lanes, r0, e = place[i]
        outs[e] = o
        if e == 1:
            o = outs[0] - lam * outs[1]
            ms = jnp.mean(o * o, axis=-1, keepdims=True)
            o = o * lax.rsqrt(ms + NORM_EPS) * subln_ref[...] * (1.0 - lam_init)
            o_ref[r0:r0 + ATT_SUB, lanes] = o.astype(BF)

    _attend_pipelined(items, emit)


def _attn_diff(q_src, kv_srcs, lam_vecs, subln, lam_init):
    bsz, n_q_tok, _ = q_src.shape
    tq = _attn_tile(n_q_tok)
    nh = B_HEADS
    n_seg = len(kv_srcs)
    n_blk = _blocks_per_step(n_q_tok, nh)
    w = n_blk * LANES
    steps = nh // n_blk
    k_specs = [pl.BlockSpec((None, s.shape[1], w), lambda b, h, i: (b, 0, steps + h)) for s in kv_srcs]
    v_specs = [pl.BlockSpec((None, s.shape[1], w), lambda b, h, i: (b, 0, 2 * steps + h)) for s in kv_srcs]
    return pl.pallas_call(
        functools.partial(_attn_diff_kernel, n_seg=n_seg, lam_init=lam_init),
        out_shape=jax.ShapeDtypeStruct((bsz, n_q_tok, nh * LANES), BF),
        grid=(bsz, steps, n_q_tok // tq),
        in_specs=[pl.BlockSpec((None, tq, w), lambda b, h, i: (b, i, h)),
                  pl.BlockSpec(lam_vecs.shape, lambda b, h, i: (0, 0)),
                  pl.BlockSpec((1, LANES), lambda b, h, i: (0, 0))] + k_specs + v_specs,
        out_specs=pl.BlockSpec((None, tq, w), lambda b, h, i: (b, i, h)),
        compiler_params=_cparams(3),
        name="differential_attention",
    )(q_src, lam_vecs, subln.reshape(1, LANES), *kv_srcs, *kv_srcs)


NB_QROWS = 4
NB_BAND = 12


def _nb_band_start(g, rows):
    return int(np.clip(NB_QROWS * g - NB_QROWS, 0, rows - NB_BAND))


def _nb_config(g, rows):
    n_g = rows // NB_QROWS
    return 0 if g == 0 else (2 if g == n_g - 1 else 1)


def _nb_bias_table(rpb, rows):
    nh = rpb.shape[0]
    wr = min(WIN_R, rows)
    cq = np.arange(GRID_W)
    c0 = np.clip(cq - WIN_C // 2, 0, GRID_W - WIN_C)
    col_in = (cq[None, :] >= c0[:, None]) & (cq[None, :] < c0[:, None] + WIN_C)
    dc_idx = np.clip(cq[None, :] - cq[:, None], -(WIN_C - 1), WIN_C - 1) + WIN_C - 1
    n_dr = 2 * WIN_R - 1
    tiles = rpb[:, :, dc_idx]
    tiles = jnp.where(col_in[None, None], tiles * LOG2E, MASK_VALUE)
    tiles = jnp.concatenate([tiles, jnp.full((nh, 1, GRID_W, GRID_W), MASK_VALUE, F32)], axis=1)
    n_g = rows // NB_QROWS
    sel = np.full((3, NB_QROWS, NB_BAND), n_dr, dtype=np.int32)
    for cfg, g in ((0, 0), (1, 1), (2, n_g - 1)):
        start = _nb_band_start(g, rows)
        for a in range(NB_QROWS):
            r = NB_QROWS * g + a
            r0 = int(np.clip(r - wr // 2, 0, rows - wr))
            for jb in range(NB_BAND):
                kr = start + jb
                if r0 <= kr < r0 + wr:
                    sel[cfg, a, jb] = kr - r + (WIN_R - 1)
    big = tiles[:, sel]
    big = big.transpose(0, 1, 3, 5, 2, 4)
    return big.reshape(nh, 3, NB_BAND * GRID_W, NB_QROWS * GRID_W)


def _attn_nb_kernel(q_ref, kl_ref, vl_ref, kc_ref, vc_ref, bias_ref, o_ref, *, rows):
    half = _idiv(_lane_iota(), HEAD_DIM)
    kc = kc_ref[...]
    vc = _transpose_values(vc_ref)
    vl = _transpose_values(vl_ref)
    tq = NB_QROWS * GRID_W
    tk = NB_BAND * GRID_W
    items = []
    for g in range(rows // NB_QROWS):
        ks = _nb_band_start(g, rows) * GRID_W
        cfg = _nb_config(g, rows)
        q = q_ref[g * tq:(g + 1) * tq, :]
        kb = kl_ref[ks:ks + tk, :]
        vb = vl[:, ks:ks + tk]
        for e in range(2):
            q_e = jnp.where(half == e, q, jnp.zeros_like(q))
            items.append((q_e, [kb, kc], [bias_ref[e, cfg], None], [vb, vc]))
    outs = {}

    def emit(i, o):
        g, e = i // 2, i % 2
        outs[e] = o
        if e == 1:
            o_ref[g * tq:(g + 1) * tq, :] = jnp.where(half == 0, outs[0], outs[1]).astype(BF)

    _attend_pipelined(items, emit)


def _attn_nb(qkv_lat, qkv_ctx, bias):
    bsz, n_tok, _ = qkv_lat.shape
    n_ctx = qkv_ctx.shape[1]
    rows = n_tok // GRID_W
    npair = D_HEADS * HEAD_DIM // LANES
    return pl.pallas_call(
        functools.partial(_attn_nb_kernel, rows=rows),
        out_shape=jax.ShapeDtypeStruct((bsz, n_tok, npair * LANES), BF),
        grid=(npair, bsz),
        in_specs=[
            pl.BlockSpec((None, n_tok, LANES), lambda p, b: (b, 0, p)),
            pl.BlockSpec((None, n_tok, LANES), lambda p, b: (b, 0, npair + p)),
            pl.BlockSpec((None, n_tok, LANES), lambda p, b: (b, 0, 2 * npair + p)),
            pl.BlockSpec((None, n_ctx, LANES), lambda p, b: (b, 0, npair + p)),
            pl.BlockSpec((None, n_ctx, LANES), lambda p, b: (b, 0, 2 * npair + p)),
            pl.BlockSpec((2,) + bias.shape[1:], lambda p, b: (p, 0, 0, 0)),
        ],
        out_specs=pl.BlockSpec((None, n_tok, LANES), lambda p, b: (b, 0, p)),
        compiler_params=_cparams(2),
        name="neighbourhood_attention",
    )(qkv_lat, qkv_lat, qkv_lat, qkv_ctx, qkv_ctx, bias)


def _attn_ctx_kernel(q_ref, k_ref, v_ref, o_ref):
    half = _idiv(_lane_iota(), HEAD_DIM)
    q = q_ref[...]
    zero = jnp.zeros_like(q)
    keys, values = [k_ref[...]], [_transpose_values(v_ref)]
    outs = [_softmax_pv_t(_scores_t(jnp.where(half == e, q, zero), keys, [None]), values) for e in range(2)]
    o_ref[...] = jnp.where(half == 0, outs[0], outs[1]).astype(BF)


def _attn_ctx(qkv_ctx, npair):
    bsz, n_ctx, _ = qkv_ctx.shape
    return pl.pallas_call(
        _attn_ctx_kernel,
        out_shape=jax.ShapeDtypeStruct((bsz, n_ctx, npair * LANES), BF),
        grid=(bsz, npair),
        in_specs=[
            pl.BlockSpec((None, n_ctx, LANES), lambda b, p: (b, 0, p)),
            pl.BlockSpec((None, n_ctx, LANES), lambda b, p: (b, 0, npair + p)),
            pl.BlockSpec((None, n_ctx, LANES), lambda b, p: (b, 0, 2 * npair + p)),
        ],
        out_specs=pl.BlockSpec((None, n_ctx, LANES), lambda b, p: (b, 0, p)),
        compiler_params=_cparams(2),
        name="context_attention",
    )(qkv_ctx, qkv_ctx, qkv_ctx)


def _gla_geometry(rev):
    row = lax.broadcasted_iota(jnp.int32, (C_CHUNK, 1), 0)
    col = lax.broadcasted_iota(jnp.int32, (1, C_CHUNK), 1)
    sub = _idiv(row, C_SUB)
    col_sub = _idiv(col, C_SUB)
    if rev:
        tri, cross, dist = col >= row, col_sub > sub, col - row
    else:
        tri, cross, dist = col <= row, col_sub < sub, row - col
    off_blk = jnp.where(cross, sub, -1)
    dmat = jnp.where(tri, jnp.where(col_sub == sub, dist, -1), -1)
    return tri.astype(BF), sub, off_blk, dmat


def _gla_decay(g, geom):
    return _split_dot_left(geom[0], g) * LOG2E


def _gla_factors(q, k, b, rev, geom):
    sub = geom[1]
    nsub = C_CHUNK // C_SUB
    b_tot = b[0:1] if rev else b[C_CHUNK - 1:C_CHUNK]
    blocks = range(nsub - 1) if rev else range(1, nsub)
    bounds = {}
    ref_rows = jnp.zeros_like(b)
    for a in blocks:
        r = (a + 1) * C_SUB if rev else a * C_SUB - 1
        bounds[a] = b[r:r + 1]
        ref_rows = jnp.where(sub == a, bounds[a], ref_rows)
    q_in = (q * jnp.exp2(b)).astype(BF)
    q_t = (q * jnp.exp2(b - ref_rows)).astype(BF)
    k_ts = {a: (k * jnp.exp2(jnp.minimum(bounds[a] - b, 0.0))).astype(BF) for a in blocks}
    k_end = (k * jnp.exp2(b_tot - b)).astype(BF)
    return q_in, q_t, k_ts, k_end, jnp.exp2(b_tot)


def _gla_cross(factors, v, geom):
    _, q_t, k_ts, k_end, _ = factors
    a_mat = jnp.zeros((C_CHUNK, C_CHUNK), F32)
    for a, k_t in k_ts.items():
        a_mat = jnp.where(geom[2] == a, _dot_nt(q_t, k_t), a_mat)
    return a_mat, _dot_tn(v, k_end)


def _gla_same(q, k, b, a_mat, rev, geom):
    nsub, half = C_CHUNK // C_SUB, C_SUB // 2

    def halves(x):
        x = x.reshape(nsub, C_SUB, x.shape[-1])
        return x[:, :half], x[:, half:]

    first, second = zip(*[halves(x) for x in (q, k, b, a_mat, geom[3])])
    near, far = (second, first) if rev else (first, second)
    (q_n, k_n, b_n, a_n, d_n), (q_f, k_f, b_f, a_f, d_f) = near, far
    r = lax.broadcasted_iota(jnp.int32, (1, half, 1), 1)

    def rot(x, d):
        shift = (half - d) % half if rev else d
        return x if shift == 0 else pltpu.roll(x, shift, axis=1)

    def diag(qq, kk, bb, b_partner):
        return jnp.sum(qq * kk * jnp.exp2(bb - b_partner), axis=-1, keepdims=True)

    a_n = jnp.where(d_n == 0, jnp.sum(q_n * k_n, axis=-1, keepdims=True), a_n)
    a_f = jnp.where(d_f == 0, jnp.sum(q_f * k_f, axis=-1, keepdims=True), a_f)
    for d in range(1, half):
        kn, bn, kf, bf = rot(k_n, d), rot(b_n, d), rot(k_f, d), rot(b_f, d)
        own = (r + d < half) if rev else (r >= d)
        a_n = jnp.where(d_n == d, diag(q_n, kn, b_n, bn), a_n)
        a_f = jnp.where(d_f == d, diag(q_f, jnp.where(own, kf, kn), b_f, jnp.where(own, bf, bn)), a_f)
    for d in range(half, C_SUB):
        a_f = jnp.where(d_f == d, diag(q_f, rot(k_n, d - half), b_f, rot(b_n, d - half)), a_f)
    lo, hi = (a_f, a_n) if rev else (a_n, a_f)
    return jnp.concatenate([lo, hi], axis=1).reshape(C_CHUNK, C_CHUNK)


def _gla_advance(factors, a_mat, v, st, increment):
    q_in, _, _, _, decay = factors
    o = _dot_nt(q_in, st.astype(BF)) + _dot(a_mat.astype(BF), v)
    return o, st * decay + increment


def _gla_kernel(ql_ref, kl_ref, vl_ref, ogl_ref, gfl_ref, gbl_ref,
                qc_ref, kc_ref, vc_ref, ogc_ref, gfc_ref, gbc_ref, ng_ref,
                ol_ref, oc_ref, fl_ref, bl_ref, fc_ref, bc_ref, *, with_ctx_out):
    dk = ql_ref.shape[-1]
    dv = vl_ref.shape[-1]
    n_lat = ql_ref.shape[0] // C_CHUNK
    n_ctx = qc_ref.shape[0] // C_CHUNK
    geom_f = _gla_geometry(False)
    geom_b = _gla_geometry(True)

    def load(q_ref, k_ref, v_ref, g_ref, c):
        sl = pl.ds(pl.multiple_of(c * C_CHUNK, C_CHUNK), C_CHUNK)
        return q_ref[sl, :].astype(F32), k_ref[sl, :].astype(F32), v_ref[sl, :], g_ref[sl, :]

    def chunk_at(c):
        return pl.ds(pl.multiple_of(c * C_CHUNK, C_CHUNK), C_CHUNK)

    def scan_both(refs_f, refs_b, f_ref, b_ref, n_chunks, states):
        group = min(GLA_GROUP, n_chunks)

        def body(step, carry):
            st_f, st_b = carry
            work = []
            for u in range(group):
                j = step * group + u
                work.append((False, geom_f, refs_f, j, f_ref))
                work.append((True, geom_b, refs_b, n_chunks - 1 - j, b_ref))
            data = [load(*refs, c) for _, _, refs, c, _ in work]
            decays = [_gla_decay(d[3], w[1]) for d, w in zip(data, work)]
            factors = [_gla_factors(d[0], d[1], b, w[0], w[1]) for d, b, w in zip(data, decays, work)]
            cross = [_gla_cross(f, d[2], w[1]) for f, d, w in zip(factors, data, work)]
            a_mats = [_gla_same(d[0], d[1], b, x[0], w[0], w[1])
                      for d, b, x, w in zip(data, decays, cross, work)]
            for idx, (rev, _, _, c, out_ref) in enumerate(work):
                st = st_b if rev else st_f
                o, st = _gla_advance(factors[idx], a_mats[idx], data[idx][2], st, cross[idx][1])
                out_ref[chunk_at(c), :] = o
                if rev:
                    st_b = st
                else:
                    st_f = st
            return st_f, st_b
        return lax.fori_loop(0, n_chunks // group, body, states)

    def finish(f_ref, b_ref, og_ref, out_ref):
        def body(i, carry):
            sl = pl.ds(pl.multiple_of(i * GLA_FINISH_ROWS, GLA_FINISH_ROWS), GLA_FINISH_ROWS)
            tot = f_ref[sl, :] + b_ref[sl, :]
            ms = jnp.mean(tot * tot, axis=-1, keepdims=True)
            y = tot * lax.rsqrt(ms + NORM_EPS) * ng_ref[...]
            og = og_ref[sl, :].astype(F32)
            out_ref[sl, :] = (y * (og * _sigmoid(og))).astype(BF)
            return carry
        lax.fori_loop(0, f_ref.shape[0] // GLA_FINISH_ROWS, body, 0)

    zero = jnp.zeros((dv, dk), F32)
    lat_f = (ql_ref, kl_ref, vl_ref, gfl_ref)
    lat_b = (ql_ref, kl_ref, vl_ref, gbl_ref)
    ctx_f = (qc_ref, kc_ref, vc_ref, gfc_ref)
    ctx_b = (qc_ref, kc_ref, vc_ref, gbc_ref)
    states = scan_both(ctx_f, ctx_b, fc_ref, bc_ref, n_ctx, (zero, zero))
    scan_both(lat_f, lat_b, fl_ref, bl_ref, n_lat, states)
    finish(fl_ref, bl_ref, ogl_ref, ol_ref)
    if with_ctx_out:
        finish(fc_ref, bc_ref, ogc_ref, oc_ref)
    else:
        oc_ref[...] = jnp.zeros_like(oc_ref)


def _gla(proj_lat, gates_lat, proj_ctx, gates_ctx, norm_g, with_ctx_out):
    bsz, n_lat, _ = proj_lat.shape
    n_ctx = proj_ctx.shape[1]
    nh = C_HEADS
    key_dim = gates_lat.shape[-1] // 2
    val_dim = (proj_lat.shape[-1] - 2 * key_dim) // 2
    dk = key_dim // nh
    dv = val_dim // nh
    kb, vb = key_dim // dk, val_dim // dv

    def specs(n_tok):
        return [
            pl.BlockSpec((None, n_tok, dk), lambda b, h: (b, 0, h)),
            pl.BlockSpec((None, n_tok, dk), lambda b, h: (b, 0, kb + h)),
            pl.BlockSpec((None, n_tok, dv), lambda b, h: (b, 0, (2 * kb * dk) // dv + h)),
            pl.BlockSpec((None, n_tok, dv), lambda b, h: (b, 0, (2 * kb * dk) // dv + vb + h)),
            pl.BlockSpec((None, n_tok, dk), lambda b, h: (b, 0, h)),
            pl.BlockSpec((None, n_tok, dk), lambda b, h: (b, 0, kb + h)),
        ]

    out_l, out_c = pl.pallas_call(
        functools.partial(_gla_kernel, with_ctx_out=with_ctx_out),
        out_shape=(jax.ShapeDtypeStruct((bsz, n_lat, nh * dv), BF),
                   jax.ShapeDtypeStruct((bsz, n_ctx, nh * dv), BF)),
        grid=(bsz, nh),
        in_specs=specs(n_lat) + specs(n_ctx) + [pl.BlockSpec((1, dv), lambda b, h: (0, 0))],
        out_specs=(pl.BlockSpec((None, n_lat, dv), lambda b, h: (b, 0, h)),
                   pl.BlockSpec((None, n_ctx, dv), lambda b, h: (b, 0, h))),
        scratch_shapes=[pltpu.VMEM((n_lat, dv), F32), pltpu.VMEM((n_lat, dv), F32),
                        pltpu.VMEM((n_ctx, dv), F32), pltpu.VMEM((n_ctx, dv), F32)],
        compiler_params=_cparams(2),
        name="gated_linear_attention",
    )(proj_lat, proj_lat, proj_lat, proj_lat, gates_lat, gates_lat,
      proj_ctx, proj_ctx, proj_ctx, proj_ctx, gates_ctx, gates_ctx, norm_g.reshape(1, dv))
    return out_l, out_c


def kernel(x, c, ctx, c_ctx, mod_w, mod_b, ln1_g, ln1_b, ffn_w_up, ffn_conv_w, ffn_conv_b, ffn_w_down, ln2_g, ln2_b, a_w_qkv, a_q_norm, a_k_norm, a_w_o, b_w_qkv, b_lambda_q1, b_lambda_k1, b_lambda_q2, b_lambda_k2, b_subln, b_w_o, c_w_in, c_w_gate_fwd, c_b_gate_fwd, c_w_gate_bwd, c_b_gate_bwd, c_norm, c_w_o, d_w_qkv, d_rpb, d_w_o):
    bsz, n_lat, dm = x.shape
    n_ctx = ctx.shape[1]
    scale = HEAD_DIM ** -0.5 * LOG2E

    pad_rows = (-(bsz + 1)) % 8
    cond = jnp.concatenate([c, c_ctx[None, :], jnp.zeros((pad_rows, dm), F32)], axis=0)
    mod_all = _modulation(cond, mod_w, mod_b)

    x_lat, x_ctx = x, ctx
    for i in range(DEPTH):
        kind, j = i % N_MIXERS, i // N_MIXERS
        last = i == DEPTH - 1
        m_lat = mod_all[i, :bsz].reshape(bsz, N_MOD, dm)
        m_ctx = jnp.broadcast_to(mod_all[i, bsz].reshape(1, N_MOD, dm), (bsz, N_MOD, dm))

        if kind == 0:
            w = a_w_qkv[j].astype(BF)
            gains = (a_q_norm[j], a_k_norm[j])
            nq, nk = A_HEADS * HEAD_DIM // LANES, A_KV_HEADS * HEAD_DIM // LANES
            p_lat = _proj_qkv(x_lat, m_lat, w, n_q=nq, n_k=nk, n_v=nk, q_scale=scale, norm_gains=gains, rope=True)
            p_ctx = _proj_qkv(x_ctx, m_ctx, w, n_q=nq, n_k=nk, n_v=nk, q_scale=scale, norm_gains=gains)
            a_lat = _attn_gqa(p_lat, [p_ctx, p_lat])
            a_ctx = None if last else _attn_gqa(p_ctx, [p_ctx])
            w_o = a_w_o[j]
        elif kind == 1:
            w = b_w_qkv[j].astype(BF)
            nb = 2 * B_HEADS * HEAD_DIM // LANES
            p_lat = _proj_qkv(x_lat, m_lat, w, n_q=nb, n_k=nb, n_v=nb, q_scale=scale, rope=True)
            p_ctx = _proj_qkv(x_ctx, m_ctx, w, n_q=nb, n_k=nb, n_v=nb, q_scale=scale)
            lam_vecs = jnp.stack([b_lambda_q1[j], b_lambda_k1[j], b_lambda_q2[j], b_lambda_k2[j]])
            lam_init = 0.8 - 0.6 * math.exp(-0.3 * i)
            a_lat = _attn_diff(p_lat, [p_ctx, p_lat], lam_vecs, b_subln[j], lam_init)
            a_ctx = None if last else _attn_diff(p_ctx, [p_ctx], lam_vecs, b_subln[j], lam_init)
            w_o = b_w_o[j]
        elif kind == 2:
            w_in = c_w_in[j]
            n_main = w_in.shape[1] - 2 * C_GATE_RANK
            w_main = w_in[:, :n_main].astype(BF)
            w_z = jnp.pad(w_in[:, n_main:], ((0, 0), (0, LANES - 2 * C_GATE_RANK))).astype(BF)
            kd = c_w_gate_fwd.shape[-1]
            w_gate = jnp.zeros((LANES, 2 * kd), F32)
            w_gate = w_gate.at[:C_GATE_RANK, :kd].set(c_w_gate_fwd[j])
            w_gate = w_gate.at[C_GATE_RANK:2 * C_GATE_RANK, kd:].set(c_w_gate_bwd[j]).astype(BF)
            b_gate = jnp.concatenate([c_b_gate_fwd[j], c_b_gate_bwd[j]]).reshape(1, 2 * kd)
            q_scale = (kd // C_HEADS) ** -0.5
            nq = kd // LANES
            p_lat, g_lat = _proj_gla(x_lat, m_lat, w_main, w_z, w_gate, b_gate, q_scale=q_scale, n_q=nq)
            p_ctx, g_ctx = _proj_gla(x_ctx, m_ctx, w_main, w_z, w_gate, b_gate, q_scale=q_scale, n_q=nq)
            a_lat, a_ctx = _gla(p_lat, g_lat, p_ctx, g_ctx, c_norm[j], not last)
            w_o = c_w_o[j]
        else:
            w = d_w_qkv[j].astype(BF)
            nb = D_HEADS * HEAD_DIM // LANES
            p_lat = _proj_qkv(x_lat, m_lat, w, n_q=nb, n_k=nb, n_v=nb, q_scale=scale)
            p_ctx = _proj_qkv(x_ctx, m_ctx, w, n_q=nb, n_k=nb, n_v=nb, q_scale=scale)
            bias = _nb_bias_table(d_rpb[j], n_lat // GRID_W)
            a_lat = _attn_nb(p_lat, p_ctx, bias)
            a_ctx = None if last else _attn_ctx(p_ctx, nb)
            w_o = d_w_o[j]

        w_o = w_o.astype(BF)
        w_up = ffn_w_up[i].astype(BF)
        w_dn = ffn_w_down[i].astype(BF)
        ffn = (w_o, ln1_g[i], ln1_b[i], w_up, ffn_conv_w[i], ffn_conv_b[i], w_dn, ln2_g[i], ln2_b[i])
        x_lat = _mix_ffn(a_lat, x_lat, m_lat, *ffn)
        if not last:
            x_ctx = _mix_ffn(a_ctx, x_ctx, m_ctx, *ffn)
    return x_lat
```
